```python
import math
import jax, jax.numpy as jnp
from jax import lax
import numpy as np


D_MODEL = 1024
BATCH = 8
SEQ = 4096
DEPTH = 1

GRID_W = 64
CTX_LEN = 256
D_MIX = D_MODEL
D_S5 = D_MIX // 2
D_LRU = D_MIX - D_S5
S5_GROUP = 16
S5_GROUPS = D_S5 // S5_GROUP
S5_STATE = 64
S5_DT_MIN = 0.001
S5_DT_MAX = 0.1
LRU_HEADS = 8
LRU_HEAD_DIM = D_LRU // LRU_HEADS
LRU_C = 8.0
CONV_W = 4
CONV_LEFT = 2
N_EXPERTS = 64
TOP_K = 8
N_GROUPS = 8
TOPK_GROUPS = 4
D_EXPERT = 256
D_SHARED = 256
ROUTED_SCALE = 2.5
MOE_BLOCK = 256
ALPHA = (2.0 * DEPTH) ** 0.25
BETA = (8.0 * DEPTH) ** -0.25
LN_EPS = 1e-5

kernel_name = 'hybrid_s5_rglru_moe_dit'


def layer_norm(x, g, b):
    xf = x.astype(jnp.float32)
    mu = xf.mean(-1, keepdims=True)
    var = jnp.square(xf - mu).mean(-1, keepdims=True)
    return ((xf - mu) * lax.rsqrt(var + LN_EPS) * g + b).astype(x.dtype)


def _lin_combine(e1, e2):
    a1, b1 = e1
    a2, b2 = e2
    return a1 * a2, a2 * b1 + b2


def linear_scan(a, b, h0, reverse):
    if h0 is not None:
        idx = -1 if reverse else 0
        b = b.at[:, idx].add(a[:, idx] * h0)
    return lax.associative_scan(_lin_combine, (a, b), reverse=reverse, axis=1)[1]


def _cplx_combine(e1, e2):
    a1r, a1i, b1r, b1i = e1
    a2r, a2i, b2r, b2i = e2
    return (a2r * a1r - a2i * a1i, a2r * a1i + a2i * a1r,
            a2r * b1r - a2i * b1i + b2r, a2r * b1i + a2i * b1r + b2i)


def s5_discretize(lam_re, lam_im, log_dt, b_re, b_im):
    f32 = jnp.float32
    lam_re = lam_re.astype(f32)
    lam_im = lam_im.astype(f32)
    dt = jnp.exp(log_dt.astype(f32))[:, None]
    mag = jnp.exp(lam_re * dt)
    ar = mag * jnp.cos(lam_im * dt)
    ai = mag * jnp.sin(lam_im * dt)
    den = lam_re * lam_re + lam_im * lam_im
    nr = ar - 1.0
    fr = ((nr * lam_re + ai * lam_im) / den)[..., None]
    fi = ((ai * lam_re - nr * lam_im) / den)[..., None]
    b_re = b_re.astype(f32)
    b_im = b_im.astype(f32)
    return ar, ai, fr * b_re - fi * b_im, fr * b_im + fi * b_re


def s5_states(u, ar, ai, bbr, bbi, h0, reverse):
    br = jnp.einsum('blgc,gpc->blgp', u, bbr)
    bi = jnp.einsum('blgc,gpc->blgp', u, bbi)
    n = u.shape[1]
    a_r = jnp.broadcast_to(ar[None, None], (1, n) + ar.shape)
    a_i = jnp.broadcast_to(ai[None, None], (1, n) + ai.shape)
    if h0 is not None:
        idx = -1 if reverse else 0
        hr, hi = h0
        br = br.at[:, idx].add(ar * hr - ai * hi)
        bi = bi.at[:, idx].add(ar * hi + ai * hr)
    _, _, xr, xi = lax.associative_scan(_cplx_combine, (a_r, a_i, br, bi), reverse=reverse, axis=1)
    return xr, xi


def s5_readout(xr, xi, c_re, c_im):
    return jnp.einsum('blgp,gcp->blgc', xr, c_re) - jnp.einsum('blgp,gcp->blgc', xi, c_im)


def s5_mixer(ul, uc, lam_re, lam_im, log_dt, b_re, b_im, c_re, c_im, d, w_glu, b_glu, need_ctx):
    f32 = jnp.float32
    bsz, seq, _ = ul.shape
    ul = ul.astype(f32).reshape(bsz, seq, S5_GROUPS, S5_GROUP)
    uc = uc.astype(f32).reshape(bsz, uc.shape[1], S5_GROUPS, S5_GROUP)
    dg = d.astype(f32).reshape(S5_GROUPS, S5_GROUP)
    y_lat = dg * ul
    y_ctx = dg * uc if need_ctx else None
    for dr, rev in ((0, False), (1, True)):
        ar, ai, bbr, bbi = s5_discretize(lam_re[dr], lam_im[dr], log_dt[dr], b_re[dr], b_im[dr])
        cr, ci = s5_states(uc, ar, ai, bbr, bbi, None, rev)
        end = 0 if rev else -1
        lr, li = s5_states(ul, ar, ai, bbr, bbi, (cr[:, end], ci[:, end]), rev)
        cre = c_re[dr].astype(f32)
        cim = c_im[dr].astype(f32)
        y_lat = y_lat + s5_readout(lr, li, cre, cim)
        if need_ctx:
            y_ctx = y_ctx + s5_readout(cr, ci, cre, cim)

    def glu(y):
        y = jax.nn.gelu(y.reshape(y.shape[0], y.shape[1], D_S5))
        return y * jax.nn.sigmoid(y @ w_glu + b_glu)

    return glu(y_lat), (glu(y_ctx) if need_ctx else None)


def centred_dwconv(x, w, bias):
    n = x.shape[1]
    pad = [(0, 0)] * x.ndim
    pad[1] = (CONV_LEFT, CONV_W - 1 - CONV_LEFT)
    xp = jnp.pad(x, pad)
    out = bias + w[0] * lax.slice_in_dim(xp, 0, n, axis=1)
    for k in range(1, CONV_W):
        out = out + w[k] * lax.slice_in_dim(xp, k, k + n, axis=1)
    return out


def rglru_coeffs(x, w_a, b_a, w_x, b_x, lam):
    bsz, n, _ = x.shape
    xh = x.reshape(bsz, n, LRU_HEADS, LRU_HEAD_DIM)
    r = jax.nn.sigmoid(jnp.einsum('blhi,hij->blhj', xh, w_a).reshape(bsz, n, D_LRU) + b_a)
    i = jax.nn.sigmoid(jnp.einsum('blhi,hij->blhj', xh, w_x).reshape(bsz, n, D_LRU) + b_x)
    log_a = -LRU_C * r * jax.nn.softplus(-lam.astype(jnp.float32))
    a = jnp.exp(log_a)
    b = jnp.sqrt(-jnp.expm1(2.0 * log_a)) * (i * x)
    return a, b


def lru_mixer(xl, xc, conv_w, conv_b, w_a, b_a, w_x, b_x, lam, rows, need_ctx):
    f32 = jnp.float32
    bsz, seq, ch = xl.shape
    grid = centred_dwconv(xl.astype(f32).reshape(bsz, rows, GRID_W, ch), conv_w, conv_b)
    xs = grid.transpose(0, 2, 1, 3).reshape(bsz, seq, ch)
    xcs = centred_dwconv(xc.astype(f32), conv_w, conv_b)
    h_lat = None
    h_ctx = None
    for dr, rev in ((0, False), (1, True)):
        a_c, b_c = rglru_coeffs(xcs, w_a[dr], b_a[dr], w_x[dr], b_x[dr], lam[dr])
        hc = linear_scan(a_c, b_c, None, rev)
        a_l, b_l = rglru_coeffs(xs, w_a[dr], b_a[dr], w_x[dr], b_x[dr], lam[dr])
        hl = linear_scan(a_l, b_l, hc[:, 0 if rev else -1], rev)
        h_lat = hl if h_lat is None else h_lat + hl
        if need_ctx:
            h_ctx = hc if h_ctx is None else h_ctx + hc
    h_lat = h_lat.reshape(bsz, GRID_W, rows, ch).transpose(0, 2, 1, 3).reshape(bsz, seq, ch)
    return h_lat, h_ctx


def swiglu(v, w_gate, w_up, w_down):
    return (jax.nn.silu(v @ w_gate) * (v @ w_up)) @ w_down


def moe(v, router_w, router_bias, w_gate, w_up, w_down, sh_gate, sh_up, sh_down):
    n_tok, dm = v.shape
    scores = jax.nn.sigmoid(v.astype(jnp.float32) @ router_w.astype(jnp.float32))
    sel = scores + router_bias.astype(jnp.float32)
    grp = sel.reshape(n_tok, N_GROUPS, N_EXPERTS // N_GROUPS)
    grp_score = lax.top_k(grp, 2)[0].sum(-1)
    _, top_groups = lax.top_k(grp_score, TOPK_GROUPS)
    group_mask = jnp.any(top_groups[..., None] == jnp.arange(N_GROUPS), axis=1)
    expert_mask = jnp.repeat(group_mask, N_EXPERTS // N_GROUPS, axis=1)
    _, top_idx = lax.top_k(jnp.where(expert_mask, sel, -jnp.inf), TOP_K)
    wts = jnp.take_along_axis(scores, top_idx, axis=1)
    wts = wts / wts.sum(-1, keepdims=True) * ROUTED_SCALE

    tk = n_tok * TOP_K
    flat_e = top_idx.reshape(-1)
    flat_tok = jnp.arange(tk, dtype=jnp.int32) // TOP_K
    order = jnp.argsort(flat_e)
    e_sorted = flat_e[order]
    tok_sorted = flat_tok[order]
    w_sorted = wts.reshape(-1)[order]
    counts = jnp.bincount(flat_e, length=N_EXPERTS)
    padded = (counts + MOE_BLOCK - 1) // MOE_BLOCK * MOE_BLOCK
    pad_end = jnp.cumsum(padded)
    pad_start = pad_end - padded
    start = jnp.cumsum(counts) - counts
    dest = pad_start[e_sorted] + (jnp.arange(tk) - start[e_sorted])
    n_blocks = -(-(tk + N_EXPERTS * (MOE_BLOCK - 1)) // MOE_BLOCK)
    n_pad = n_blocks * MOE_BLOCK
    buf_tok = jnp.full((n_pad,), n_tok, jnp.int32).at[dest].set(tok_sorted)
    buf_w = jnp.zeros((n_pad,), jnp.float32).at[dest].set(w_sorted)
    block_e = jnp.minimum(jnp.searchsorted(pad_end, jnp.arange(n_blocks) * MOE_BLOCK, side='right'),
                          N_EXPERTS - 1)
    v_pad = jnp.concatenate([v, jnp.zeros((1, dm), v.dtype)], axis=0)

    def step(acc, inp):
        tok, wt, e = inp
        xb = v_pad[tok]
        out = swiglu(xb, w_gate[e], w_up[e], w_down[e]) * wt[:, None]
        return acc.at[tok].add(out.astype(acc.dtype)), None

    acc, _ = lax.scan(step, jnp.zeros((n_tok + 1, dm), v.dtype),
                      (buf_tok.reshape(n_blocks, MOE_BLOCK), buf_w.reshape(n_blocks, MOE_BLOCK), block_e))
    return acc[:n_tok] + swiglu(v, sh_gate, sh_up, sh_down)


def setup_inputs(seed: int = 0) -> dict:
    key = jax.random.key(seed)
    ks = iter(jax.random.split(key, 64))
    f32 = jnp.float32

    def nrm(shape, std):
        return std * jax.random.normal(next(ks), shape, f32)

    L = DEPTH
    G, P = S5_GROUPS, S5_STATE
    hd = LRU_HEAD_DIM
    n_idx = jnp.arange(P, dtype=f32)
    a_c = jax.random.uniform(next(ks), (L, 2, D_LRU), f32, 0.9, 0.999)
    sig = a_c ** (1.0 / LRU_C)
    return {
        'x': nrm((BATCH, SEQ, D_MODEL), 1.0),
        'c': nrm((BATCH, D_MODEL), 1.0),
        'ctx': nrm((BATCH, CTX_LEN, D_MODEL), 1.0),
        'c_ctx': nrm((D_MODEL,), 1.0),
        'ln_in_g': 1.0 + nrm((D_MODEL,), 0.02),
        'ln_in_b': nrm((D_MODEL,), 0.02),
        'ada_w': nrm((L, D_MODEL, 6 * D_MODEL), D_MODEL ** -0.5),
        'ada_b': nrm((L, 6 * D_MODEL), 0.02),
        'w_in': nrm((L, D_MODEL, D_S5 + 2 * D_LRU), D_MODEL ** -0.5),
        's5_lam_re': -0.5 + nrm((L, 2, G, P), 0.01),
        's5_lam_im': math.pi * n_idx + nrm((L, 2, G, P), 0.01),
        's5_log_dt': jax.random.uniform(next(ks), (L, 2, G), f32, math.log(S5_DT_MIN), math.log(S5_DT_MAX)),
        's5_b_re': nrm((L, 2, G, P, S5_GROUP), (2 * S5_GROUP) ** -0.5),
        's5_b_im': nrm((L, 2, G, P, S5_GROUP), (2 * S5_GROUP) ** -0.5),
        's5_c_re': nrm((L, 2, G, S5_GROUP, P), P ** -0.5),
        's5_c_im': nrm((L, 2, G, S5_GROUP, P), P ** -0.5),
        's5_d': nrm((L, D_S5), 1.0),
        's5_w_glu': nrm((L, D_S5, D_S5), D_S5 ** -0.5),
        's5_b_glu': nrm((L, D_S5), 0.02),
        'lru_conv_w': nrm((L, CONV_W, D_LRU), CONV_W ** -0.5),
        'lru_conv_b': nrm((L, D_LRU), 0.02),
        'lru_w_a': nrm((L, 2, LRU_HEADS, hd, hd), hd ** -0.5),
        'lru_b_a': nrm((L, 2, D_LRU), 0.02),
        'lru_w_x': nrm((L, 2, LRU_HEADS, hd, hd), hd ** -0.5),
        'lru_b_x': nrm((L, 2, D_LRU), 0.02),
        'lru_lam': jnp.log(sig) - jnp.log1p(-sig),
        'w_out': nrm((L, D_MIX, D_MODEL), BETA * D_MIX ** -0.5),
        'ln1_g': 1.0 + nrm((L, D_MODEL), 0.02),
        'ln1_b': nrm((L, D_MODEL), 0.02),
        'router_w': nrm((L, D_MODEL, N_EXPERTS), D_MODEL ** -0.5),
        'router_bias': nrm((L, N_EXPERTS), 0.01),
        'exp_w_gate': nrm((L, N_EXPERTS, D_MODEL, D_EXPERT), D_MODEL ** -0.5),
        'exp_w_up': nrm((L, N_EXPERTS, D_MODEL, D_EXPERT), D_MODEL ** -0.5),
        'exp_w_down': nrm((L, N_EXPERTS, D_EXPERT, D_MODEL), BETA * D_EXPERT ** -0.5),
        'sh_w_gate': nrm((L, D_MODEL, D_SHARED), D_MODEL ** -0.5),
        'sh_w_up': nrm((L, D_MODEL, D_SHARED), D_MODEL ** -0.5),
        'sh_w_down': nrm((L, D_SHARED, D_MODEL), BETA * D_SHARED ** -0.5),
        'ln2_g': 1.0 + nrm((L, D_MODEL), 0.02),
        'ln2_b': nrm((L, D_MODEL), 0.02),
    }


def reference(x, c, ctx, c_ctx, ln_in_g, ln_in_b, ada_w, ada_b, w_in,
              s5_lam_re, s5_lam_im, s5_log_dt, s5_b_re, s5_b_im, s5_c_re, s5_c_im,
              s5_d, s5_w_glu, s5_b_glu,
              lru_conv_w, lru_conv_b, lru_w_a, lru_b_a, lru_w_x, lru_b_x, lru_lam,
              w_out, ln1_g, ln1_b, router_w, router_bias,
              exp_w_gate, exp_w_up, exp_w_down, sh_w_gate, sh_w_up, sh_w_down,
              ln2_g, ln2_b):
    bsz, seq, dm = x.shape
    rows = seq // GRID_W
    h = layer_norm(x, ln_in_g, ln_in_b)
    hc = layer_norm(ctx, ln_in_g, ln_in_b)
    s_lat = jax.nn.silu(c)
    s_ctx = jax.nn.silu(c_ctx)
    o1, o2 = D_S5, D_S5 + D_LRU
    for l in range(DEPTH):
        last = l == DEPTH - 1
        sh1, sc1, g1, sh2, sc2, g2 = jnp.split((s_lat @ ada_w[l] + ada_b[l])[:, None, :], 6, axis=-1)
        csh1, csc1, cg1, csh2, csc2, cg2 = jnp.split(s_ctx @ ada_w[l] + ada_b[l], 6, axis=-1)
        u = (h * (1.0 + sc1) + sh1) @ w_in[l]
        uc = (hc * (1.0 + csc1) + csh1) @ w_in[l]
        s5_lat, s5_ctx = s5_mixer(u[..., :o1], uc[..., :o1], s5_lam_re[l], s5_lam_im[l], s5_log_dt[l],
                                  s5_b_re[l], s5_b_im[l], s5_c_re[l], s5_c_im[l], s5_d[l],
                                  s5_w_glu[l], s5_b_glu[l], not last)
        lru_lat, lru_ctx = lru_mixer(u[..., o1:o2], uc[..., o1:o2], lru_conv_w[l], lru_conv_b[l],
                                     lru_w_a[l], lru_b_a[l], lru_w_x[l], lru_b_x[l], lru_lam[l],
                                     rows, not last)
        y = jnp.concatenate([s5_lat, lru_lat * jax.nn.gelu(u[..., o2:])], axis=-1) @ w_out[l]
        h = layer_norm(ALPHA * h + g1 * y, ln1_g[l], ln1_b[l])
        if not last:
            yc = jnp.concatenate([s5_ctx, lru_ctx * jax.nn.gelu(uc[..., o2:])], axis=-1) @ w_out[l]
            hc = layer_norm(ALPHA * hc + cg1 * yc, ln1_g[l], ln1_b[l])
        v = (h * (1.0 + sc2) + sh2).reshape(bsz * seq, dm)
        moe_args = (router_w[l], router_bias[l], exp_w_gate[l], exp_w_up[l], exp_w_down[l],
                    sh_w_gate[l], sh_w_up[l], sh_w_down[l])
        if last:
            f = moe(v, *moe_args).reshape(bsz, seq, dm)
        else:
            vc = (hc * (1.0 + csc2) + csh2).reshape(-1, dm)
            fall = moe(jnp.concatenate([v, vc.astype(v.dtype)], axis=0), *moe_args)
            f = fall[:bsz * seq].reshape(bsz, seq, dm)
            fc = fall[bsz * seq:].reshape(hc.shape)
            hc = layer_norm(ALPHA * hc + cg2 * fc, ln2_g[l], ln2_b[l])
        h = layer_norm(ALPHA * h + g2 * f, ln2_g[l], ln2_b[l])
    return h.astype(x.dtype)
```

```python
import functools
import math

import jax
import jax.numpy as jnp
from jax import lax
from jax.experimental import pallas as pl
from jax.experimental.pallas import tpu as pltpu

F32 = jnp.float32
BF16 = jnp.bfloat16
I32 = jnp.int32
HIGHEST = lax.Precision.HIGHEST

GRID_W = 64
S5_GROUP = 16
S5_STATE = 64
S5_CHUNK = 16
LRU_HEADS = 8
LRU_C = 8.0
CONV_W = 4
CONV_LEFT = 2
N_EXPERTS = 64
TOP_K = 8
N_GROUPS = 8
TOPK_GROUPS = 4
ROUTED_SCALE = 2.5
LN_EPS = 1e-5
DEPTH = 1
ALPHA = (2.0 * DEPTH) ** 0.25

ROW_BLOCK = 256
VMEM_LIMIT = 52 * 1024 * 1024


def _cparams(sem):
    return pltpu.CompilerParams(dimension_semantics=sem, vmem_limit_bytes=VMEM_LIMIT)


def _ln(x, g, b):
    mu = jnp.mean(x, axis=-1, keepdims=True)
    xc = x - mu
    var = jnp.mean(xc * xc, axis=-1, keepdims=True)
    return xc * lax.rsqrt(var + LN_EPS) * g + b


def _gelu(x):
    return x * (0.5 * (1.0 + jnp.tanh(math.sqrt(2.0 / math.pi) * (x + 0.044715 * (x * x * x)))))


def _silu(x):
    return x * jax.nn.sigmoid(x)


def _softplus(x):
    return jnp.maximum(x, 0.0) + jnp.log1p(jnp.exp(-jnp.abs(x)))


def _dot(a, b):
    return jnp.dot(a, b, preferred_element_type=F32)


def _dot_nt(a, b, precision=None):
    return lax.dot_general(a, b, (((1,), (1,)), ((), ())), precision=precision,
                           preferred_element_type=F32)


def _ada_kernel(c_ref, w_ref, b_ref, o_ref):
    s = _silu(c_ref[...])
    o_ref[...] = jnp.dot(s, w_ref[...], precision=HIGHEST, preferred_element_type=F32) + b_ref[...]


def _ada(cc, ada_w, ada_b):
    r, d = cc.shape
    n = ada_w.shape[1]
    tn = 512
    return pl.pallas_call(
        _ada_kernel,
        grid=(n // tn,),
        in_specs=[pl.BlockSpec((r, d), lambda j: (0, 0)),
                  pl.BlockSpec((d, tn), lambda j: (0, j)),
                  pl.BlockSpec((1, tn), lambda j: (0, j))],
        out_specs=pl.BlockSpec((r, tn), lambda j: (0, j)),
        out_shape=jax.ShapeDtypeStruct((r, n), F32),
        compiler_params=_cparams(("parallel",)),
        name="ada",
    )(cc, ada_w, ada_b.reshape(1, n))


def _in_proj_kernel(x_ref, g_ref, b_ref, sc_ref, sh_ref, w_ref, us5_ref, ulru_ref):
    h = _ln(x_ref[...], g_ref[...], b_ref[...])
    m = h * (1.0 + sc_ref[...]) + sh_ref[...]
    u = _dot(m.astype(BF16), w_ref[...])
    ds5 = us5_ref.shape[-1]
    us5_ref[...] = u[:, :ds5].astype(BF16)
    ulru_ref[...] = u[:, ds5:].astype(BF16)


def _in_proj(x2d, ln_g, ln_b, sc, sh, w_bf16, rows_per_mod, d_s5, tm=512):
    t, d = x2d.shape
    n = w_bf16.shape[1]
    tm = min(tm, t)
    mod_spec = pl.BlockSpec((None, 1, d), lambda i: ((i * tm) // rows_per_mod, 0, 0))
    vec = pl.BlockSpec((1, d), lambda i: (0, 0))
    return pl.pallas_call(
        _in_proj_kernel,
        grid=(t // tm,),
        in_specs=[pl.BlockSpec((tm, d), lambda i: (i, 0)), vec, vec, mod_spec, mod_spec,
                  pl.BlockSpec((d, n), lambda i: (0, 0))],
        out_specs=[pl.BlockSpec((tm, d_s5), lambda i: (i, 0)),
                   pl.BlockSpec((tm, n - d_s5), lambda i: (i, 0))],
        out_shape=[jax.ShapeDtypeStruct((t, d_s5), BF16),
                   jax.ShapeDtypeStruct((t, n - d_s5), BF16)],
        compiler_params=_cparams(("parallel",)),
        name="in_proj",
    )(x2d, ln_g.reshape(1, d), ln_b.reshape(1, d), sc, sh, w_bf16)


def _s5_prep_kernel(lam_ref, bre_ref, bim_ref, cre_ref, cim_ref, d_ref,
                    tt_ref, win_ref, wout_ref, a16_ref, cp_scr):
    p, lc, gs = S5_STATE, S5_CHUNK, S5_GROUP
    nl = 4 * p
    lre = lam_ref[0:1, :]
    lim = lam_ref[1:2, :]
    dt = jnp.exp(lam_ref[2:3, :])
    lane = lax.broadcasted_iota(I32, (1, nl), 1)
    is_re = lane < 2 * p
    is_f = (lane % (2 * p)) < p

    def powers(ef, eb):
        e = jnp.where(is_f, float(ef), float(eb))
        mag = jnp.exp(lre * dt * e)
        ang = lim * dt * e
        return mag * jnp.cos(ang), mag * jnp.sin(ang)

    bre, bim = bre_ref[...], bim_ref[...]
    cre, cim = cre_ref[...], cim_ref[...]
    ar, ai = powers(1, 1)
    den = lre * lre + lim * lim
    nr = ar - 1.0
    fr = (nr * lre + ai * lim) / den
    fi = (ai * lre - nr * lim) / den
    bbre = fr * bre - fi * bim
    bbim = fr * bim + fi * bre

    def c_times(pc, ps):
        return cre * jnp.where(is_re, pc, -ps) + cim * jnp.where(is_re, -ps, -pc)

    for s in range(lc):
        pc, ps = powers(lc - 1 - s, s)
        qa = jnp.where(is_re, pc, ps)
        qb = jnp.where(is_re, -ps, pc)
        win_ref[gs * s:gs * (s + 1), :] = (bbre * qa + bbim * qb).astype(BF16)
    for t in range(lc):
        pc, ps = powers(t + 1, lc - t)
        wout_ref[gs * t:gs * (t + 1), :] = c_times(pc, ps).astype(BF16)
    for j in range(lc):
        pc, ps = powers(j, lc - 1 - j)
        cp_scr[gs * j:gs * (j + 1), :] = c_times(pc, ps)

    bbcat = jnp.where(is_re, bbre, bbim)
    cp = cp_scr[...]
    ktf = _dot_nt(jnp.where(is_f, bbcat, 0.0), cp, HIGHEST)
    ktb = _dot_nt(jnp.where(is_f, 0.0, bbcat), cp, HIGHEST)
    lane2 = lax.broadcasted_iota(I32, (gs, gs * lc), 1)
    row2 = lax.broadcasted_iota(I32, (gs, gs * lc), 0)
    dcol = d_ref[...]
    width = gs * lc
    for s in range(lc):
        sf = gs * s
        tf = ktf if sf == 0 else pltpu.roll(ktf, sf, 1)
        tf = jnp.where(lane2 >= sf, tf, 0.0)
        sb = (gs * (s + 1)) % width
        tb = ktb if sb == 0 else pltpu.roll(ktb, sb, 1)
        tb = jnp.where(lane2 < gs * (s + 1), tb, 0.0)
        skip = jnp.where(lane2 == sf + row2, dcol, 0.0)
        tt_ref[gs * s:gs * (s + 1), :] = (tf + tb + skip).astype(BF16)
    pc, ps = powers(lc, lc)
    a16_ref[0:1, :] = pc[:, :2 * p]
    a16_ref[1:2, :] = ps[:, :2 * p]


def _s5_prep(lam3, b4re, b4im, c4re, c4im, dcol):
    g = lam3.shape[0]
    nl = 4 * S5_STATE
    k = S5_GROUP * S5_CHUNK
    m3 = lambda i: (i, 0, 0)
    return pl.pallas_call(
        _s5_prep_kernel,
        grid=(g,),
        in_specs=[pl.BlockSpec((None, 3, nl), m3)]
        + [pl.BlockSpec((None, S5_GROUP, nl), m3)] * 4
        + [pl.BlockSpec((None, S5_GROUP, 1), m3)],
        out_specs=[pl.BlockSpec((None, k, k), m3), pl.BlockSpec((None, k, nl), m3),
                   pl.BlockSpec((None, k, nl), m3), pl.BlockSpec((None, 2, nl // 2), m3)],
        out_shape=[jax.ShapeDtypeStruct((g, k, k), BF16), jax.ShapeDtypeStruct((g, k, nl), BF16),
                   jax.ShapeDtypeStruct((g, k, nl), BF16), jax.ShapeDtypeStruct((g, 2, nl // 2), F32)],
        scratch_shapes=[pltpu.VMEM((k, nl), F32)],
        compiler_params=_cparams(("parallel",)),
        name="s5_prep",
    )(lam3, b4re, b4im, c4re, c4im, dcol)


def _s5_kernel(xl_ref, xc_ref, tt_ref, win_ref, wout_ref, a16_ref, y_ref,
               z_scr, zc_scr, sf_scr, sb_scr, *, nkl, nkc, bsz):
    h = 2 * S5_STATE
    xl = xl_ref[...]
    win = win_ref[...]
    z_scr[...] = _dot(xl, win)
    zc_scr[...] = _dot(xc_ref[...], win)
    is_f = lax.broadcasted_iota(I32, (bsz, h), 1) < S5_STATE
    ar = a16_ref[0:1, :]
    ai = a16_ref[1:2, :]

    def pick(z_ref, i, n):
        zf = z_ref[pl.ds(i * bsz, bsz), :]
        zb = z_ref[pl.ds((n - 1 - i) * bsz, bsz), :]
        return jnp.where(is_f, zf[:, :h], zb[:, :h]), jnp.where(is_f, zf[:, h:], zb[:, h:])

    def update(sr, si, zr, zi):
        return ar * sr - ai * si + zr, ar * si + ai * sr + zi

    def ctx_step(i, carry):
        zr, zi = pick(zc_scr, i, nkc)
        return update(*carry, zr, zi)

    def lat_step(i, carry):
        sr, si = carry
        st = jnp.concatenate([sr, si], axis=1)
        sf_scr[pl.ds(i * bsz, bsz), :] = st
        sb_scr[pl.ds((nkl - 1 - i) * bsz, bsz), :] = st
        zr, zi = pick(z_scr, i, nkl)
        return update(sr, si, zr, zi)

    zero = jnp.zeros((bsz, h), F32)
    carry = lax.fori_loop(0, nkc, ctx_step, (zero, zero))
    lax.fori_loop(0, nkl, lat_step, carry)

    is_f4 = (lax.broadcasted_iota(I32, (1, 2 * h), 1) % h) < S5_STATE
    xs = jnp.where(is_f4, sf_scr[...], sb_scr[...]).astype(BF16)
    y = _dot(xl, tt_ref[...]) + _dot_nt(xs, wout_ref[...])
    y_ref[...] = y.astype(BF16)


def _s5(xl, xc, tt, win, wout, a16, bsz):
    g, nl_rows, k = xl.shape
    nc_rows = xc.shape[1]
    nl = 4 * S5_STATE
    m3 = lambda i: (i, 0, 0)
    kern = functools.partial(_s5_kernel, nkl=nl_rows // bsz, nkc=nc_rows // bsz, bsz=bsz)
    return pl.pallas_call(
        kern,
        grid=(g,),
        in_specs=[pl.BlockSpec((None, nl_rows, k), m3), pl.BlockSpec((None, nc_rows, k), m3),
                  pl.BlockSpec((None, k, k), m3), pl.BlockSpec((None, k, nl), m3),
                  pl.BlockSpec((None, k, nl), m3), pl.BlockSpec((None, 2, nl // 2), m3)],
        out_specs=pl.BlockSpec((None, nl_rows, k), m3),
        out_shape=jax.ShapeDtypeStruct((g, nl_rows, k), BF16),
        scratch_shapes=[pltpu.VMEM((nl_rows, nl), F32), pltpu.VMEM((nc_rows, nl), F32),
                        pltpu.VMEM((nl_rows, nl), F32), pltpu.VMEM((nl_rows, nl), F32)],
        compiler_params=_cparams(("parallel",)),
        name="s5",
    )(xl, xc, tt, win, wout, a16)


def _lru_kernel(xv_ref, xg_ref, xc_ref, cw_ref, cb_ref, wg_ref, bg_ref, lam_ref, o_ref,
                a_scr, b_scr, ac_scr, bc_scr, cin_scr, *, rows, cl, slab_block):
    w = GRID_W
    c = xv_ref.shape[-1]
    cw = cw_ref[...]
    cb = cb_ref[...]
    sp = _softplus(-lam_ref[...])
    bg = bg_ref[...]
    wg = wg_ref[...]

    def coeffs(xs, store):
        g = _dot(xs.astype(BF16), wg) + bg
        for d in range(2):
            r = jax.nn.sigmoid(g[:, (2 * d) * c:(2 * d + 1) * c])
            i = jax.nn.sigmoid(g[:, (2 * d + 1) * c:(2 * d + 2) * c])
            log_a = -LRU_C * r * sp[d:d + 1, :]
            a = jnp.exp(log_a)
            b = jnp.sqrt(-jnp.tanh(log_a) * (a * a + 1.0)) * (i * xs)
            store(d, a, b)

    def tap(r0, r1, off):
        lo, hi = r0 + off, r1 + off
        clo, chi = max(lo, 0), min(hi, rows)
        parts = []
        if clo > lo:
            parts.append(jnp.zeros((clo - lo, w, c), F32))
        if chi > clo:
            parts.append(xv_ref[clo:chi].astype(F32))
        if hi > chi:
            parts.append(jnp.zeros((hi - chi, w, c), F32))
        return parts[0] if len(parts) == 1 else jnp.concatenate(parts, axis=0)

    for r0 in range(0, rows, slab_block):
        r1 = r0 + slab_block
        xs3 = cb
        for k in range(CONV_W):
            xs3 = xs3 + cw[k:k + 1, :] * tap(r0, r1, k - CONV_LEFT)
        xs = xs3.reshape(slab_block * w, c)

        def store_lat(d, a, b, r0=r0, r1=r1):
            a_scr[d, r0 * w:r1 * w, :] = a
            b_scr[d, r0 * w:r1 * w, :] = b

        coeffs(xs, store_lat)

    xc = xc_ref[...].astype(F32)
    trow = lax.broadcasted_iota(I32, (cl, c), 0)
    xcs = cb
    for k in range(CONV_W):
        off = k - CONV_LEFT
        sh = xc if off == 0 else pltpu.roll(xc, (-off) % cl, 0)
        ok = (trow + off >= 0) & (trow + off < cl)
        xcs = xcs + cw[k:k + 1, :] * jnp.where(ok, sh, 0.0)

    def store_ctx(d, a, b):
        ac_scr[d] = a
        bc_scr[d] = b

    coeffs(xcs, store_ctx)

    for d in range(2):
        rev = d == 1

        def ctx_step(j, h, d=d, rev=rev):
            idx = (cl - 1 - j) if rev else j
            return ac_scr[d, pl.ds(idx, 1), :] * h + bc_scr[d, pl.ds(idx, 1), :]

        h0 = lax.fori_loop(0, cl, ctx_step, jnp.zeros((1, c), F32))

        def col_step(j, carry, d=d, rev=rev):
            h, p = carry
            r = (rows - 1 - j) if rev else j
            off = pl.multiple_of(r * w, w)
            a = a_scr[d, pl.ds(off, w), :]
            h = a * h + b_scr[d, pl.ds(off, w), :]
            p = a * p
            b_scr[d, pl.ds(off, w), :] = h
            a_scr[d, pl.ds(off, w), :] = p
            return h, p

        lax.fori_loop(0, rows, col_step, (jnp.zeros((w, c), F32), jnp.ones((w, c), F32)))

        last = 0 if rev else (rows - 1) * w

        def carry_step(j, cin, d=d, rev=rev, last=last):
            col = (w - 1 - j) if rev else j
            cin_scr[pl.ds(col, 1), :] = cin
            return a_scr[d, pl.ds(last + col, 1), :] * cin + b_scr[d, pl.ds(last + col, 1), :]

        lax.fori_loop(0, w, carry_step, h0)
        cin = cin_scr[...]

        if not rev:
            def fix_step(r, _, d=d, cin=cin):
                off = pl.multiple_of(r * w, w)
                b_scr[d, pl.ds(off, w), :] = b_scr[d, pl.ds(off, w), :] + a_scr[d, pl.ds(off, w), :] * cin
                return 0

            lax.fori_loop(0, rows, fix_step, 0)
        else:
            def out_step(r, _, cin=cin):
                off = pl.multiple_of(r * w, w)
                hsum = (b_scr[0, pl.ds(off, w), :] + b_scr[1, pl.ds(off, w), :]
                        + a_scr[1, pl.ds(off, w), :] * cin)
                o_ref[r] = (hsum * _gelu(xg_ref[r].astype(F32))).astype(o_ref.dtype)
                return 0

            lax.fori_loop(0, rows, out_step, 0)


def _lru(u4, uc3, conv_w, conv_b, wg, bg, lam, c_blk=256):
    bsz, rows, w, n2 = u4.shape
    d_lru = n2 // 2
    cl = uc3.shape[1]
    ncb = d_lru // c_blk
    kern = functools.partial(_lru_kernel, rows=rows, cl=cl, slab_block=8)
    n = rows * w
    return pl.pallas_call(
        kern,
        grid=(bsz, ncb),
        in_specs=[pl.BlockSpec((None, rows, w, c_blk), lambda b, j: (b, 0, 0, j)),
                  pl.BlockSpec((None, rows, w, c_blk), lambda b, j: (b, 0, 0, ncb + j)),
                  pl.BlockSpec((None, cl, c_blk), lambda b, j: (b, 0, j)),
                  pl.BlockSpec((CONV_W, c_blk), lambda b, j: (0, j)),
                  pl.BlockSpec((1, c_blk), lambda b, j: (0, j)),
                  pl.BlockSpec((None, c_blk, 4 * c_blk), lambda b, j: (j, 0, 0)),
                  pl.BlockSpec((None, 1, 4 * c_blk), lambda b, j: (j, 0, 0)),
                  pl.BlockSpec((2, c_blk), lambda b, j: (0, j))],
        out_specs=pl.BlockSpec((None, rows, w, c_blk), lambda b, j: (b, 0, 0, j)),
        out_shape=jax.ShapeDtypeStruct((bsz, rows, w, d_lru), BF16),
        scratch_shapes=[pltpu.VMEM((2, n, c_blk), F32), pltpu.VMEM((2, n, c_blk), F32),
                        pltpu.VMEM((2, cl, c_blk), F32), pltpu.VMEM((2, cl, c_blk), F32),
                        pltpu.VMEM((w, c_blk), F32)],
        compiler_params=_cparams(("parallel", "parallel")),
        name="lru",
    )(u4, u4, uc3, conv_w, conv_b, wg, bg, lam)


def _mix_kernel(ys5_ref, ylru_ref, x_ref, lng_ref, lnb_ref, g1_ref, sc2_ref, sh2_ref,
                wglu_ref, bglu_ref, wo1_ref, wo2_ref, l1g_ref, l1b_ref, rwt_ref, rb_ref, tri_ref,
                h1_ref, v_ref, eidx_ref, rank_ref, wts_ref, cnt_ref, carry_scr):
    i = pl.program_id(0)

    @pl.when(i == 0)
    def _():
        carry_scr[...] = jnp.zeros_like(carry_scr)

    y = _gelu(ys5_ref[...].astype(F32))
    s5o = y * jax.nn.sigmoid(_dot(y.astype(BF16), wglu_ref[...]) + bglu_ref[...])
    y1 = _dot(s5o.astype(BF16), wo1_ref[...]) + _dot(ylru_ref[...], wo2_ref[...])
    h = _ln(x_ref[...], lng_ref[...], lnb_ref[...])
    h1 = _ln(ALPHA * h + g1_ref[...] * y1, l1g_ref[...], l1b_ref[...])
    h1_ref[...] = h1
    v = h1 * (1.0 + sc2_ref[...]) + sh2_ref[...]
    v_ref[...] = v

    tm = v.shape[0]
    ne, ng, gsz = N_EXPERTS, N_GROUPS, N_EXPERTS // N_GROUPS
    scores = jax.nn.sigmoid(_dot_nt(rwt_ref[...], v, HIGHEST))
    s3 = scores.reshape(ng, gsz, tm)
    sel3 = (scores + rb_ref[...]).reshape(ng, gsz, tm)
    ii = lax.broadcasted_iota(I32, (ng, gsz, tm), 1)
    gi = lax.broadcasted_iota(I32, (ng, gsz, tm), 0)
    neg = -jnp.inf

    m1 = jnp.max(sel3, axis=1, keepdims=True)
    f1 = jnp.min(jnp.where(sel3 == m1, ii, gsz), axis=1, keepdims=True)
    m2 = jnp.max(jnp.where(ii == f1, neg, sel3), axis=1, keepdims=True)
    cur = m1 + m2
    gidx = lax.broadcasted_iota(I32, (ng, 1, tm), 0)
    gmask = jnp.zeros((ng, 1, tm), jnp.bool_)
    for _ in range(TOPK_GROUPS):
        mx = jnp.max(cur, axis=0, keepdims=True)
        fg = jnp.min(jnp.where(cur == mx, gidx, ng), axis=0, keepdims=True)
        hit = gidx == fg
        gmask = gmask | hit
        cur = jnp.where(hit, neg, cur)

    selm = jnp.where(gmask, sel3, neg)
    eid = gi * gsz + ii
    picks, erows, wrows = [], [], []
    for _ in range(TOP_K):
        mx = jnp.max(jnp.max(selm, axis=1, keepdims=True), axis=0, keepdims=True)
        fe = jnp.min(jnp.min(jnp.where(selm == mx, eid, ne), axis=1, keepdims=True),
                     axis=0, keepdims=True)
        hit = eid == fe
        selm = jnp.where(hit, neg, selm)
        picks.append(hit)
        erows.append(fe.reshape(1, tm))
        wsel = jnp.where(hit, s3, 0.0)
        wrows.append(jnp.sum(jnp.sum(wsel, axis=1, keepdims=True), axis=0).reshape(1, tm))
    denom = wrows[0]
    for k in range(1, TOP_K):
        denom = denom + wrows[k]

    chosen = picks[0]
    for k in range(1, TOP_K):
        chosen = chosen | picks[k]
    chosen_f = jnp.where(chosen, 1.0, 0.0).reshape(ne, tm)
    carry = carry_scr[...]
    cnt3 = (_dot(chosen_f.astype(BF16), tri_ref[...]) + carry).reshape(ng, gsz, tm)
    rrows = []
    for k in range(TOP_K):
        rsel = jnp.where(picks[k], cnt3, 0.0)
        rrows.append(jnp.sum(jnp.sum(rsel, axis=1, keepdims=True), axis=0).reshape(1, tm))
    new_carry = carry + jnp.sum(chosen_f, axis=1, keepdims=True)
    carry_scr[...] = new_carry
    cnt_ref[...] = new_carry.astype(I32)

    eidx_ref[...] = jnp.concatenate(erows, axis=0)
    rank_ref[...] = jnp.concatenate(rrows, axis=0).astype(I32)
    wts_ref[...] = jnp.concatenate([wr / denom * ROUTED_SCALE for wr in wrows], axis=0)


def _mix(ys5, ylru, x2d, ln_g, ln_b, g1, sc2, sh2, wglu, bglu, wo1, wo2, l1g, l1b, rwt, rb, seq, tm=512):
    t, d = x2d.shape
    ds = ys5.shape[1]
    ne = rwt.shape[0]
    tri = (lax.broadcasted_iota(I32, (tm, tm), 0) < lax.broadcasted_iota(I32, (tm, tm), 1)).astype(BF16)
    row = lambda n: pl.BlockSpec((tm, n), lambda i: (i, 0))
    vec = lambda n: pl.BlockSpec((1, n), lambda i: (0, 0))
    mod = pl.BlockSpec((None, 1, d), lambda i: ((i * tm) // seq, 0, 0))
    full = lambda a, b: pl.BlockSpec((a, b), lambda i: (0, 0))
    tok = pl.BlockSpec((TOP_K, tm), lambda i: (0, i))
    return pl.pallas_call(
        _mix_kernel,
        grid=(t // tm,),
        in_specs=[row(ds), row(ds), row(d), vec(d), vec(d), mod, mod, mod,
                  full(ds, ds), vec(ds), full(ds, d), full(ds, d), vec(d), vec(d),
                  full(ne, d), full(ne, 1), full(tm, tm)],
        out_specs=[row(d), row(d), tok, tok, tok, full(ne, 1)],
        out_shape=[jax.ShapeDtypeStruct((t, d), F32), jax.ShapeDtypeStruct((t, d), F32),
                   jax.ShapeDtypeStruct((TOP_K, t), I32), jax.ShapeDtypeStruct((TOP_K, t), I32),
                   jax.ShapeDtypeStruct((TOP_K, t), F32), jax.ShapeDtypeStruct((ne, 1), I32)],
        scratch_shapes=[pltpu.VMEM((ne, 1), F32)],
        compiler_params=_cparams(("arbitrary",)),
        name="mix",
    )(ys5, ylru, x2d, ln_g.reshape(1, d), ln_b.reshape(1, d), g1, sc2, sh2,
      wglu, bglu.reshape(1, ds), wo1, wo2, l1g.reshape(1, d), l1b.reshape(1, d), rwt, rb, tri)


def _dest_kernel(start_ref, eidx_ref, rank_ref, dest_ref):
    e = eidx_ref[...]
    acc = rank_ref[...]
    for j in range(N_EXPERTS):
        acc = acc + jnp.where(e == j, start_ref[j], 0)
    dest_ref[...] = acc


def _dest(start, eidx, rank, tn=2048):
    k, t = eidx.shape
    tn = min(tn, t)
    tok = pl.BlockSpec((k, tn), lambda i, s: (0, i))
    return pl.pallas_call(
        _dest_kernel,
        grid_spec=pltpu.PrefetchScalarGridSpec(num_scalar_prefetch=1, grid=(t // tn,),
                                               in_specs=[tok, tok], out_specs=tok),
        out_shape=jax.ShapeDtypeStruct((k, t), I32),
        compiler_params=_cparams(("parallel",)),
        name="dest",
    )(start, eidx, rank)


def _row_copy_wait(hbm_ref, vmem_ref, sem, rows):
    pltpu.make_async_copy(hbm_ref.at[pl.ds(0, rows), :], vmem_ref, sem).wait()


def _dispatch_kernel(dest_ref, v_ref, xs_ref, sem):
    tm = v_ref.shape[0]

    def body(t, c):
        for k in range(TOP_K):
            pltpu.make_async_copy(v_ref.at[pl.ds(t, 1), :],
                                  xs_ref.at[pl.ds(dest_ref[k, t], 1), :], sem).start()
        return c

    lax.fori_loop(0, tm, body, 0)
    for _ in range(TOP_K):
        _row_copy_wait(xs_ref, v_ref, sem, tm)


def _dispatch(dest, v, tm=256):
    t, d = v.shape
    return pl.pallas_call(
        _dispatch_kernel,
        grid=(t // tm,),
        in_specs=[pl.BlockSpec((TOP_K, tm), lambda i: (0, i), memory_space=pltpu.SMEM),
                  pl.BlockSpec((tm, d), lambda i: (i, 0))],
        out_specs=pl.BlockSpec(memory_space=pl.ANY),
        out_shape=jax.ShapeDtypeStruct((t * TOP_K, d), F32),
        scratch_shapes=[pltpu.SemaphoreType.DMA],
        compiler_params=_cparams(("arbitrary",)),
        name="dispatch",
    )(dest, v)


def _gmm_kernel(vb_ref, ve_ref, vlo_ref, vhi_ref, x_ref, wg_ref, wu_ref, wd_ref, y_ref):
    j = pl.program_id(0)
    x = x_ref[...].astype(BF16)
    g = _dot(x, wg_ref[...].astype(BF16))
    u = _dot(x, wu_ref[...].astype(BF16))
    y = _dot((_silu(g) * u).astype(BF16), wd_ref[...].astype(BF16))
    rows = vb_ref[j] * ROW_BLOCK + lax.broadcasted_iota(I32, (ROW_BLOCK, 1), 0)
    mine = (rows >= vlo_ref[j]) & (rows < vhi_ref[j])
    first = jnp.logical_or(j == 0, vb_ref[jnp.maximum(j - 1, 0)] != vb_ref[j])

    @pl.when(first)
    def _():
        y_ref[...] = jnp.where(mine, y, 0.0)

    @pl.when(jnp.logical_not(first))
    def _():
        y_ref[...] = jnp.where(mine, y, y_ref[...])


def _gmm(vblock, vexp, vlo, vhi, xs, wg, wu, wd):
    n, d = xs.shape
    de = wg.shape[2]
    nv = vblock.shape[0]
    return pl.pallas_call(
        _gmm_kernel,
        grid_spec=pltpu.PrefetchScalarGridSpec(
            num_scalar_prefetch=4, grid=(nv,),
            in_specs=[pl.BlockSpec((ROW_BLOCK, d), lambda j, vb, ve, lo, hi: (vb[j], 0)),
                      pl.BlockSpec((None, d, de), lambda j, vb, ve, lo, hi: (ve[j], 0, 0)),
                      pl.BlockSpec((None, d, de), lambda j, vb, ve, lo, hi: (ve[j], 0, 0)),
                      pl.BlockSpec((None, de, d), lambda j, vb, ve, lo, hi: (ve[j], 0, 0))],
            out_specs=pl.BlockSpec((ROW_BLOCK, d), lambda j, vb, ve, lo, hi: (vb[j], 0))),
        out_shape=jax.ShapeDtypeStruct((n, d), F32),
        compiler_params=_cparams(("arbitrary",)),
        name="gmm",
    )(vblock, vexp, vlo, vhi, xs, wg, wu, wd)


def _visits(counts, n_rows):
    ne = counts.shape[0]
    n_blocks = n_rows // ROW_BLOCK
    nv = n_blocks + ne - 1
    end = jnp.cumsum(counts)
    start = end - counts
    first = start // ROW_BLOCK
    nvis = jnp.where(counts > 0, (end - 1) // ROW_BLOCK - first + 1, 0)
    vend = jnp.cumsum(nvis)
    vstart = vend - nvis
    j = jnp.arange(nv, dtype=I32)
    e = jnp.minimum(jnp.searchsorted(vend, j, side="right"), ne - 1).astype(I32)
    live = j < vend[ne - 1]
    blk = jnp.where(live, first[e] + (j - vstart[e]), n_blocks - 1).astype(I32)
    lo = jnp.where(live, start[e], 0).astype(I32)
    hi = jnp.where(live, end[e], 0).astype(I32)
    return blk, e, lo, hi, start.astype(I32)


def _combine_kernel(dest_ref, wts_ref, v_ref, h1_ref, g2_ref, shg_ref, shu_ref, shd_ref,
                    l2g_ref, l2b_ref, ys_ref, o_ref, gbuf, sem):
    tm = v_ref.shape[0]

    def body(t, c):
        for k in range(TOP_K):
            pltpu.make_async_copy(ys_ref.at[pl.ds(dest_ref[k, t], 1), :],
                                  gbuf.at[k, pl.ds(t, 1), :], sem).start()
        return c

    lax.fori_loop(0, tm, body, 0)

    vb = v_ref[...].astype(BF16)
    hidden = _silu(_dot(vb, shg_ref[...])) * _dot(vb, shu_ref[...])
    f = _dot(hidden.astype(BF16), shd_ref[...])

    for k in range(TOP_K):
        _row_copy_wait(ys_ref, gbuf.at[k], sem, tm)
    wts = wts_ref[...]
    for k in range(TOP_K):
        f = f + wts[:, k:k + 1] * gbuf[k]
    o_ref[...] = _ln(ALPHA * h1_ref[...] + g2_ref[...] * f, l2g_ref[...], l2b_ref[...])


def _combine(dest, wts_t, v, h1, g2, shg, shu, shd, l2g, l2b, ys, seq, tm=256):
    t, d = v.shape
    dsh = shg.shape[1]
    row = pl.BlockSpec((tm, d), lambda i: (i, 0))
    vec = pl.BlockSpec((1, d), lambda i: (0, 0))
    return pl.pallas_call(
        _combine_kernel,
        grid=(t // tm,),
        in_specs=[pl.BlockSpec((TOP_K, tm), lambda i: (0, i), memory_space=pltpu.SMEM),
                  pl.BlockSpec((tm, TOP_K), lambda i: (i, 0)), row, row,
                  pl.BlockSpec((None, 1, d), lambda i: ((i * tm) // seq, 0, 0)),
                  pl.BlockSpec((d, dsh), lambda i: (0, 0)), pl.BlockSpec((d, dsh), lambda i: (0, 0)),
                  pl.BlockSpec((dsh, d), lambda i: (0, 0)), vec, vec,
                  pl.BlockSpec(memory_space=pl.ANY)],
        out_specs=row,
        out_shape=jax.ShapeDtypeStruct((t, d), F32),
        scratch_shapes=[pltpu.VMEM((TOP_K, tm, d), F32), pltpu.SemaphoreType.DMA],
        compiler_params=_cparams(("arbitrary",)),
        name="combine",
    )(dest, wts_t, v, h1, g2, shg, shu, shd, l2g.reshape(1, d), l2b.reshape(1, d), ys)


def _quad(a0, a1):
    return jnp.concatenate([a0, a1, a0, a1], axis=-1)


def kernel(x, c, ctx, c_ctx, ln_in_g, ln_in_b, ada_w, ada_b, w_in, s5_lam_re, s5_lam_im, s5_log_dt, s5_b_re, s5_b_im, s5_c_re, s5_c_im, s5_d, s5_w_glu, s5_b_glu, lru_conv_w, lru_conv_b, lru_w_a, lru_b_a, lru_w_x, lru_b_x, lru_lam, w_out, ln1_g, ln1_b, router_w, router_bias, exp_w_gate, exp_w_up, exp_w_down, sh_w_gate, sh_w_up, sh_w_down, ln2_g, ln2_b):
    bsz, seq, dm = x.shape
    cl = ctx.shape[1]
    assert ada_w.shape[0] == DEPTH
    d_s5 = s5_w_glu.shape[1]
    d_lru = lru_lam.shape[2]
    ngrp = d_s5 // S5_GROUP
    rows = seq // GRID_W
    t = bsz * seq
    lc = S5_CHUNK
    nkl, nkc = seq // lc, cl // lc

    pad = (-(bsz + 1)) % 8
    cc = jnp.concatenate([c, c_ctx[None, :], jnp.zeros((pad, dm), F32)], axis=0)
    mods = _ada(cc, ada_w[0], ada_b[0])
    sh1, sc1, g1, sh2, sc2, g2 = [mods[:bsz, k * dm:(k + 1) * dm].reshape(bsz, 1, dm) for k in range(6)]
    csh1, csc1 = [mods[bsz:bsz + 1, k * dm:(k + 1) * dm].reshape(1, 1, dm) for k in range(2)]

    x2d = x.reshape(t, dm)
    w_in_b = w_in[0].astype(BF16)
    us5, ulru = _in_proj(x2d, ln_in_g, ln_in_b, sc1, sh1, w_in_b, seq, d_s5)
    ucs5, uclru = _in_proj(ctx.reshape(bsz * cl, dm), ln_in_g, ln_in_b, csc1, csh1, w_in_b, bsz * cl, d_s5)

    def to_chunks(u, nk):
        u = u.reshape(bsz, nk, lc, ngrp, S5_GROUP).transpose(3, 1, 0, 2, 4)
        return u.reshape(ngrp, nk * bsz, lc * S5_GROUP)

    lam3 = jnp.stack([_quad(s5_lam_re[0, 0], s5_lam_re[0, 1]), _quad(s5_lam_im[0, 0], s5_lam_im[0, 1]),
                      _quad(*[jnp.broadcast_to(s5_log_dt[0, k][:, None], (ngrp, S5_STATE)) for k in range(2)])],
                     axis=1)
    bt_re = jnp.swapaxes(s5_b_re[0], -1, -2)
    bt_im = jnp.swapaxes(s5_b_im[0], -1, -2)
    tt, win, wout, a16 = _s5_prep(lam3, _quad(bt_re[0], bt_re[1]), _quad(bt_im[0], bt_im[1]),
                                  _quad(s5_c_re[0, 0], s5_c_re[0, 1]), _quad(s5_c_im[0, 0], s5_c_im[0, 1]),
                                  s5_d[0].reshape(ngrp, S5_GROUP, 1))
    ys = _s5(to_chunks(us5, nkl), to_chunks(ucs5, nkc), tt, win, wout, a16, bsz)
    ys5 = ys.reshape(ngrp, nkl, bsz, lc, S5_GROUP).transpose(2, 1, 3, 0, 4).reshape(t, d_s5)

    c_blk = 256
    hd = d_lru // LRU_HEADS
    hpb = c_blk // hd
    ncb = d_lru // c_blk

    def blockdiag(wh):
        wh = wh.reshape(ncb, hpb, hd, hd)
        eye = jnp.eye(hpb, dtype=wh.dtype)
        return jnp.einsum("nhij,hk->nhikj", wh, eye).reshape(ncb, c_blk, c_blk)

    wg = jnp.concatenate([blockdiag(lru_w_a[0, 0]), blockdiag(lru_w_x[0, 0]),
                          blockdiag(lru_w_a[0, 1]), blockdiag(lru_w_x[0, 1])], axis=-1).astype(BF16)
    bgate = jnp.concatenate([lru_b_a[0, 0].reshape(ncb, 1, c_blk), lru_b_x[0, 0].reshape(ncb, 1, c_blk),
                             lru_b_a[0, 1].reshape(ncb, 1, c_blk), lru_b_x[0, 1].reshape(ncb, 1, c_blk)], axis=-1)
    ylru = _lru(ulru.reshape(bsz, rows, GRID_W, 2 * d_lru), uclru.reshape(bsz, cl, 2 * d_lru),
                lru_conv_w[0], lru_conv_b[0].reshape(1, d_lru), wg, bgate, lru_lam[0], c_blk)
    ylru = ylru.reshape(t, d_lru)

    w_out_b = w_out[0].astype(BF16)
    h1, v, eidx, rank, wts, counts = _mix(
        ys5, ylru, x2d, ln_in_g, ln_in_b, g1, sc2, sh2, s5_w_glu[0].astype(BF16), s5_b_glu[0],
        w_out_b[:d_s5], w_out_b[d_s5:], ln1_g[0], ln1_b[0], router_w[0].T,
        router_bias[0].reshape(N_EXPERTS, 1), seq)

    n_rows = t * TOP_K
    vblock, vexp, vlo, vhi, start = _visits(counts[:, 0], n_rows)
    dest = _dest(start, eidx, rank)
    xs = _dispatch(dest, v)
    ysort = _gmm(vblock, vexp, vlo, vhi, xs, exp_w_gate[0], exp_w_up[0], exp_w_down[0])
    out = _combine(dest, wts.T, v, h1, g2, sh_w_gate[0].astype(BF16), sh_w_up[0].astype(BF16),
                   sh_w_down[0].astype(BF16), ln2_g[0], ln2_b[0], ysort, seq)
    return out.reshape(bsz, seq, dm).astype(x.dtype)
```

```python
import functools
import math

import jax
import jax.numpy as jnp
from jax import lax
from jax.experimental import pallas as pl
from jax.experimental.pallas import tpu as pltpu

F32 = jnp.float32
BF16 = jnp.bfloat16
I32 = jnp.int32
HIGHEST = lax.Precision.HIGHEST

GRID_W = 64
S5_GROUP = 16
S5_STATE = 64
S5_CHUNK = 16
LRU_HEADS = 8
LRU_C = 8.0
CONV_W = 4
CONV_LEFT = 2
N_EXPERTS = 64
TOP_K = 8
N_GROUPS = 8
TOPK_GROUPS = 4
ROUTED_SCALE = 2.5
LN_EPS = 1e-5
DEPTH = 1
ALPHA = (2.0 * DEPTH) ** 0.25

ROW_BLOCK = 256
VMEM_LIMIT = 52 * 1024 * 1024


def _cparams(sem):
    return pltpu.CompilerParams(dimension_semantics=sem, vmem_limit_bytes=VMEM_LIMIT)


def _ln(x, g, b):
    mu = jnp.mean(x, axis=-1, keepdims=True)
    xc = x - mu
    var = jnp.mean(xc * xc, axis=-1, keepdims=True)
    return xc * lax.rsqrt(var + LN_EPS) * g + b


def _gelu(x):
    return x * (0.5 * (1.0 + jnp.tanh(math.sqrt(2.0 / math.pi) * (x + 0.044715 * (x * x * x)))))


def _silu(x):
    return x * jax.nn.sigmoid(x)


def _softplus(x):
    return jnp.maximum(x, 0.0) + jnp.log1p(jnp.exp(-jnp.abs(x)))


def _dot(a, b):
    return jnp.dot(a, b, preferred_element_type=F32)


def _dot_nt(a, b, precision=None):
    return lax.dot_general(a, b, (((1,), (1,)), ((), ())), precision=precision,
                           preferred_element_type=F32)


def _ada_kernel(c_ref, w_ref, b_ref, o_ref):
    s = _silu(c_ref[...])
    o_ref[...] = jnp.dot(s, w_ref[...], precision=HIGHEST, preferred_element_type=F32) + b_ref[...]


def _ada(cc, ada_w, ada_b):
    r, d = cc.shape
    n = ada_w.shape[1]
    tn = 512
    return pl.pallas_call(
        _ada_kernel,
        grid=(n // tn,),
        in_specs=[pl.BlockSpec((r, d), lambda j: (0, 0)),
                  pl.BlockSpec((d, tn), lambda j: (0, j)),
                  pl.BlockSpec((1, tn), lambda j: (0, j))],
        out_specs=pl.BlockSpec((r, tn), lambda j: (0, j)),
        out_shape=jax.ShapeDtypeStruct((r, n), F32),
        compiler_params=_cparams(("parallel",)),
        name="ada",
    )(cc, ada_w, ada_b.reshape(1, n))


def _in_proj_kernel(x_ref, g_ref, b_ref, sc_ref, sh_ref, w_ref, us5_ref, ulru_ref):
    h = _ln(x_ref[...], g_ref[...], b_ref[...])
    m = h * (1.0 + sc_ref[...]) + sh_ref[...]
    u = _dot(m.astype(BF16), w_ref[...])
    ds5 = us5_ref.shape[-1]
    us5_ref[...] = u[:, :ds5].astype(BF16)
    ulru_ref[...] = u[:, ds5:].astype(BF16)


def _in_proj(x2d, ln_g, ln_b, sc, sh, w_bf16, rows_per_mod, d_s5, tm=512):
    t, d = x2d.shape
    n = w_bf16.shape[1]
    tm = min(tm, t)
    mod_spec = pl.BlockSpec((None, 1, d), lambda i: ((i * tm) // rows_per_mod, 0, 0))
    vec = pl.BlockSpec((1, d), lambda i: (0, 0))
    return pl.pallas_call(
        _in_proj_kernel,
        grid=(t // tm,),
        in_specs=[pl.BlockSpec((tm, d), lambda i: (i, 0)), vec, vec, mod_spec, mod_spec,
                  pl.BlockSpec((d, n), lambda i: (0, 0))],
        out_specs=[pl.BlockSpec((tm, d_s5), lambda i: (i, 0)),
                   pl.BlockSpec((tm, n - d_s5), lambda i: (i, 0))],
        out_shape=[jax.ShapeDtypeStruct((t, d_s5), BF16),
                   jax.ShapeDtypeStruct((t, n - d_s5), BF16)],
        compiler_params=_cparams(("parallel",)),
        name="in_proj",
    )(x2d, ln_g.reshape(1, d), ln_b.reshape(1, d), sc, sh, w_bf16)


def _s5_prep_kernel(lam_ref, bre_ref, bim_ref, cre_ref, cim_ref, d_ref,
                    tt_ref, win_ref, wout_ref, a16_ref, cp_scr):
    p, lc, gs = S5_STATE, S5_CHUNK, S5_GROUP
    nl = 4 * p
    lre = lam_ref[0:1, :]
    lim = lam_ref[1:2, :]
    dt = jnp.exp(lam_ref[2:3, :])
    lane = lax.broadcasted_iota(I32, (1, nl), 1)
    is_re = lane < 2 * p
    is_f = (lane % (2 * p)) < p

    def powers(ef, eb):
        e = jnp.where(is_f, float(ef), float(eb))
        mag = jnp.exp(lre * dt * e)
        ang = lim * dt * e
        return mag * jnp.cos(ang), mag * jnp.sin(ang)

    bre, bim = bre_ref[...], bim_ref[...]
    cre, cim = cre_ref[...], cim_ref[...]
    ar, ai = powers(1, 1)
    den = lre * lre + lim * lim
    nr = ar - 1.0
    fr = (nr * lre + ai * lim) / den
    fi = (ai * lre - nr * lim) / den
    bbre = fr * bre - fi * bim
    bbim = fr * bim + fi * bre

    def c_times(pc, ps):
        return cre * jnp.where(is_re, pc, -ps) + cim * jnp.where(is_re, -ps, -pc)

    for s in range(lc):
        pc, ps = powers(lc - 1 - s, s)
        qa = jnp.where(is_re, pc, ps)
        qb = jnp.where(is_re, -ps, pc)
        win_ref[gs * s:gs * (s + 1), :] = (bbre * qa + bbim * qb).astype(BF16)
    for t in range(lc):
        pc, ps = powers(t + 1, lc - t)
        wout_ref[gs * t:gs * (t + 1), :] = c_times(pc, ps).astype(BF16)
    for j in range(lc):
        pc, ps = powers(j, lc - 1 - j)
        cp_scr[gs * j:gs * (j + 1), :] = c_times(pc, ps)

    bbcat = jnp.where(is_re, bbre, bbim)
    cp = cp_scr[...]
    ktf = _dot_nt(jnp.where(is_f, bbcat, 0.0), cp, HIGHEST)
    ktb = _dot_nt(jnp.where(is_f, 0.0, bbcat), cp, HIGHEST)
    lane2 = lax.broadcasted_iota(I32, (gs, gs * lc), 1)
    row2 = lax.broadcasted_iota(I32, (gs, gs * lc), 0)
    dcol = d_ref[...]
    width = gs * lc
    for s in range(lc):
        sf = gs * s
        tf = ktf if sf == 0 else pltpu.roll(ktf, sf, 1)
        tf = jnp.where(lane2 >= sf, tf, 0.0)
        sb = (gs * (s + 1)) % width
        tb = ktb if sb == 0 else pltpu.roll(ktb, sb, 1)
        tb = jnp.where(lane2 < gs * (s + 1), tb, 0.0)
        skip = jnp.where(lane2 == sf + row2, dcol, 0.0)
        tt_ref[gs * s:gs * (s + 1), :] = (tf + tb + skip).astype(BF16)
    pc, ps = powers(lc, lc)
    a16_ref[0:1, :] = pc[:, :2 * p]
    a16_ref[1:2, :] = ps[:, :2 * p]


def _s5_prep(lam3, b4re, b4im, c4re, c4im, dcol):
    g = lam3.shape[0]
    nl = 4 * S5_STATE
    k = S5_GROUP * S5_CHUNK
    m3 = lambda i: (i, 0, 0)
    return pl.pallas_call(
        _s5_prep_kernel,
        grid=(g,),
        in_specs=[pl.BlockSpec((None, 3, nl), m3)]
        + [pl.BlockSpec((None, S5_GROUP, nl), m3)] * 4
        + [pl.BlockSpec((None, S5_GROUP, 1), m3)],
        out_specs=[pl.BlockSpec((None, k, k), m3), pl.BlockSpec((None, k, nl), m3),
                   pl.BlockSpec((None, k, nl), m3), pl.BlockSpec((None, 2, nl // 2), m3)],
        out_shape=[jax.ShapeDtypeStruct((g, k, k), BF16), jax.ShapeDtypeStruct((g, k, nl), BF16),
                   jax.ShapeDtypeStruct((g, k, nl), BF16), jax.ShapeDtypeStruct((g, 2, nl // 2), F32)],
        scratch_shapes=[pltpu.VMEM((k, nl), F32)],
        compiler_params=_cparams(("parallel",)),
        name="s5_prep",
    )(lam3, b4re, b4im, c4re, c4im, dcol)


def _s5_kernel(xl_ref, xc_ref, tt_ref, win_ref, wout_ref, a16_ref, y_ref,
               z_scr, zc_scr, sf_scr, sb_scr, *, nkl, nkc, bsz):
    h = 2 * S5_STATE
    xl = xl_ref[...]
    win = win_ref[...]
    z_scr[...] = _dot(xl, win)
    zc_scr[...] = _dot(xc_ref[...], win)
    is_f = lax.broadcasted_iota(I32, (bsz, h), 1) < S5_STATE
    ar = a16_ref[0:1, :]
    ai = a16_ref[1:2, :]

    def pick(z_ref, i, n):
        zf = z_ref[pl.ds(i * bsz, bsz), :]
        zb = z_ref[pl.ds((n - 1 - i) * bsz, bsz), :]
        return jnp.where(is_f, zf[:, :h], zb[:, :h]), jnp.where(is_f, zf[:, h:], zb[:, h:])

    def update(sr, si, zr, zi):
        return ar * sr - ai * si + zr, ar * si + ai * sr + zi

    def ctx_step(i, carry):
        zr, zi = pick(zc_scr, i, nkc)
        return update(*carry, zr, zi)

    def lat_step(i, carry):
        sr, si = carry
        st = jnp.concatenate([sr, si], axis=1)
        sf_scr[pl.ds(i * bsz, bsz), :] = st
        sb_scr[pl.ds((nkl - 1 - i) * bsz, bsz), :] = st
        zr, zi = pick(z_scr, i, nkl)
        return update(sr, si, zr, zi)

    zero = jnp.zeros((bsz, h), F32)
    carry = lax.fori_loop(0, nkc, ctx_step, (zero, zero))
    lax.fori_loop(0, nkl, lat_step, carry)

    is_f4 = (lax.broadcasted_iota(I32, (1, 2 * h), 1) % h) < S5_STATE
    xs = jnp.where(is_f4, sf_scr[...], sb_scr[...]).astype(BF16)
    y = _dot(xl, tt_ref[...]) + _dot_nt(xs, wout_ref[...])
    y_ref[...] = y.astype(BF16)


def _s5(xl, xc, tt, win, wout, a16, bsz):
    g, nl_rows, k = xl.shape
    nc_rows = xc.shape[1]
    nl = 4 * S5_STATE
    m3 = lambda i: (i, 0, 0)
    kern = functools.partial(_s5_kernel, nkl=nl_rows // bsz, nkc=nc_rows // bsz, bsz=bsz)
    return pl.pallas_call(
        kern,
        grid=(g,),
        in_specs=[pl.BlockSpec((None, nl_rows, k), m3), pl.BlockSpec((None, nc_rows, k), m3),
                  pl.BlockSpec((None, k, k), m3), pl.BlockSpec((None, k, nl), m3),
                  pl.BlockSpec((None, k, nl), m3), pl.BlockSpec((None, 2, nl // 2), m3)],
        out_specs=pl.BlockSpec((None, nl_rows, k), m3),
        out_shape=jax.ShapeDtypeStruct((g, nl_rows, k), BF16),
        scratch_shapes=[pltpu.VMEM((nl_rows, nl), F32), pltpu.VMEM((nc_rows, nl), F32),
                        pltpu.VMEM((nl_rows, nl), F32), pltpu.VMEM((nl_rows, nl), F32)],
        compiler_params=_cparams(("parallel",)),
        name="s5",
    )(xl, xc, tt, win, wout, a16)


def _lru_kernel(xv_ref, xg_ref, xc_ref, cw_ref, cb_ref, wg_ref, bg_ref, lam_ref, o_ref,
                a_scr, b_scr, ac_scr, bc_scr, cin_scr, *, rows, cl, slab_block):
    w = GRID_W
    c = xv_ref.shape[-1]
    cw = cw_ref[...]
    cb = cb_ref[...]
    sp = _softplus(-lam_ref[...])
    bg = bg_ref[...]
    wg = wg_ref[...]

    def coeffs(xs, store):
        g = _dot(xs.astype(BF16), wg) + bg
        for d in range(2):
            r = jax.nn.sigmoid(g[:, (2 * d) * c:(2 * d + 1) * c])
            i = jax.nn.sigmoid(g[:, (2 * d + 1) * c:(2 * d + 2) * c])
            log_a = -LRU_C * r * sp[d:d + 1, :]
            a = jnp.exp(log_a)
            b = jnp.sqrt(-jnp.tanh(log_a) * (a * a + 1.0)) * (i * xs)
            store(d, a, b)

    def tap(r0, r1, off):
        lo, hi = r0 + off, r1 + off
        clo, chi = max(lo, 0), min(hi, rows)
        parts = []
        if clo > lo:
            parts.append(jnp.zeros((clo - lo, w, c), F32))
        if chi > clo:
            parts.append(xv_ref[clo:chi].astype(F32))
        if hi > chi:
            parts.append(jnp.zeros((hi - chi, w, c), F32))
        return parts[0] if len(parts) == 1 else jnp.concatenate(parts, axis=0)

    for r0 in range(0, rows, slab_block):
        r1 = r0 + slab_block
        xs3 = cb
        for k in range(CONV_W):
            xs3 = xs3 + cw[k:k + 1, :] * tap(r0, r1, k - CONV_LEFT)
        xs = xs3.reshape(slab_block * w, c)

        def store_lat(d, a, b, r0=r0, r1=r1):
            a_scr[d, r0 * w:r1 * w, :] = a
            b_scr[d, r0 * w:r1 * w, :] = b

        coeffs(xs, store_lat)

    xc = xc_ref[...].astype(F32)
    trow = lax.broadcasted_iota(I32, (cl, c), 0)
    xcs = cb
    for k in range(CONV_W):
        off = k - CONV_LEFT
        sh = xc if off == 0 else pltpu.roll(xc, (-off) % cl, 0)
        ok = (trow + off >= 0) & (trow + off < cl)
        xcs = xcs + cw[k:k + 1, :] * jnp.where(ok, sh, 0.0)

    def store_ctx(d, a, b):
        ac_scr[d] = a
        bc_scr[d] = b

    coeffs(xcs, store_ctx)

    for d in range(2):
        rev = d == 1

        def ctx_step(j, h, d=d, rev=rev):
            idx = (cl - 1 - j) if rev else j
            return ac_scr[d, pl.ds(idx, 1), :] * h + bc_scr[d, pl.ds(idx, 1), :]

        h0 = lax.fori_loop(0, cl, ctx_step, jnp.zeros((1, c), F32))

        def col_step(j, carry, d=d, rev=rev):
            h, p = carry
            r = (rows - 1 - j) if rev else j
            off = pl.multiple_of(r * w, w)
            a = a_scr[d, pl.ds(off, w), :]
            h = a * h + b_scr[d, pl.ds(off, w), :]
            p = a * p
            b_scr[d, pl.ds(off, w), :] = h
            a_scr[d, pl.ds(off, w), :] = p
            return h, p

        lax.fori_loop(0, rows, col_step, (jnp.zeros((w, c), F32), jnp.ones((w, c), F32)))

        last = 0 if rev else (rows - 1) * w

        def carry_step(j, cin, d=d, rev=rev, last=last):
            col = (w - 1 - j) if rev else j
            cin_scr[pl.ds(col, 1), :] = cin
            return a_scr[d, pl.ds(last + col, 1), :] * cin + b_scr[d, pl.ds(last + col, 1), :]

        lax.fori_loop(0, w, carry_step, h0)
        cin = cin_scr[...]

        if not rev:
            def fix_step(r, _, d=d, cin=cin):
                off = pl.multiple_of(r * w, w)
                b_scr[d, pl.ds(off, w), :] = b_scr[d, pl.ds(off, w), :] + a_scr[d, pl.ds(off, w), :] * cin
                return 0

            lax.fori_loop(0, rows, fix_step, 0)
        else:
            def out_step(r, _, cin=cin):
                off = pl.multiple_of(r * w, w)
                hsum = (b_scr[0, pl.ds(off, w), :] + b_scr[1, pl.ds(off, w), :]
                        + a_scr[1, pl.ds(off, w), :] * cin)
                o_ref[r] = (hsum * _gelu(xg_ref[r].astype(F32))).astype(o_ref.dtype)
                return 0

            lax.fori_loop(0, rows, out_step, 0)


def _lru(u4, uc3, conv_w, conv_b, wg, bg, lam, c_blk=256):
    bsz, rows, w, n2 = u4.shape
    d_lru = n2 // 2
    cl = uc3.shape[1]
    ncb = d_lru // c_blk
    kern = functools.partial(_lru_kernel, rows=rows, cl=cl, slab_block=8)
    n = rows * w
    return pl.pallas_call(
        kern,
        grid=(bsz, ncb),
        in_specs=[pl.BlockSpec((None, rows, w, c_blk), lambda b, j: (b, 0, 0, j)),
                  pl.BlockSpec((None, rows, w, c_blk), lambda b, j: (b, 0, 0, ncb + j)),
                  pl.BlockSpec((None, cl, c_blk), lambda b, j: (b, 0, j)),
                  pl.BlockSpec((CONV_W, c_blk), lambda b, j: (0, j)),
                  pl.BlockSpec((1, c_blk), lambda b, j: (0, j)),
                  pl.BlockSpec((None, c_blk, 4 * c_blk), lambda b, j: (j, 0, 0)),
                  pl.BlockSpec((None, 1, 4 * c_blk), lambda b, j: (j, 0, 0)),
                  pl.BlockSpec((2, c_blk), lambda b, j: (0, j))],
        out_specs=pl.BlockSpec((None, rows, w, c_blk), lambda b, j: (b, 0, 0, j)),
        out_shape=jax.ShapeDtypeStruct((bsz, rows, w, d_lru), BF16),
        scratch_shapes=[pltpu.VMEM((2, n, c_blk), F32), pltpu.VMEM((2, n, c_blk), F32),
                        pltpu.VMEM((2, cl, c_blk), F32), pltpu.VMEM((2, cl, c_blk), F32),
                        pltpu.VMEM((w, c_blk), F32)],
        compiler_params=_cparams(("parallel", "parallel")),
        name="lru",
    )(u4, u4, uc3, conv_w, conv_b, wg, bg, lam)


def _mix_kernel(ys5_ref, ylru_ref, x_ref, lng_ref, lnb_ref, g1_ref, sc2_ref, sh2_ref,
                wglu_ref, bglu_ref, wo1_ref, wo2_ref, l1g_ref, l1b_ref, rwt_ref, rb_ref, tri_ref,
                h1_ref, v_ref, eidx_ref, rank_ref, wts_ref, cnt_ref, carry_scr):
    i = pl.program_id(0)

    @pl.when(i == 0)
    def _():
        carry_scr[...] = jnp.zeros_like(carry_scr)

    y = _gelu(ys5_ref[...].astype(F32))
    s5o = y * jax.nn.sigmoid(_dot(y.astype(BF16), wglu_ref[...]) + bglu_ref[...])
    y1 = _dot(s5o.astype(BF16), wo1_ref[...]) + _dot(ylru_ref[...], wo2_ref[...])
    h = _ln(x_ref[...], lng_ref[...], lnb_ref[...])
    h1 = _ln(ALPHA * h + g1_ref[...] * y1, l1g_ref[...], l1b_ref[...])
    h1_ref[...] = h1
    v = h1 * (1.0 + sc2_ref[...]) + sh2_ref[...]
    v_ref[...] = v

    tm = v.shape[0]
    ne, ng, gsz = N_EXPERTS, N_GROUPS, N_EXPERTS // N_GROUPS
    scores = jax.nn.sigmoid(_dot_nt(rwt_ref[...], v, HIGHEST))
    s3 = scores.reshape(ng, gsz, tm)
    sel3 = (scores + rb_ref[...]).reshape(ng, gsz, tm)
    ii = lax.broadcasted_iota(I32, (ng, gsz, tm), 1)
    gi = lax.broadcasted_iota(I32, (ng, gsz, tm), 0)
    neg = -jnp.inf

    m1 = jnp.max(sel3, axis=1, keepdims=True)
    f1 = jnp.min(jnp.where(sel3 == m1, ii, gsz), axis=1, keepdims=True)
    m2 = jnp.max(jnp.where(ii == f1, neg, sel3), axis=1, keepdims=True)
    cur = m1 + m2
    gidx = lax.broadcasted_iota(I32, (ng, 1, tm), 0)
    gmask = jnp.zeros((ng, 1, tm), jnp.bool_)
    for _ in range(TOPK_GROUPS):
        mx = jnp.max(cur, axis=0, keepdims=True)
        fg = jnp.min(jnp.where(cur == mx, gidx, ng), axis=0, keepdims=True)
        hit = gidx == fg
        gmask = gmask | hit
        cur = jnp.where(hit, neg, cur)

    selm = jnp.where(gmask, sel3, neg)
    eid = gi * gsz + ii
    picks, erows, wrows = [], [], []
    for _ in range(TOP_K):
        mx = jnp.max(jnp.max(selm, axis=1, keepdims=True), axis=0, keepdims=True)
        fe = jnp.min(jnp.min(jnp.where(selm == mx, eid, ne), axis=1, keepdims=True),
                     axis=0, keepdims=True)
        hit = eid == fe
        selm = jnp.where(hit, neg, selm)
        picks.append(hit)
        erows.append(fe.reshape(1, tm))
        wsel = jnp.where(hit, s3, 0.0)
        wrows.append(jnp.sum(jnp.sum(wsel, axis=1, keepdims=True), axis=0).reshape(1, tm))
    denom = wrows[0]
    for k in range(1, TOP_K):
        denom = denom + wrows[k]

    chosen = picks[0]
    for k in range(1, TOP_K):
        chosen = chosen | picks[k]
    chosen_f = jnp.where(chosen, 1.0, 0.0).reshape(ne, tm)
    carry = carry_scr[...]
    cnt3 = (_dot(chosen_f.astype(BF16), tri_ref[...]) + carry).reshape(ng, gsz, tm)
    rrows = []
    for k in range(TOP_K):
        rsel = jnp.where(picks[k], cnt3, 0.0)
        rrows.append(jnp.sum(jnp.sum(rsel, axis=1, keepdims=True), axis=0).reshape(1, tm))
    new_carry = carry + jnp.sum(chosen_f, axis=1, keepdims=True)
    carry_scr[...] = new_carry
    cnt_ref[...] = new_carry.astype(I32)

    eidx_ref[...] = jnp.concatenate(erows, axis=0)
    rank_ref[...] = jnp.concatenate(rrows, axis=0).astype(I32)
    wts_ref[...] = jnp.concatenate([wr / denom * ROUTED_SCALE for wr in wrows], axis=0)


def _mix(ys5, ylru, x2d, ln_g, ln_b, g1, sc2, sh2, wglu, bglu, wo1, wo2, l1g, l1b, rwt, rb, seq, tm=512):
    t, d = x2d.shape
    ds = ys5.shape[1]
    ne = rwt.shape[0]
    tri = (lax.broadcasted_iota(I32, (tm, tm), 0) < lax.broadcasted_iota(I32, (tm, tm), 1)).astype(BF16)
    row = lambda n: pl.BlockSpec((tm, n), lambda i: (i, 0))
    vec = lambda n: pl.BlockSpec((1, n), lambda i: (0, 0))
    mod = pl.BlockSpec((None, 1, d), lambda i: ((i * tm) // seq, 0, 0))
    full = lambda a, b: pl.BlockSpec((a, b), lambda i: (0, 0))
    tok = pl.BlockSpec((TOP_K, tm), lambda i: (0, i))
    return pl.pallas_call(
        _mix_kernel,
        grid=(t // tm,),
        in_specs=[row(ds), row(ds), row(d), vec(d), vec(d), mod, mod, mod,
                  full(ds, ds), vec(ds), full(ds, d), full(ds, d), vec(d), vec(d),
                  full(ne, d), full(ne, 1), full(tm, tm)],
        out_specs=[row(d), row(d), tok, tok, tok, full(ne, 1)],
        out_shape=[jax.ShapeDtypeStruct((t, d), F32), jax.ShapeDtypeStruct((t, d), F32),
                   jax.ShapeDtypeStruct((TOP_K, t), I32), jax.ShapeDtypeStruct((TOP_K, t), I32),
                   jax.ShapeDtypeStruct((TOP_K, t), F32), jax.ShapeDtypeStruct((ne, 1), I32)],
        scratch_shapes=[pltpu.VMEM((ne, 1), F32)],
        compiler_params=_cparams(("arbitrary",)),
        name="mix",
    )(ys5, ylru, x2d, ln_g.reshape(1, d), ln_b.reshape(1, d), g1, sc2, sh2,
      wglu, bglu.reshape(1, ds), wo1, wo2, l1g.reshape(1, d), l1b.reshape(1, d), rwt, rb, tri)


def _dest_kernel(start_ref, eidx_ref, rank_ref, dest_ref):
    e = eidx_ref[...]
    acc = rank_ref[...]
    for j in range(N_EXPERTS):
        acc = acc + jnp.where(e == j, start_ref[j], 0)
    dest_ref[...] = acc


def _dest(start, eidx, rank, tn=2048):
    k, t = eidx.shape
    tn = min(tn, t)
    tok = pl.BlockSpec((k, tn), lambda i, s: (0, i))
    return pl.pallas_call(
        _dest_kernel,
        grid_spec=pltpu.PrefetchScalarGridSpec(num_scalar_prefetch=1, grid=(t // tn,),
                                               in_specs=[tok, tok], out_specs=tok),
        out_shape=jax.ShapeDtypeStruct((k, t), I32),
        compiler_params=_cparams(("parallel",)),
        name="dest",
    )(start, eidx, rank)


def _row_copy_wait(hbm_ref, vmem_ref, sem, rows):
    pltpu.make_async_copy(hbm_ref.at[pl.ds(0, rows), :], vmem_ref, sem).wait()


def _dispatch_kernel(dest_ref, v_ref, xs_ref, sem):
    tm = v_ref.shape[0]

    def body(t, c):
        for k in range(TOP_K):
            pltpu.make_async_copy(v_ref.at[pl.ds(t, 1), :],
                                  xs_ref.at[pl.ds(dest_ref[k, t], 1), :], sem).start()
        return c

    lax.fori_loop(0, tm, body, 0)
    for _ in range(TOP_K):
        _row_copy_wait(xs_ref, v_ref, sem, tm)


def _dispatch(dest, v, tm=256):
    t, d = v.shape
    return pl.pallas_call(
        _dispatch_kernel,
        grid=(t // tm,),
        in_specs=[pl.BlockSpec((TOP_K, tm), lambda i: (0, i), memory_space=pltpu.SMEM),
                  pl.BlockSpec((tm, d), lambda i: (i, 0))],
        out_specs=pl.BlockSpec(memory_space=pl.ANY),
        out_shape=jax.ShapeDtypeStruct((t * TOP_K, d), F32),
        scratch_shapes=[pltpu.SemaphoreType.DMA],
        compiler_params=_cparams(("arbitrary",)),
        name="dispatch",
    )(dest, v)


def _gmm_kernel(vb_ref, ve_ref, vlo_ref, vhi_ref, x_ref, wg_ref, wu_ref, wd_ref, y_ref):
    j = pl.program_id(0)
    x = x_ref[...].astype(BF16)
    g = _dot(x, wg_ref[...].astype(BF16))
    u = _dot(x, wu_ref[...].astype(BF16))
    y = _dot((_silu(g) * u).astype(BF16), wd_ref[...].astype(BF16))
    rows = vb_ref[j] * ROW_BLOCK + lax.broadcasted_iota(I32, (ROW_BLOCK, 1), 0)
    mine = (rows >= vlo_ref[j]) & (rows < vhi_ref[j])
    first = jnp.logical_or(j == 0, vb_ref[jnp.maximum(j - 1, 0)] != vb_ref[j])

    @pl.when(first)
    def _():
        y_ref[...] = jnp.where(mine, y, 0.0)

    @pl.when(jnp.logical_not(first))
    def _():
        y_ref[...] = jnp.where(mine, y, y_ref[...])


def _gmm(vblock, vexp, vlo, vhi, xs, wg, wu, wd):
    n, d = xs.shape
    de = wg.shape[2]
    nv = vblock.shape[0]
    return pl.pallas_call(
        _gmm_kernel,
        grid_spec=pltpu.PrefetchScalarGridSpec(
            num_scalar_prefetch=4, grid=(nv,),
            in_specs=[pl.BlockSpec((ROW_BLOCK, d), lambda j, vb, ve, lo, hi: (vb[j], 0)),
                      pl.BlockSpec((None, d, de), lambda j, vb, ve, lo, hi: (ve[j], 0, 0)),
                      pl.BlockSpec((None, d, de), lambda j, vb, ve, lo, hi: (ve[j], 0, 0)),
                      pl.BlockSpec((None, de, d), lambda j, vb, ve, lo, hi: (ve[j], 0, 0))],
            out_specs=pl.BlockSpec((ROW_BLOCK, d), lambda j, vb, ve, lo, hi: (vb[j], 0))),
        out_shape=jax.ShapeDtypeStruct((n, d), F32),
        compiler_params=_cparams(("arbitrary",)),
        name="gmm",
    )(vblock, vexp, vlo, vhi, xs, wg, wu, wd)


def _visits_kernel(cnt_ref, blk_ref, exp_ref, lo_ref, hi_ref, start_ref, *, n_blocks, nv):
    ne = N_EXPERTS
    shift = ROW_BLOCK.bit_length() - 1

    def fill(j, c):
        blk_ref[j] = n_blocks - 1
        exp_ref[j] = ne - 1
        lo_ref[j] = 0
        hi_ref[j] = 0
        return c

    lax.fori_loop(0, nv, fill, 0)

    def per_expert(e, carry):
        pos, off = carry
        cnt = cnt_ref[e]
        start_ref[e] = off
        first = lax.shift_right_logical(off, shift)
        last = lax.shift_right_logical(off + cnt - 1, shift)
        nvis = jnp.where(cnt > 0, last - first + 1, 0)

        def put(k, c):
            blk_ref[pos + k] = first + k
            exp_ref[pos + k] = e
            lo_ref[pos + k] = off
            hi_ref[pos + k] = off + cnt
            return c

        lax.fori_loop(0, nvis, put, 0)
        return pos + nvis, off + cnt

    lax.fori_loop(0, ne, per_expert, (jnp.int32(0), jnp.int32(0)))


def _visits(counts, n_rows):
    ne = counts.shape[0]
    n_blocks = n_rows // ROW_BLOCK
    nv = n_blocks + ne - 1
    smem = pl.BlockSpec(memory_space=pltpu.SMEM)
    vec = jax.ShapeDtypeStruct((nv,), I32)
    return pl.pallas_call(
        functools.partial(_visits_kernel, n_blocks=n_blocks, nv=nv),
        in_specs=[smem],
        out_specs=[smem] * 5,
        out_shape=[vec, vec, vec, vec, jax.ShapeDtypeStruct((ne,), I32)],
        name="visits",
    )(counts)


def _combine_kernel(dest_ref, wts_ref, v_ref, h1_ref, g2_ref, shg_ref, shu_ref, shd_ref,
                    l2g_ref, l2b_ref, ys_ref, o_ref, gbuf, sem):
    tm = v_ref.shape[0]

    def body(t, c):
        for k in range(TOP_K):
            pltpu.make_async_copy(ys_ref.at[pl.ds(dest_ref[k, t], 1), :],
                                  gbuf.at[k, pl.ds(t, 1), :], sem).start()
        return c

    lax.fori_loop(0, tm, body, 0)

    vb = v_ref[...].astype(BF16)
    hidden = _silu(_dot(vb, shg_ref[...])) * _dot(vb, shu_ref[...])
    f = _dot(hidden.astype(BF16), shd_ref[...])

    for k in range(TOP_K):
        _row_copy_wait(ys_ref, gbuf.at[k], sem, tm)
    wts = wts_ref[...]
    for k in range(TOP_K):
        f = f + wts[:, k:k + 1] * gbuf[k]
    o_ref[...] = _ln(ALPHA * h1_ref[...] + g2_ref[...] * f, l2g_ref[...], l2b_ref[...])


def _combine(dest, wts_t, v, h1, g2, shg, shu, shd, l2g, l2b, ys, seq, tm=256):
    t, d = v.shape
    dsh = shg.shape[1]
    row = pl.BlockSpec((tm, d), lambda i: (i, 0))
    vec = pl.BlockSpec((1, d), lambda i: (0, 0))
    return pl.pallas_call(
        _combine_kernel,
        grid=(t // tm,),
        in_specs=[pl.BlockSpec((TOP_K, tm), lambda i: (0, i), memory_space=pltpu.SMEM),
                  pl.BlockSpec((tm, TOP_K), lambda i: (i, 0)), row, row,
                  pl.BlockSpec((None, 1, d), lambda i: ((i * tm) // seq, 0, 0)),
                  pl.BlockSpec((d, dsh), lambda i: (0, 0)), pl.BlockSpec((d, dsh), lambda i: (0, 0)),
                  pl.BlockSpec((dsh, d), lambda i: (0, 0)), vec, vec,
                  pl.BlockSpec(memory_space=pl.ANY)],
        out_specs=row,
        out_shape=jax.ShapeDtypeStruct((t, d), F32),
        scratch_shapes=[pltpu.VMEM((TOP_K, tm, d), F32), pltpu.SemaphoreType.DMA],
        compiler_params=_cparams(("arbitrary",)),
        name="combine",
    )(dest, wts_t, v, h1, g2, shg, shu, shd, l2g.reshape(1, d), l2b.reshape(1, d), ys)


def _quad(a0, a1):
    return jnp.concatenate([a0, a1, a0, a1], axis=-1)


def kernel(x, c, ctx, c_ctx, ln_in_g, ln_in_b, ada_w, ada_b, w_in, s5_lam_re, s5_lam_im, s5_log_dt, s5_b_re, s5_b_im, s5_c_re, s5_c_im, s5_d, s5_w_glu, s5_b_glu, lru_conv_w, lru_conv_b, lru_w_a, lru_b_a, lru_w_x, lru_b_x, lru_lam, w_out, ln1_g, ln1_b, router_w, router_bias, exp_w_gate, exp_w_up, exp_w_down, sh_w_gate, sh_w_up, sh_w_down, ln2_g, ln2_b):
    bsz, seq, dm = x.shape
    cl = ctx.shape[1]
    assert ada_w.shape[0] == DEPTH
    d_s5 = s5_w_glu.shape[1]
    d_lru = lru_lam.shape[2]
    ngrp = d_s5 // S5_GROUP
    rows = seq // GRID_W
    t = bsz * seq
    lc = S5_CHUNK
    nkl, nkc = seq // lc, cl // lc

    pad = (-(bsz + 1)) % 8
    cc = jnp.concatenate([c, c_ctx[None, :], jnp.zeros((pad, dm), F32)], axis=0)
    mods = _ada(cc, ada_w[0], ada_b[0])
    sh1, sc1, g1, sh2, sc2, g2 = [mods[:bsz, k * dm:(k + 1) * dm].reshape(bsz, 1, dm) for k in range(6)]
    csh1, csc1 = [mods[bsz:bsz + 1, k * dm:(k + 1) * dm].reshape(1, 1, dm) for k in range(2)]

    x2d = x.reshape(t, dm)
    w_in_b = w_in[0].astype(BF16)
    us5, ulru = _in_proj(x2d, ln_in_g, ln_in_b, sc1, sh1, w_in_b, seq, d_s5)
    ucs5, uclru = _in_proj(ctx.reshape(bsz * cl, dm), ln_in_g, ln_in_b, csc1, csh1, w_in_b, bsz * cl, d_s5)

    def to_chunks(u, nk):
        u = u.reshape(bsz, nk, lc, ngrp, S5_GROUP).transpose(3, 1, 0, 2, 4)
        return u.reshape(ngrp, nk * bsz, lc * S5_GROUP)

    lam3 = jnp.stack([_quad(s5_lam_re[0, 0], s5_lam_re[0, 1]), _quad(s5_lam_im[0, 0], s5_lam_im[0, 1]),
                      _quad(*[jnp.broadcast_to(s5_log_dt[0, k][:, None], (ngrp, S5_STATE)) for k in range(2)])],
                     axis=1)
    bt_re = jnp.swapaxes(s5_b_re[0], -1, -2)
    bt_im = jnp.swapaxes(s5_b_im[0], -1, -2)
    tt, win, wout, a16 = _s5_prep(lam3, _quad(bt_re[0], bt_re[1]), _quad(bt_im[0], bt_im[1]),
                                  _quad(s5_c_re[0, 0], s5_c_re[0, 1]), _quad(s5_c_im[0, 0], s5_c_im[0, 1]),
                                  s5_d[0].reshape(ngrp, S5_GROUP, 1))
    ys = _s5(to_chunks(us5, nkl), to_chunks(ucs5, nkc), tt, win, wout, a16, bsz)
    ys5 = ys.reshape(ngrp, nkl, bsz, lc, S5_GROUP).transpose(2, 1, 3, 0, 4).reshape(t, d_s5)

    c_blk = 256
    hd = d_lru // LRU_HEADS
    hpb = c_blk // hd
    ncb = d_lru // c_blk

    def blockdiag(wh):
        wh = wh.reshape(ncb, hpb, hd, hd)
        eye = jnp.eye(hpb, dtype=wh.dtype)
        return jnp.einsum("nhij,hk->nhikj", wh, eye).reshape(ncb, c_blk, c_blk)

    wg = jnp.concatenate([blockdiag(lru_w_a[0, 0]), blockdiag(lru_w_x[0, 0]),
                          blockdiag(lru_w_a[0, 1]), blockdiag(lru_w_x[0, 1])], axis=-1).astype(BF16)
    bgate = jnp.concatenate([lru_b_a[0, 0].reshape(ncb, 1, c_blk), lru_b_x[0, 0].reshape(ncb, 1, c_blk),
                             lru_b_a[0, 1].reshape(ncb, 1, c_blk), lru_b_x[0, 1].reshape(ncb, 1, c_blk)], axis=-1)
    ylru = _lru(ulru.reshape(bsz, rows, GRID_W, 2 * d_lru), uclru.reshape(bsz, cl, 2 * d_lru),
                lru_conv_w[0], lru_conv_b[0].reshape(1, d_lru), wg, bgate, lru_lam[0], c_blk)
    ylru = ylru.reshape(t, d_lru)

    w_out_b = w_out[0].astype(BF16)
    h1, v, eidx, rank, wts, counts = _mix(
        ys5, ylru, x2d, ln_in_g, ln_in_b, g1, sc2, sh2, s5_w_glu[0].astype(BF16), s5_b_glu[0],
        w_out_b[:d_s5], w_out_b[d_s5:], ln1_g[0], ln1_b[0], router_w[0].T,
        router_bias[0].reshape(N_EXPERTS, 1), seq)

    n_rows = t * TOP_K
    vblock, vexp, vlo, vhi, start = _visits(counts[:, 0], n_rows)
    dest = _dest(start, eidx, rank)
    xs = _dispatch(dest, v)
    ysort = _gmm(vblock, vexp, vlo, vhi, xs, exp_w_gate[0], exp_w_up[0], exp_w_down[0])
    out = _combine(dest, wts.T, v, h1, g2, sh_w_gate[0].astype(BF16), sh_w_up[0].astype(BF16),
                   sh_w_down[0].astype(BF16), ln2_g[0], ln2_b[0], ysort, seq)
    return out.reshape(bsz, seq, dm).astype(x.dtype)
```

```python
import functools
import math

import jax
import jax.numpy as jnp
from jax import lax
from jax.experimental import pallas as pl
from jax.experimental.pallas import tpu as pltpu

F32 = jnp.float32
BF16 = jnp.bfloat16
I32 = jnp.int32
HIGHEST = lax.Precision.HIGHEST

GRID_W = 64
S5_GROUP = 16
S5_STATE = 64
S5_CHUNK = 16
LRU_HEADS = 8
LRU_C = 8.0
CONV_W = 4
CONV_LEFT = 2
N_EXPERTS = 64
TOP_K = 8
N_GROUPS = 8
TOPK_GROUPS = 4
ROUTED_SCALE = 2.5
LN_EPS = 1e-5
DEPTH = 1
ALPHA = (2.0 * DEPTH) ** 0.25

ROW_BLOCK = 256
TOKEN_TILE = 256
SLOT = 48
SLOT_HEAD = 32
SLOT_CHUNK = 16 * SLOT
SEG_ALIGN = 8
VMEM_LIMIT = 52 * 1024 * 1024


def _cparams(sem):
    return pltpu.CompilerParams(dimension_semantics=sem, vmem_limit_bytes=VMEM_LIMIT)


def _ln(x, g, b):
    mu = jnp.mean(x, axis=-1, keepdims=True)
    xc = x - mu
    var = jnp.mean(xc * xc, axis=-1, keepdims=True)
    return xc * lax.rsqrt(var + LN_EPS) * g + b


def _gelu(x):
    return x * (0.5 * (1.0 + jnp.tanh(math.sqrt(2.0 / math.pi) * (x + 0.044715 * (x * x * x)))))


def _silu(x):
    return x * jax.nn.sigmoid(x)


def _softplus(x):
    return jnp.maximum(x, 0.0) + jnp.log1p(jnp.exp(-jnp.abs(x)))


def _dot(a, b):
    return jnp.dot(a, b, preferred_element_type=F32)


def _dot_nt(a, b, precision=None):
    return lax.dot_general(a, b, (((1,), (1,)), ((), ())), precision=precision,
                           preferred_element_type=F32)


def _ada_kernel(c_ref, w_ref, b_ref, o_ref):
    s = _silu(c_ref[...])
    o_ref[...] = jnp.dot(s, w_ref[...], precision=HIGHEST, preferred_element_type=F32) + b_ref[...]


def _ada(cc, ada_w, ada_b):
    r, d = cc.shape
    n = ada_w.shape[1]
    tn = 512
    return pl.pallas_call(
        _ada_kernel,
        grid=(n // tn,),
        in_specs=[pl.BlockSpec((r, d), lambda j: (0, 0)),
                  pl.BlockSpec((d, tn), lambda j: (0, j)),
                  pl.BlockSpec((1, tn), lambda j: (0, j))],
        out_specs=pl.BlockSpec((r, tn), lambda j: (0, j)),
        out_shape=jax.ShapeDtypeStruct((r, n), F32),
        compiler_params=_cparams(("parallel",)),
        name="ada",
    )(cc, ada_w, ada_b.reshape(1, n))


def _in_proj_kernel(x_ref, g_ref, b_ref, sc_ref, sh_ref, w_ref, us5_ref, ulru_ref):
    h = _ln(x_ref[...], g_ref[...], b_ref[...])
    m = h * (1.0 + sc_ref[...]) + sh_ref[...]
    u = _dot(m.astype(BF16), w_ref[...])
    ds5 = us5_ref.shape[-1]
    us5_ref[...] = u[:, :ds5].astype(BF16)
    ulru_ref[...] = u[:, ds5:].astype(BF16)


def _in_proj(x2d, ln_g, ln_b, sc, sh, w_bf16, rows_per_mod, d_s5, tm=512):
    t, d = x2d.shape
    n = w_bf16.shape[1]
    tm = min(tm, t)
    mod_spec = pl.BlockSpec((None, 1, d), lambda i: ((i * tm) // rows_per_mod, 0, 0))
    vec = pl.BlockSpec((1, d), lambda i: (0, 0))
    return pl.pallas_call(
        _in_proj_kernel,
        grid=(t // tm,),
        in_specs=[pl.BlockSpec((tm, d), lambda i: (i, 0)), vec, vec, mod_spec, mod_spec,
                  pl.BlockSpec((d, n), lambda i: (0, 0))],
        out_specs=[pl.BlockSpec((tm, d_s5), lambda i: (i, 0)),
                   pl.BlockSpec((tm, n - d_s5), lambda i: (i, 0))],
        out_shape=[jax.ShapeDtypeStruct((t, d_s5), BF16),
                   jax.ShapeDtypeStruct((t, n - d_s5), BF16)],
        compiler_params=_cparams(("parallel",)),
        name="in_proj",
    )(x2d, ln_g.reshape(1, d), ln_b.reshape(1, d), sc, sh, w_bf16)


def _s5_prep_kernel(lam_ref, bre_ref, bim_ref, cre_ref, cim_ref, d_ref,
                    tt_ref, win_ref, wout_ref, a16_ref, cp_scr):
    p, lc, gs = S5_STATE, S5_CHUNK, S5_GROUP
    nl = 4 * p
    lre = lam_ref[0:1, :]
    lim = lam_ref[1:2, :]
    dt = jnp.exp(lam_ref[2:3, :])
    lane = lax.broadcasted_iota(I32, (1, nl), 1)
    is_re = lane < 2 * p
    is_f = (lane % (2 * p)) < p

    def powers(ef, eb):
        e = jnp.where(is_f, float(ef), float(eb))
        mag = jnp.exp(lre * dt * e)
        ang = lim * dt * e
        return mag * jnp.cos(ang), mag * jnp.sin(ang)

    bre, bim = bre_ref[...], bim_ref[...]
    cre, cim = cre_ref[...], cim_ref[...]
    ar, ai = powers(1, 1)
    den = lre * lre + lim * lim
    nr = ar - 1.0
    fr = (nr * lre + ai * lim) / den
    fi = (ai * lre - nr * lim) / den
    bbre = fr * bre - fi * bim
    bbim = fr * bim + fi * bre

    def c_times(pc, ps):
        return cre * jnp.where(is_re, pc, -ps) + cim * jnp.where(is_re, -ps, -pc)

    for s in range(lc):
        pc, ps = powers(lc - 1 - s, s)
        qa = jnp.where(is_re, pc, ps)
        qb = jnp.where(is_re, -ps, pc)
        win_ref[gs * s:gs * (s + 1), :] = (bbre * qa + bbim * qb).astype(BF16)
    for t in range(lc):
        pc, ps = powers(t + 1, lc - t)
        wout_ref[gs * t:gs * (t + 1), :] = c_times(pc, ps).astype(BF16)
    for j in range(lc):
        pc, ps = powers(j, lc - 1 - j)
        cp_scr[gs * j:gs * (j + 1), :] = c_times(pc, ps)

    bbcat = jnp.where(is_re, bbre, bbim)
    cp = cp_scr[...]
    ktf = _dot_nt(jnp.where(is_f, bbcat, 0.0), cp, HIGHEST)
    ktb = _dot_nt(jnp.where(is_f, 0.0, bbcat), cp, HIGHEST)
    lane2 = lax.broadcasted_iota(I32, (gs, gs * lc), 1)
    row2 = lax.broadcasted_iota(I32, (gs, gs * lc), 0)
    dcol = d_ref[...]
    width = gs * lc
    for s in range(lc):
        sf = gs * s
        tf = ktf if sf == 0 else pltpu.roll(ktf, sf, 1)
        tf = jnp.where(lane2 >= sf, tf, 0.0)
        sb = (gs * (s + 1)) % width
        tb = ktb if sb == 0 else pltpu.roll(ktb, sb, 1)
        tb = jnp.where(lane2 < gs * (s + 1), tb, 0.0)
        skip = jnp.where(lane2 == sf + row2, dcol, 0.0)
        tt_ref[gs * s:gs * (s + 1), :] = (tf + tb + skip).astype(BF16)
    pc, ps = powers(lc, lc)
    a16_ref[0:1, :] = pc[:, :2 * p]
    a16_ref[1:2, :] = ps[:, :2 * p]


def _s5_prep(lam3, b4re, b4im, c4re, c4im, dcol):
    g = lam3.shape[0]
    nl = 4 * S5_STATE
    k = S5_GROUP * S5_CHUNK
    m3 = lambda i: (i, 0, 0)
    return pl.pallas_call(
        _s5_prep_kernel,
        grid=(g,),
        in_specs=[pl.BlockSpec((None, 3, nl), m3)]
        + [pl.BlockSpec((None, S5_GROUP, nl), m3)] * 4
        + [pl.BlockSpec((None, S5_GROUP, 1), m3)],
        out_specs=[pl.BlockSpec((None, k, k), m3), pl.BlockSpec((None, k, nl), m3),
                   pl.BlockSpec((None, k, nl), m3), pl.BlockSpec((None, 2, nl // 2), m3)],
        out_shape=[jax.ShapeDtypeStruct((g, k, k), BF16), jax.ShapeDtypeStruct((g, k, nl), BF16),
                   jax.ShapeDtypeStruct((g, k, nl), BF16), jax.ShapeDtypeStruct((g, 2, nl // 2), F32)],
        scratch_shapes=[pltpu.VMEM((k, nl), F32)],
        compiler_params=_cparams(("parallel",)),
        name="s5_prep",
    )(lam3, b4re, b4im, c4re, c4im, dcol)


def _s5_kernel(xl_ref, xc_ref, tt_ref, win_ref, wout_ref, a16_ref, y_ref,
               z_scr, zc_scr, sf_scr, sb_scr, *, nkl, nkc, bsz):
    h = 2 * S5_STATE
    xl = xl_ref[...]
    win = win_ref[...]
    z_scr[...] = _dot(xl, win)
    zc_scr[...] = _dot(xc_ref[...], win)
    is_f = lax.broadcasted_iota(I32, (bsz, h), 1) < S5_STATE
    ar = a16_ref[0:1, :]
    ai = a16_ref[1:2, :]

    def pick(z_ref, i, n):
        zf = z_ref[pl.ds(i * bsz, bsz), :]
        zb = z_ref[pl.ds((n - 1 - i) * bsz, bsz), :]
        return jnp.where(is_f, zf[:, :h], zb[:, :h]), jnp.where(is_f, zf[:, h:], zb[:, h:])

    def update(sr, si, zr, zi):
        return ar * sr - ai * si + zr, ar * si + ai * sr + zi

    def ctx_step(i, carry):
        zr, zi = pick(zc_scr, i, nkc)
        return update(*carry, zr, zi)

    def lat_step(i, carry):
        sr, si = carry
        st = jnp.concatenate([sr, si], axis=1)
        sf_scr[pl.ds(i * bsz, bsz), :] = st
        sb_scr[pl.ds((nkl - 1 - i) * bsz, bsz), :] = st
        zr, zi = pick(z_scr, i, nkl)
        return update(sr, si, zr, zi)

    zero = jnp.zeros((bsz, h), F32)
    carry = lax.fori_loop(0, nkc, ctx_step, (zero, zero))
    lax.fori_loop(0, nkl, lat_step, carry)

    is_f4 = (lax.broadcasted_iota(I32, (1, 2 * h), 1) % h) < S5_STATE
    xs = jnp.where(is_f4, sf_scr[...], sb_scr[...]).astype(BF16)
    y = _dot(xl, tt_ref[...]) + _dot_nt(xs, wout_ref[...])
    y_ref[...] = y.astype(BF16)


def _s5(xl, xc, tt, win, wout, a16, bsz):
    g, nl_rows, k = xl.shape
    nc_rows = xc.shape[1]
    nl = 4 * S5_STATE
    m3 = lambda i: (i, 0, 0)
    kern = functools.partial(_s5_kernel, nkl=nl_rows // bsz, nkc=nc_rows // bsz, bsz=bsz)
    return pl.pallas_call(
        kern,
        grid=(g,),
        in_specs=[pl.BlockSpec((None, nl_rows, k), m3), pl.BlockSpec((None, nc_rows, k), m3),
                  pl.BlockSpec((None, k, k), m3), pl.BlockSpec((None, k, nl), m3),
                  pl.BlockSpec((None, k, nl), m3), pl.BlockSpec((None, 2, nl // 2), m3)],
        out_specs=pl.BlockSpec((None, nl_rows, k), m3),
        out_shape=jax.ShapeDtypeStruct((g, nl_rows, k), BF16),
        scratch_shapes=[pltpu.VMEM((nl_rows, nl), F32), pltpu.VMEM((nc_rows, nl), F32),
                        pltpu.VMEM((nl_rows, nl), F32), pltpu.VMEM((nl_rows, nl), F32)],
        compiler_params=_cparams(("parallel",)),
        name="s5",
    )(xl, xc, tt, win, wout, a16)


def _lru_kernel(xv_ref, xg_ref, xc_ref, cw_ref, cb_ref, wg_ref, bg_ref, lam_ref, o_ref,
                a_scr, b_scr, ac_scr, bc_scr, cin_scr, *, rows, cl, slab_block):
    w = GRID_W
    c = xv_ref.shape[-1]
    cw = cw_ref[...]
    cb = cb_ref[...]
    sp = _softplus(-lam_ref[...])
    bg = bg_ref[...]
    wg = wg_ref[...]

    def coeffs(xs, store):
        g = _dot(xs.astype(BF16), wg) + bg
        for d in range(2):
            r = jax.nn.sigmoid(g[:, (2 * d) * c:(2 * d + 1) * c])
            i = jax.nn.sigmoid(g[:, (2 * d + 1) * c:(2 * d + 2) * c])
            log_a = -LRU_C * r * sp[d:d + 1, :]
            a = jnp.exp(log_a)
            b = jnp.sqrt(-jnp.tanh(log_a) * (a * a + 1.0)) * (i * xs)
            store(d, a, b)

    def tap(r0, r1, off):
        lo, hi = r0 + off, r1 + off
        clo, chi = max(lo, 0), min(hi, rows)
        parts = []
        if clo > lo:
            parts.append(jnp.zeros((clo - lo, w, c), F32))
        if chi > clo:
            parts.append(xv_ref[clo:chi].astype(F32))
        if hi > chi:
            parts.append(jnp.zeros((hi - chi, w, c), F32))
        return parts[0] if len(parts) == 1 else jnp.concatenate(parts, axis=0)

    for r0 in range(0, rows, slab_block):
        r1 = r0 + slab_block
        xs3 = cb
        for k in range(CONV_W):
            xs3 = xs3 + cw[k:k + 1, :] * tap(r0, r1, k - CONV_LEFT)
        xs = xs3.reshape(slab_block * w, c)

        def store_lat(d, a, b, r0=r0, r1=r1):
            a_scr[d, r0 * w:r1 * w, :] = a
            b_scr[d, r0 * w:r1 * w, :] = b

        coeffs(xs, store_lat)

    xc = xc_ref[...].astype(F32)
    trow = lax.broadcasted_iota(I32, (cl, c), 0)
    xcs = cb
    for k in range(CONV_W):
        off = k - CONV_LEFT
        sh = xc if off == 0 else pltpu.roll(xc, (-off) % cl, 0)
        ok = (trow + off >= 0) & (trow + off < cl)
        xcs = xcs + cw[k:k + 1, :] * jnp.where(ok, sh, 0.0)

    def store_ctx(d, a, b):
        ac_scr[d] = a
        bc_scr[d] = b

    coeffs(xcs, store_ctx)

    for d in range(2):
        rev = d == 1

        def ctx_step(j, h, d=d, rev=rev):
            idx = (cl - 1 - j) if rev else j
            return ac_scr[d, pl.ds(idx, 1), :] * h + bc_scr[d, pl.ds(idx, 1), :]

        h0 = lax.fori_loop(0, cl, ctx_step, jnp.zeros((1, c), F32))

        def col_step(j, carry, d=d, rev=rev):
            h, p = carry
            r = (rows - 1 - j) if rev else j
            off = pl.multiple_of(r * w, w)
            a = a_scr[d, pl.ds(off, w), :]
            h = a * h + b_scr[d, pl.ds(off, w), :]
            p = a * p
            b_scr[d, pl.ds(off, w), :] = h
            a_scr[d, pl.ds(off, w), :] = p
            return h, p

        lax.fori_loop(0, rows, col_step, (jnp.zeros((w, c), F32), jnp.ones((w, c), F32)))

        last = 0 if rev else (rows - 1) * w

        def carry_step(j, cin, d=d, rev=rev, last=last):
            col = (w - 1 - j) if rev else j
            cin_scr[pl.ds(col, 1), :] = cin
            return a_scr[d, pl.ds(last + col, 1), :] * cin + b_scr[d, pl.ds(last + col, 1), :]

        lax.fori_loop(0, w, carry_step, h0)
        cin = cin_scr[...]

        if not rev:
            def fix_step(r, _, d=d, cin=cin):
                off = pl.multiple_of(r * w, w)
                b_scr[d, pl.ds(off, w), :] = b_scr[d, pl.ds(off, w), :] + a_scr[d, pl.ds(off, w), :] * cin
                return 0

            lax.fori_loop(0, rows, fix_step, 0)
        else:
            def out_step(r, _, cin=cin):
                off = pl.multiple_of(r * w, w)
                hsum = (b_scr[0, pl.ds(off, w), :] + b_scr[1, pl.ds(off, w), :]
                        + a_scr[1, pl.ds(off, w), :] * cin)
                o_ref[r] = (hsum * _gelu(xg_ref[r].astype(F32))).astype(o_ref.dtype)
                return 0

            lax.fori_loop(0, rows, out_step, 0)


def _lru(u4, uc3, conv_w, conv_b, wg, bg, lam, c_blk=256):
    bsz, rows, w, n2 = u4.shape
    d_lru = n2 // 2
    cl = uc3.shape[1]
    ncb = d_lru // c_blk
    kern = functools.partial(_lru_kernel, rows=rows, cl=cl, slab_block=8)
    n = rows * w
    return pl.pallas_call(
        kern,
        grid=(bsz, ncb),
        in_specs=[pl.BlockSpec((None, rows, w, c_blk), lambda b, j: (b, 0, 0, j)),
                  pl.BlockSpec((None, rows, w, c_blk), lambda b, j: (b, 0, 0, ncb + j)),
                  pl.BlockSpec((None, cl, c_blk), lambda b, j: (b, 0, j)),
                  pl.BlockSpec((CONV_W, c_blk), lambda b, j: (0, j)),
                  pl.BlockSpec((1, c_blk), lambda b, j: (0, j)),
                  pl.BlockSpec((None, c_blk, 4 * c_blk), lambda b, j: (j, 0, 0)),
                  pl.BlockSpec((None, 1, 4 * c_blk), lambda b, j: (j, 0, 0)),
                  pl.BlockSpec((2, c_blk), lambda b, j: (0, j))],
        out_specs=pl.BlockSpec((None, rows, w, c_blk), lambda b, j: (b, 0, 0, j)),
        out_shape=jax.ShapeDtypeStruct((bsz, rows, w, d_lru), BF16),
        scratch_shapes=[pltpu.VMEM((2, n, c_blk), F32), pltpu.VMEM((2, n, c_blk), F32),
                        pltpu.VMEM((2, cl, c_blk), F32), pltpu.VMEM((2, cl, c_blk), F32),
                        pltpu.VMEM((w, c_blk), F32)],
        compiler_params=_cparams(("parallel", "parallel")),
        name="lru",
    )(u4, u4, uc3, conv_w, conv_b, wg, bg, lam)


def _mix_kernel(ys5_ref, ylru_ref, x_ref, lng_ref, lnb_ref, g1_ref, sc2_ref, sh2_ref,
                wglu_ref, bglu_ref, wo1_ref, wo2_ref, l1g_ref, l1b_ref, rwt_ref, rb_ref, tri_ref,
                h1_ref, v_ref, eidx_ref, rank_ref, wts_ref, tcnt_ref):
    y = _gelu(ys5_ref[...].astype(F32))
    s5o = y * jax.nn.sigmoid(_dot(y.astype(BF16), wglu_ref[...]) + bglu_ref[...])
    y1 = _dot(s5o.astype(BF16), wo1_ref[...]) + _dot(ylru_ref[...], wo2_ref[...])
    h = _ln(x_ref[...], lng_ref[...], lnb_ref[...])
    h1 = _ln(ALPHA * h + g1_ref[...] * y1, l1g_ref[...], l1b_ref[...])
    h1_ref[...] = h1
    v = h1 * (1.0 + sc2_ref[...]) + sh2_ref[...]
    v_ref[...] = v.astype(v_ref.dtype)

    tm = v.shape[0]
    ne, ng, gsz = N_EXPERTS, N_GROUPS, N_EXPERTS // N_GROUPS
    scores = jax.nn.sigmoid(_dot_nt(rwt_ref[...], v, HIGHEST))
    s3 = scores.reshape(ng, gsz, tm)
    sel3 = (scores + rb_ref[...]).reshape(ng, gsz, tm)
    ii = lax.broadcasted_iota(I32, (ng, gsz, tm), 1)
    gi = lax.broadcasted_iota(I32, (ng, gsz, tm), 0)
    neg = -jnp.inf

    m1 = jnp.max(sel3, axis=1, keepdims=True)
    f1 = jnp.min(jnp.where(sel3 == m1, ii, gsz), axis=1, keepdims=True)
    m2 = jnp.max(jnp.where(ii == f1, neg, sel3), axis=1, keepdims=True)
    cur = m1 + m2
    gidx = lax.broadcasted_iota(I32, (ng, 1, tm), 0)
    gmask = jnp.zeros((ng, 1, tm), jnp.bool_)
    for _ in range(TOPK_GROUPS):
        mx = jnp.max(cur, axis=0, keepdims=True)
        fg = jnp.min(jnp.where(cur == mx, gidx, ng), axis=0, keepdims=True)
        hit = gidx == fg
        gmask = gmask | hit
        cur = jnp.where(hit, neg, cur)

    selm = jnp.where(gmask, sel3, neg)
    eid = gi * gsz + ii
    picks, erows, wrows = [], [], []
    for _ in range(TOP_K):
        mx = jnp.max(jnp.max(selm, axis=1, keepdims=True), axis=0, keepdims=True)
        fe = jnp.min(jnp.min(jnp.where(selm == mx, eid, ne), axis=1, keepdims=True),
                     axis=0, keepdims=True)
        hit = eid == fe
        selm = jnp.where(hit, neg, selm)
        picks.append(hit)
        erows.append(fe.reshape(1, tm))
        wsel = jnp.where(hit, s3, 0.0)
        wrows.append(jnp.sum(jnp.sum(wsel, axis=1, keepdims=True), axis=0).reshape(1, tm))
    denom = wrows[0]
    for k in range(1, TOP_K):
        denom = denom + wrows[k]

    chosen = picks[0]
    for k in range(1, TOP_K):
        chosen = chosen | picks[k]
    chosen_f = jnp.where(chosen, 1.0, 0.0).reshape(ne, tm)
    cnt3 = _dot(chosen_f.astype(BF16), tri_ref[...]).reshape(ng, gsz, tm)
    rrows = []
    for k in range(TOP_K):
        rsel = jnp.where(picks[k], cnt3, 0.0)
        rrows.append(jnp.sum(jnp.sum(rsel, axis=1, keepdims=True), axis=0).reshape(1, tm))
    tcnt_ref[...] = jnp.concatenate(
        [jnp.sum(chosen_f[:, j * TOKEN_TILE:(j + 1) * TOKEN_TILE], axis=1, keepdims=True)
         for j in range(tm // TOKEN_TILE)], axis=1).astype(I32)

    eidx_ref[...] = jnp.concatenate(erows, axis=0)
    rank_ref[...] = jnp.concatenate(rrows, axis=0).astype(I32)
    wts_ref[...] = jnp.concatenate([wr / denom * ROUTED_SCALE for wr in wrows], axis=0)


def _mix(ys5, ylru, x2d, ln_g, ln_b, g1, sc2, sh2, wglu, bglu, wo1, wo2, l1g, l1b, rwt, rb, seq, tm=512):
    t, d = x2d.shape
    ds = ys5.shape[1]
    ne = rwt.shape[0]
    nsub = tm // TOKEN_TILE
    r_i = lax.broadcasted_iota(I32, (tm, tm), 0)
    c_i = lax.broadcasted_iota(I32, (tm, tm), 1)
    tri = ((r_i < c_i) & (r_i // TOKEN_TILE == c_i // TOKEN_TILE)).astype(BF16)
    row = lambda n: pl.BlockSpec((tm, n), lambda i: (i, 0))
    vec = lambda n: pl.BlockSpec((1, n), lambda i: (0, 0))
    mod = pl.BlockSpec((None, 1, d), lambda i: ((i * tm) // seq, 0, 0))
    full = lambda a, b: pl.BlockSpec((a, b), lambda i: (0, 0))
    tok = pl.BlockSpec((TOP_K, tm), lambda i: (0, i))
    return pl.pallas_call(
        _mix_kernel,
        grid=(t // tm,),
        in_specs=[row(ds), row(ds), row(d), vec(d), vec(d), mod, mod, mod,
                  full(ds, ds), vec(ds), full(ds, d), full(ds, d), vec(d), vec(d),
                  full(ne, d), full(ne, 1), full(tm, tm)],
        out_specs=[row(d), row(d), tok, tok, tok,
                   pl.BlockSpec((None, ne, nsub), lambda i: (i, 0, 0))],
        out_shape=[jax.ShapeDtypeStruct((t, d), F32), jax.ShapeDtypeStruct((t, d), BF16),
                   jax.ShapeDtypeStruct((TOP_K, t), I32), jax.ShapeDtypeStruct((TOP_K, t), I32),
                   jax.ShapeDtypeStruct((TOP_K, t), F32),
                   jax.ShapeDtypeStruct((t // tm, ne, nsub), I32)],
        compiler_params=_cparams(("parallel",)),
        name="mix",
    )(ys5, ylru, x2d, ln_g.reshape(1, d), ln_b.reshape(1, d), g1, sc2, sh2,
      wglu, bglu.reshape(1, ds), wo1, wo2, l1g.reshape(1, d), l1b.reshape(1, d), rwt, rb, tri)


def _plan_kernel(tc_ref, dst_ref, npass_ref, slack_ref, tail_ref, blk_ref, xblk_ref, exp_ref, lo_ref,
                 hi_ref, *, nt, n_blocks, nv):
    ne = N_EXPERTS
    shift = ROW_BLOCK.bit_length() - 1

    def put(pos, blk, xblk, e, lo, hi):
        blk_ref[pos] = blk
        xblk_ref[pos] = xblk
        exp_ref[pos] = e
        lo_ref[pos] = lo
        hi_ref[pos] = hi

    def per_expert(e, carry):
        pos, done, off = carry

        def per_tile(i, c):
            dst_ref[i, e] = off + c
            return c + jnp.bitwise_and(tc_ref[i, e] + (SEG_ALIGN - 1), -SEG_ALIGN)

        cnt = lax.fori_loop(0, nt, per_tile, jnp.int32(0))
        slack_ref[e] = off + cnt
        first = lax.shift_right_logical(off, shift)
        last = lax.shift_right_logical(off + cnt - 1, shift)
        has = cnt > 0
        gap = jnp.where(has, jnp.maximum(first - done, 0), 0)

        def put_gap(k, c):
            put(pos + k, done + k, 0, e, 0, 0)
            return c

        lax.fori_loop(0, gap, put_gap, 0)
        pos = pos + gap
        nvis = jnp.where(has, last - first + 1, 0)

        def put_vis(k, c):
            put(pos + k, first + k, first + k, e, off, off + cnt)
            return c

        lax.fori_loop(0, nvis, put_vis, 0)
        done = jnp.where(has, jnp.maximum(done, last + 1), done)
        return pos + nvis, done, off + cnt + SLOT

    pos, done, end = lax.fori_loop(0, ne, per_expert, (jnp.int32(0), jnp.int32(0), jnp.int32(0)))
    tail_ref[0] = end

    def put_tail(k, c):
        put(pos + k, done + k, 0, ne - 1, 0, 0)
        return c

    lax.fori_loop(0, n_blocks - done, put_tail, 0)
    pos = pos + (n_blocks - done)

    def put_rest(j, c):
        put(j, n_blocks - 1, 0, ne - 1, 0, 0)
        return c

    lax.fori_loop(pos, nv, put_rest, 0)

    def per_tile_pass(i, c):
        m = lax.fori_loop(0, ne, lambda e, m: jnp.maximum(m, tc_ref[i, e]), jnp.int32(0))
        n = jnp.int32(1)
        for q in range(1, -(-TOKEN_TILE // SLOT)):
            n = n + (m > q * SLOT).astype(I32)
        npass_ref[i] = n
        return c

    lax.fori_loop(0, nt, per_tile_pass, 0)


def _sorted_rows(n_tokens):
    nt = n_tokens // TOKEN_TILE
    rows = n_tokens * TOP_K + nt * N_EXPERTS * (SEG_ALIGN - 1) + N_EXPERTS * SLOT
    return -(-rows // ROW_BLOCK) * ROW_BLOCK


def _plan(tc, n_rows):
    nt, ne = tc.shape
    n_blocks = n_rows // ROW_BLOCK
    nv = n_blocks + ne
    smem = pl.BlockSpec(memory_space=pltpu.SMEM)
    vec = jax.ShapeDtypeStruct((nv,), I32)
    return pl.pallas_call(
        functools.partial(_plan_kernel, nt=nt, n_blocks=n_blocks, nv=nv),
        in_specs=[smem],
        out_specs=[smem] * 9,
        out_shape=[jax.ShapeDtypeStruct((nt, ne), I32), jax.ShapeDtypeStruct((nt,), I32),
                   jax.ShapeDtypeStruct((ne,), I32), jax.ShapeDtypeStruct((1,), I32),
                   vec, vec, vec, vec, vec],
        name="plan",
    )(tc)


def _slot_copies(tc_ref, dst_ref, tile, p, hbm_ref, buf, to_hbm, sem):
    base = p * SLOT

    def one(e, fn):
        n = tc_ref[tile, e] - base
        row = dst_ref[tile, e] + base
        for lo, size in ((0, SLOT_HEAD), (SLOT_HEAD, SLOT - SLOT_HEAD)):
            @pl.when(n > lo)
            def _(lo=lo, size=size):
                slot_rows = buf.at[pl.ds(pl.multiple_of(e * SLOT + lo, 8), size), :]
                sorted_rows = hbm_ref.at[pl.ds(pl.multiple_of(row + lo, SEG_ALIGN), size), :]
                src, dst = (slot_rows, sorted_rows) if to_hbm else (sorted_rows, slot_rows)
                fn(pltpu.make_async_copy(src, dst, sem))

    def start():
        def body(e, c):
            one(e, lambda cp: cp.start())
            return c
        lax.fori_loop(0, N_EXPERTS, body, 0)

    def wait():
        def body(e, c):
            one(e, lambda cp: cp.wait())
            return c
        lax.fori_loop(0, N_EXPERTS, body, 0)

    return start, wait


def _dispatch_kernel(tc_ref, dst_ref, npass_ref, slack_ref, tail_ref, eidx_ref, lr_ref, v_ref, xs_ref,
                     cbuf, zbuf, st, sem, zsem):
    i = pl.program_id(0)
    nt = pl.num_programs(0)
    tm = v_ref.shape[0]
    nslot = N_EXPERTS * SLOT

    @pl.when(i == 0)
    def _():
        st[0] = 0
        st[1] = -1
        st[2] = 0

    vb = v_ref[...]
    e8 = eidx_ref[...]
    lr8 = lr_ref[...]

    def drain():
        @pl.when(st[1] >= 0)
        def _():
            _, wait = _slot_copies(tc_ref, dst_ref, st[1], st[2], xs_ref, cbuf.at[1 - st[0]], True, sem)
            wait()
        st[1] = -1

    def one_pass(p, c):
        s = st[0]
        rel = lr8 - p * SLOT
        tr = jnp.where((rel >= 0) & (rel < SLOT), e8 * SLOT + rel, -1)
        for c0 in range(0, nslot, SLOT_CHUNK):
            rows = lax.broadcasted_iota(I32, (SLOT_CHUNK, tm), 0) + c0
            onehot = jnp.zeros((SLOT_CHUNK, tm), F32)
            for k in range(TOP_K):
                onehot = jnp.where(rows == tr[k:k + 1, :], 1.0, onehot)
            onehot = onehot.astype(BF16)
            cbuf[s, c0:c0 + SLOT_CHUNK, :] = _dot(onehot, vb)
        drain()
        start, _ = _slot_copies(tc_ref, dst_ref, i, p, xs_ref, cbuf.at[s], True, sem)
        start()
        st[0] = 1 - s
        st[1] = i
        st[2] = p
        return c

    lax.fori_loop(0, npass_ref[i], one_pass, 0)

    @pl.when(i == nt - 1)
    def _():
        drain()
        zbuf[...] = jnp.zeros_like(zbuf)

        def zero_rows(start, size):
            rows = pl.ds(pl.multiple_of(start, SEG_ALIGN), size)
            return pltpu.make_async_copy(zbuf.at[pl.ds(0, size), :], xs_ref.at[rows, :], zsem)

        tail = tail_ref[0]
        n_small = lax.shift_right_logical(jnp.bitwise_and(-tail, ROW_BLOCK - 1),
                                          SEG_ALIGN.bit_length() - 1)
        tail_blk = tail + n_small * SEG_ALIGN
        n_big = lax.shift_right_logical(xs_ref.shape[0] - tail_blk, ROW_BLOCK.bit_length() - 1)

        def each(fn):
            lax.fori_loop(0, N_EXPERTS, lambda e, c: fn(zero_rows(slack_ref[e], SLOT), c), 0)
            lax.fori_loop(0, n_small, lambda q, c: fn(zero_rows(tail + q * SEG_ALIGN, SEG_ALIGN), c), 0)
            lax.fori_loop(0, n_big, lambda q, c: fn(zero_rows(tail_blk + q * ROW_BLOCK, ROW_BLOCK), c), 0)

        each(lambda cp, c: (cp.start(), c)[1])
        each(lambda cp, c: (cp.wait(), c)[1])


def _dispatch(tc, dst, npass, slack, tail, eidx, lrank, v, n_rows):
    t, d = v.shape
    tm = TOKEN_TILE
    nslot = N_EXPERTS * SLOT
    tok = pl.BlockSpec((TOP_K, tm), lambda i, *_: (0, i))
    return pl.pallas_call(
        _dispatch_kernel,
        grid_spec=pltpu.PrefetchScalarGridSpec(
            num_scalar_prefetch=5, grid=(t // tm,),
            in_specs=[tok, tok, pl.BlockSpec((tm, d), lambda i, *_: (i, 0))],
            out_specs=pl.BlockSpec(memory_space=pl.ANY),
            scratch_shapes=[pltpu.VMEM((2, nslot, d), F32), pltpu.VMEM((ROW_BLOCK, d), F32),
                            pltpu.SMEM((3,), I32), pltpu.SemaphoreType.DMA, pltpu.SemaphoreType.DMA]),
        out_shape=jax.ShapeDtypeStruct((n_rows, d), F32),
        compiler_params=_cparams(("arbitrary",)),
        name="dispatch",
    )(tc, dst, npass, slack, tail, eidx, lrank, v)


def _gmm_kernel(vb_ref, vx_ref, ve_ref, vlo_ref, vhi_ref, x_ref, wg_ref, wu_ref, wd_ref, y_ref,
                wg_b, wu_b, wd_b):
    j = pl.program_id(0)
    prev = jnp.maximum(j - 1, 0)
    first = jnp.logical_or(j == 0, vb_ref[prev] != vb_ref[j])
    live = vhi_ref[j] > vlo_ref[j]

    @pl.when(jnp.logical_or(j == 0, ve_ref[prev] != ve_ref[j]))
    def _():
        wg_b[...] = wg_ref[...].astype(BF16)
        wu_b[...] = wu_ref[...].astype(BF16)
        wd_b[...] = wd_ref[...].astype(BF16)

    @pl.when(live)
    def _():
        x = x_ref[...].astype(BF16)
        hidden = _silu(_dot(x, wg_b[...])) * _dot(x, wu_b[...])
        y = _dot(hidden.astype(BF16), wd_b[...])
        rows = vb_ref[j] * ROW_BLOCK + lax.broadcasted_iota(I32, (ROW_BLOCK, 1), 0)
        mine = (rows >= vlo_ref[j]) & (rows < vhi_ref[j])

        @pl.when(first)
        def _():
            y_ref[...] = jnp.where(mine, y, 0.0)

        @pl.when(jnp.logical_not(first))
        def _():
            y_ref[...] = jnp.where(mine, y, y_ref[...])

    @pl.when(jnp.logical_and(jnp.logical_not(live), first))
    def _():
        y_ref[...] = jnp.zeros_like(y_ref)


def _gmm(vblock, vxblock, vexp, vlo, vhi, xs, wg, wu, wd):
    n, d = xs.shape
    de = wg.shape[2]
    nv = vblock.shape[0]
    return pl.pallas_call(
        _gmm_kernel,
        grid_spec=pltpu.PrefetchScalarGridSpec(
            num_scalar_prefetch=5, grid=(nv,),
            in_specs=[pl.BlockSpec((ROW_BLOCK, d), lambda j, vb, vx, ve, lo, hi: (vx[j], 0)),
                      pl.BlockSpec((None, d, de), lambda j, vb, vx, ve, lo, hi: (ve[j], 0, 0)),
                      pl.BlockSpec((None, d, de), lambda j, vb, vx, ve, lo, hi: (ve[j], 0, 0)),
                      pl.BlockSpec((None, de, d), lambda j, vb, vx, ve, lo, hi: (ve[j], 0, 0))],
            out_specs=pl.BlockSpec((ROW_BLOCK, d), lambda j, vb, vx, ve, lo, hi: (vb[j], 0)),
            scratch_shapes=[pltpu.VMEM((d, de), BF16), pltpu.VMEM((d, de), BF16),
                            pltpu.VMEM((de, d), BF16)]),
        out_shape=jax.ShapeDtypeStruct((n, d), F32),
        compiler_params=_cparams(("arbitrary",)),
        name="gmm",
    )(vblock, vxblock, vexp, vlo, vhi, xs, wg, wu, wd)


def _combine_kernel(tc_ref, dst_ref, npass_ref, e_ref, lr_ref, w_ref, v_ref, h1_ref, g2_ref,
                    shg_ref, shu_ref, shd_ref, l2g_ref, l2b_ref, ys_ref, o_ref, ybuf, sems):
    i = pl.program_id(0)
    nt = pl.num_programs(0)
    tm, d = v_ref.shape
    nslot = N_EXPERTS * SLOT
    cur = i % 2

    def copies(tile, p, b):
        return _slot_copies(tc_ref, dst_ref, tile, p, ys_ref, ybuf.at[b], False, sems.at[b])

    @pl.when(i == 0)
    def _():
        ybuf[...] = jnp.zeros_like(ybuf)
        copies(0, 0, 0)[0]()

    @pl.when(i + 1 < nt)
    def _():
        copies(i + 1, 0, 1 - cur)[0]()

    vb = v_ref[...]
    hidden = _silu(_dot(vb, shg_ref[...])) * _dot(vb, shu_ref[...])
    shared = _dot(hidden.astype(BF16), shd_ref[...])

    e8 = e_ref[...]
    lr8 = lr_ref[...]
    w8 = w_ref[...]

    def fold(p):
        rel = lr8 - p * SLOT
        tr = jnp.where((rel >= 0) & (rel < SLOT), e8 * SLOT + rel, -1)
        acc = jnp.zeros((tm, d), F32)
        for c0 in range(0, nslot, SLOT_CHUNK):
            lanes = lax.broadcasted_iota(I32, (tm, SLOT_CHUNK), 1) + c0
            pw = jnp.zeros((tm, SLOT_CHUNK), F32)
            for k in range(TOP_K):
                pw = jnp.where(lanes == tr[:, k:k + 1], w8[:, k:k + 1], pw)
            acc = acc + _dot(pw.astype(BF16), ybuf[cur, c0:c0 + SLOT_CHUNK, :].astype(BF16))
        return acc

    copies(i, 0, cur)[1]()
    routed = fold(0)

    def extra_pass(p, acc):
        start, wait = copies(i, p, cur)
        start()
        wait()
        return acc + fold(p)

    routed = lax.fori_loop(1, npass_ref[i], extra_pass, routed)
    f = shared + routed
    o_ref[...] = _ln(ALPHA * h1_ref[...] + g2_ref[...] * f, l2g_ref[...], l2b_ref[...])


def _combine(tc, dst, npass, eidx_t, lrank_t, wts_t, v, h1, g2, shg, shu, shd, l2g, l2b, ys, seq):
    t, d = v.shape
    tm = TOKEN_TILE
    dsh = shg.shape[1]
    nslot = N_EXPERTS * SLOT
    row = pl.BlockSpec((tm, d), lambda i, *_: (i, 0))
    vec = pl.BlockSpec((1, d), lambda i, *_: (0, 0))
    tok = pl.BlockSpec((tm, TOP_K), lambda i, *_: (i, 0))
    return pl.pallas_call(
        _combine_kernel,
        grid_spec=pltpu.PrefetchScalarGridSpec(
            num_scalar_prefetch=3, grid=(t // tm,),
            in_specs=[tok, tok, tok, row, row,
                      pl.BlockSpec((None, 1, d), lambda i, *_: ((i * tm) // seq, 0, 0)),
                      pl.BlockSpec((d, dsh), lambda i, *_: (0, 0)),
                      pl.BlockSpec((d, dsh), lambda i, *_: (0, 0)),
                      pl.BlockSpec((dsh, d), lambda i, *_: (0, 0)), vec, vec,
                      pl.BlockSpec(memory_space=pl.ANY)],
            out_specs=row,
            scratch_shapes=[pltpu.VMEM((2, nslot, d), F32), pltpu.SemaphoreType.DMA((2,))]),
        out_shape=jax.ShapeDtypeStruct((t, d), F32),
        compiler_params=_cparams(("arbitrary",)),
        name="combine",
    )(tc, dst, npass, eidx_t, lrank_t, wts_t, v, h1, g2, shg, shu, shd,
      l2g.reshape(1, d), l2b.reshape(1, d), ys)


def _quad(a0, a1):
    return jnp.concatenate([a0, a1, a0, a1], axis=-1)


def kernel(x, c, ctx, c_ctx, ln_in_g, ln_in_b, ada_w, ada_b, w_in, s5_lam_re, s5_lam_im, s5_log_dt, s5_b_re, s5_b_im, s5_c_re, s5_c_im, s5_d, s5_w_glu, s5_b_glu, lru_conv_w, lru_conv_b, lru_w_a, lru_b_a, lru_w_x, lru_b_x, lru_lam, w_out, ln1_g, ln1_b, router_w, router_bias, exp_w_gate, exp_w_up, exp_w_down, sh_w_gate, sh_w_up, sh_w_down, ln2_g, ln2_b):
    bsz, seq, dm = x.shape
    cl = ctx.shape[1]
    assert ada_w.shape[0] == DEPTH
    d_s5 = s5_w_glu.shape[1]
    d_lru = lru_lam.shape[2]
    ngrp = d_s5 // S5_GROUP
    rows = seq // GRID_W
    t = bsz * seq
    lc = S5_CHUNK
    nkl, nkc = seq // lc, cl // lc

    pad = (-(bsz + 1)) % 8
    cc = jnp.concatenate([c, c_ctx[None, :], jnp.zeros((pad, dm), F32)], axis=0)
    mods = _ada(cc, ada_w[0], ada_b[0])
    sh1, sc1, g1, sh2, sc2, g2 = [mods[:bsz, k * dm:(k + 1) * dm].reshape(bsz, 1, dm) for k in range(6)]
    csh1, csc1 = [mods[bsz:bsz + 1, k * dm:(k + 1) * dm].reshape(1, 1, dm) for k in range(2)]

    x2d = x.reshape(t, dm)
    w_in_b = w_in[0].astype(BF16)
    us5, ulru = _in_proj(x2d, ln_in_g, ln_in_b, sc1, sh1, w_in_b, seq, d_s5)
    ucs5, uclru = _in_proj(ctx.reshape(bsz * cl, dm), ln_in_g, ln_in_b, csc1, csh1, w_in_b, bsz * cl, d_s5)

    def to_chunks(u, nk):
        u = u.reshape(bsz, nk, lc, ngrp, S5_GROUP).transpose(3, 1, 0, 2, 4)
        return u.reshape(ngrp, nk * bsz, lc * S5_GROUP)

    lam3 = jnp.stack([_quad(s5_lam_re[0, 0], s5_lam_re[0, 1]), _quad(s5_lam_im[0, 0], s5_lam_im[0, 1]),
                      _quad(*[jnp.broadcast_to(s5_log_dt[0, k][:, None], (ngrp, S5_STATE)) for k in range(2)])],
                     axis=1)
    bt_re = jnp.swapaxes(s5_b_re[0], -1, -2)
    bt_im = jnp.swapaxes(s5_b_im[0], -1, -2)
    tt, win, wout, a16 = _s5_prep(lam3, _quad(bt_re[0], bt_re[1]), _quad(bt_im[0], bt_im[1]),
                                  _quad(s5_c_re[0, 0], s5_c_re[0, 1]), _quad(s5_c_im[0, 0], s5_c_im[0, 1]),
                                  s5_d[0].reshape(ngrp, S5_GROUP, 1))
    ys = _s5(to_chunks(us5, nkl), to_chunks(ucs5, nkc), tt, win, wout, a16, bsz)
    ys5 = ys.reshape(ngrp, nkl, bsz, lc, S5_GROUP).transpose(2, 1, 3, 0, 4).reshape(t, d_s5)

    c_blk = 256
    hd = d_lru // LRU_HEADS
    hpb = c_blk // hd
    ncb = d_lru // c_blk

    def blockdiag(wh):
        wh = wh.reshape(ncb, hpb, hd, hd)
        eye = jnp.eye(hpb, dtype=wh.dtype)
        return jnp.einsum("nhij,hk->nhikj", wh, eye).reshape(ncb, c_blk, c_blk)

    wg = jnp.concatenate([blockdiag(lru_w_a[0, 0]), blockdiag(lru_w_x[0, 0]),
                          blockdiag(lru_w_a[0, 1]), blockdiag(lru_w_x[0, 1])], axis=-1).astype(BF16)
    bgate = jnp.concatenate([lru_b_a[0, 0].reshape(ncb, 1, c_blk), lru_b_x[0, 0].reshape(ncb, 1, c_blk),
                             lru_b_a[0, 1].reshape(ncb, 1, c_blk), lru_b_x[0, 1].reshape(ncb, 1, c_blk)], axis=-1)
    ylru = _lru(ulru.reshape(bsz, rows, GRID_W, 2 * d_lru), uclru.reshape(bsz, cl, 2 * d_lru),
                lru_conv_w[0], lru_conv_b[0].reshape(1, d_lru), wg, bgate, lru_lam[0], c_blk)
    ylru = ylru.reshape(t, d_lru)

    w_out_b = w_out[0].astype(BF16)
    h1, v, eidx, lrank, wts, tcnt = _mix(
        ys5, ylru, x2d, ln_in_g, ln_in_b, g1, sc2, sh2, s5_w_glu[0].astype(BF16), s5_b_glu[0],
        w_out_b[:d_s5], w_out_b[d_s5:], ln1_g[0], ln1_b[0], router_w[0].T,
        router_bias[0].reshape(N_EXPERTS, 1), seq)

    n_rows = _sorted_rows(t)
    tc = tcnt.transpose(0, 2, 1).reshape(t // TOKEN_TILE, N_EXPERTS)
    dst, npass, slack, tail, vblock, vxblock, vexp, vlo, vhi = _plan(tc, n_rows)
    xs = _dispatch(tc, dst, npass, slack, tail, eidx, lrank, v, n_rows)
    ysort = _gmm(vblock, vxblock, vexp, vlo, vhi, xs, exp_w_gate[0], exp_w_up[0], exp_w_down[0])
    out = _combine(tc, dst, npass, eidx.T, lrank.T, wts.T, v, h1, g2, sh_w_gate[0].astype(BF16),
                   sh_w_up[0].astype(BF16), sh_w_down[0].astype(BF16), ln2_g[0], ln2_b[0], ysort, seq)
    return out.reshape(bsz, seq, dm).astype(x.dtype)
```

```python
import functools
import math

import jax
import jax.numpy as jnp
from jax import lax
from jax.experimental import pallas as pl
from jax.experimental.pallas import tpu as pltpu

F32 = jnp.float32
BF16 = jnp.bfloat16
I32 = jnp.int32
HIGHEST = lax.Precision.HIGHEST

GRID_W = 64
S5_GROUP = 16
S5_STATE = 64
S5_CHUNK = 16
LRU_HEADS = 8
LRU_C = 8.0
CONV_W = 4
CONV_LEFT = 2
N_EXPERTS = 64
TOP_K = 8
N_GROUPS = 8
TOPK_GROUPS = 4
ROUTED_SCALE = 2.5
LN_EPS = 1e-5
DEPTH = 1
ALPHA = (2.0 * DEPTH) ** 0.25

ROW_BLOCK = 256
TOKEN_TILE = 256
SEG_ALIGN = 8
SEG_BITS = (TOKEN_TILE // SEG_ALIGN).bit_length()
NSLOT = TOKEN_TILE * TOP_K + N_EXPERTS * SEG_ALIGN
SLOT_CHUNK = NSLOT // 4
VMEM_LIMIT = 52 * 1024 * 1024


def _cparams(sem):
    return pltpu.CompilerParams(dimension_semantics=sem, vmem_limit_bytes=VMEM_LIMIT)


def _ln(x, g, b):
    mu = jnp.mean(x, axis=-1, keepdims=True)
    xc = x - mu
    var = jnp.mean(xc * xc, axis=-1, keepdims=True)
    return xc * lax.rsqrt(var + LN_EPS) * g + b


def _gelu(x):
    return x * (0.5 * (1.0 + jnp.tanh(math.sqrt(2.0 / math.pi) * (x + 0.044715 * (x * x * x)))))


def _silu(x):
    return x * jax.nn.sigmoid(x)


def _softplus(x):
    return jnp.maximum(x, 0.0) + jnp.log1p(jnp.exp(-jnp.abs(x)))


def _dot(a, b):
    return jnp.dot(a, b, preferred_element_type=F32)


def _dot_nt(a, b, precision=None):
    return lax.dot_general(a, b, (((1,), (1,)), ((), ())), precision=precision,
                           preferred_element_type=F32)


def _ada_kernel(c_ref, w_ref, b_ref, o_ref):
    s = _silu(c_ref[...])
    o_ref[...] = jnp.dot(s, w_ref[...], precision=HIGHEST, preferred_element_type=F32) + b_ref[...]


def _ada(cc, ada_w, ada_b):
    r, d = cc.shape
    n = ada_w.shape[1]
    tn = 512
    return pl.pallas_call(
        _ada_kernel,
        grid=(n // tn,),
        in_specs=[pl.BlockSpec((r, d), lambda j: (0, 0)),
                  pl.BlockSpec((d, tn), lambda j: (0, j)),
                  pl.BlockSpec((1, tn), lambda j: (0, j))],
        out_specs=pl.BlockSpec((r, tn), lambda j: (0, j)),
        out_shape=jax.ShapeDtypeStruct((r, n), F32),
        compiler_params=_cparams(("parallel",)),
        name="ada",
    )(cc, ada_w, ada_b.reshape(1, n))


def _in_proj_kernel(x_ref, g_ref, b_ref, sc_ref, sh_ref, w_ref, us5_ref, ulru_ref):
    h = _ln(x_ref[...], g_ref[...], b_ref[...])
    m = h * (1.0 + sc_ref[...]) + sh_ref[...]
    u = _dot(m.astype(BF16), w_ref[...])
    ds5 = us5_ref.shape[-1]
    us5_ref[...] = u[:, :ds5].astype(BF16)
    ulru_ref[...] = u[:, ds5:].astype(BF16)


def _in_proj(x2d, ln_g, ln_b, sc, sh, w_bf16, rows_per_mod, d_s5, tm=512):
    t, d = x2d.shape
    n = w_bf16.shape[1]
    tm = min(tm, t)
    mod_spec = pl.BlockSpec((None, 1, d), lambda i: ((i * tm) // rows_per_mod, 0, 0))
    vec = pl.BlockSpec((1, d), lambda i: (0, 0))
    return pl.pallas_call(
        _in_proj_kernel,
        grid=(t // tm,),
        in_specs=[pl.BlockSpec((tm, d), lambda i: (i, 0)), vec, vec, mod_spec, mod_spec,
                  pl.BlockSpec((d, n), lambda i: (0, 0))],
        out_specs=[pl.BlockSpec((tm, d_s5), lambda i: (i, 0)),
                   pl.BlockSpec((tm, n - d_s5), lambda i: (i, 0))],
        out_shape=[jax.ShapeDtypeStruct((t, d_s5), BF16),
                   jax.ShapeDtypeStruct((t, n - d_s5), BF16)],
        compiler_params=_cparams(("parallel",)),
        name="in_proj",
    )(x2d, ln_g.reshape(1, d), ln_b.reshape(1, d), sc, sh, w_bf16)


def _s5_prep_kernel(lam_ref, bre_ref, bim_ref, cre_ref, cim_ref, d_ref,
                    tt_ref, win_ref, wout_ref, a16_ref, cp_scr):
    p, lc, gs = S5_STATE, S5_CHUNK, S5_GROUP
    nl = 4 * p
    lre = lam_ref[0:1, :]
    lim = lam_ref[1:2, :]
    dt = jnp.exp(lam_ref[2:3, :])
    lane = lax.broadcasted_iota(I32, (1, nl), 1)
    is_re = lane < 2 * p
    is_f = (lane % (2 * p)) < p

    def powers(ef, eb):
        e = jnp.where(is_f, float(ef), float(eb))
        mag = jnp.exp(lre * dt * e)
        ang = lim * dt * e
        return mag * jnp.cos(ang), mag * jnp.sin(ang)

    bre, bim = bre_ref[...], bim_ref[...]
    cre, cim = cre_ref[...], cim_ref[...]
    ar, ai = powers(1, 1)
    den = lre * lre + lim * lim
    nr = ar - 1.0
    fr = (nr * lre + ai * lim) / den
    fi = (ai * lre - nr * lim) / den
    bbre = fr * bre - fi * bim
    bbim = fr * bim + fi * bre

    def c_times(pc, ps):
        return cre * jnp.where(is_re, pc, -ps) + cim * jnp.where(is_re, -ps, -pc)

    for s in range(lc):
        pc, ps = powers(lc - 1 - s, s)
        qa = jnp.where(is_re, pc, ps)
        qb = jnp.where(is_re, -ps, pc)
        win_ref[gs * s:gs * (s + 1), :] = (bbre * qa + bbim * qb).astype(BF16)
    for t in range(lc):
        pc, ps = powers(t + 1, lc - t)
        wout_ref[gs * t:gs * (t + 1), :] = c_times(pc, ps).astype(BF16)
    for j in range(lc):
        pc, ps = powers(j, lc - 1 - j)
        cp_scr[gs * j:gs * (j + 1), :] = c_times(pc, ps)

    bbcat = jnp.where(is_re, bbre, bbim)
    cp = cp_scr[...]
    ktf = _dot_nt(jnp.where(is_f, bbcat, 0.0), cp, HIGHEST)
    ktb = _dot_nt(jnp.where(is_f, 0.0, bbcat), cp, HIGHEST)
    lane2 = lax.broadcasted_iota(I32, (gs, gs * lc), 1)
    row2 = lax.broadcasted_iota(I32, (gs, gs * lc), 0)
    dcol = d_ref[...]
    width = gs * lc
    for s in range(lc):
        sf = gs * s
        tf = ktf if sf == 0 else pltpu.roll(ktf, sf, 1)
        tf = jnp.where(lane2 >= sf, tf, 0.0)
        sb = (gs * (s + 1)) % width
        tb = ktb if sb == 0 else pltpu.roll(ktb, sb, 1)
        tb = jnp.where(lane2 < gs * (s + 1), tb, 0.0)
        skip = jnp.where(lane2 == sf + row2, dcol, 0.0)
        tt_ref[gs * s:gs * (s + 1), :] = (tf + tb + skip).astype(BF16)
    pc, ps = powers(lc, lc)
    a16_ref[0:1, :] = pc[:, :2 * p]
    a16_ref[1:2, :] = ps[:, :2 * p]


def _s5_prep(lam3, b4re, b4im, c4re, c4im, dcol):
    g = lam3.shape[0]
    nl = 4 * S5_STATE
    k = S5_GROUP * S5_CHUNK
    m3 = lambda i: (i, 0, 0)
    return pl.pallas_call(
        _s5_prep_kernel,
        grid=(g,),
        in_specs=[pl.BlockSpec((None, 3, nl), m3)]
        + [pl.BlockSpec((None, S5_GROUP, nl), m3)] * 4
        + [pl.BlockSpec((None, S5_GROUP, 1), m3)],
        out_specs=[pl.BlockSpec((None, k, k), m3), pl.BlockSpec((None, k, nl), m3),
                   pl.BlockSpec((None, k, nl), m3), pl.BlockSpec((None, 2, nl // 2), m3)],
        out_shape=[jax.ShapeDtypeStruct((g, k, k), BF16), jax.ShapeDtypeStruct((g, k, nl), BF16),
                   jax.ShapeDtypeStruct((g, k, nl), BF16), jax.ShapeDtypeStruct((g, 2, nl // 2), F32)],
        scratch_shapes=[pltpu.VMEM((k, nl), F32)],
        compiler_params=_cparams(("parallel",)),
        name="s5_prep",
    )(lam3, b4re, b4im, c4re, c4im, dcol)


def _s5_kernel(xl_ref, xc_ref, tt_ref, win_ref, wout_ref, a16_ref, y_ref,
               z_scr, zc_scr, sf_scr, sb_scr, *, nkl, nkc, bsz):
    h = 2 * S5_STATE
    xl = xl_ref[...]
    win = win_ref[...]
    z_scr[...] = _dot(xl, win)
    zc_scr[...] = _dot(xc_ref[...], win)
    is_f = lax.broadcasted_iota(I32, (bsz, h), 1) < S5_STATE
    ar = a16_ref[0:1, :]
    ai = a16_ref[1:2, :]

    def pick(z_ref, i, n):
        zf = z_ref[pl.ds(i * bsz, bsz), :]
        zb = z_ref[pl.ds((n - 1 - i) * bsz, bsz), :]
        return jnp.where(is_f, zf[:, :h], zb[:, :h]), jnp.where(is_f, zf[:, h:], zb[:, h:])

    def update(sr, si, zr, zi):
        return ar * sr - ai * si + zr, ar * si + ai * sr + zi

    def ctx_step(i, carry):
        zr, zi = pick(zc_scr, i, nkc)
        return update(*carry, zr, zi)

    def lat_step(i, carry):
        sr, si = carry
        st = jnp.concatenate([sr, si], axis=1)
        sf_scr[pl.ds(i * bsz, bsz), :] = st
        sb_scr[pl.ds((nkl - 1 - i) * bsz, bsz), :] = st
        zr, zi = pick(z_scr, i, nkl)
        return update(sr, si, zr, zi)

    zero = jnp.zeros((bsz, h), F32)
    carry = lax.fori_loop(0, nkc, ctx_step, (zero, zero))
    lax.fori_loop(0, nkl, lat_step, carry)

    is_f4 = (lax.broadcasted_iota(I32, (1, 2 * h), 1) % h) < S5_STATE
    xs = jnp.where(is_f4, sf_scr[...], sb_scr[...]).astype(BF16)
    y = _dot(xl, tt_ref[...]) + _dot_nt(xs, wout_ref[...])
    y_ref[...] = y.astype(BF16)


def _s5(xl, xc, tt, win, wout, a16, bsz):
    g, nl_rows, k = xl.shape
    nc_rows = xc.shape[1]
    nl = 4 * S5_STATE
    m3 = lambda i: (i, 0, 0)
    kern = functools.partial(_s5_kernel, nkl=nl_rows // bsz, nkc=nc_rows // bsz, bsz=bsz)
    return pl.pallas_call(
        kern,
        grid=(g,),
        in_specs=[pl.BlockSpec((None, nl_rows, k), m3), pl.BlockSpec((None, nc_rows, k), m3),
                  pl.BlockSpec((None, k, k), m3), pl.BlockSpec((None, k, nl), m3),
                  pl.BlockSpec((None, k, nl), m3), pl.BlockSpec((None, 2, nl // 2), m3)],
        out_specs=pl.BlockSpec((None, nl_rows, k), m3),
        out_shape=jax.ShapeDtypeStruct((g, nl_rows, k), BF16),
        scratch_shapes=[pltpu.VMEM((nl_rows, nl), F32), pltpu.VMEM((nc_rows, nl), F32),
                        pltpu.VMEM((nl_rows, nl), F32), pltpu.VMEM((nl_rows, nl), F32)],
        compiler_params=_cparams(("parallel",)),
        name="s5",
    )(xl, xc, tt, win, wout, a16)


def _lru_kernel(xv_ref, xg_ref, xc_ref, cw_ref, cb_ref, wg_ref, bg_ref, lam_ref, o_ref,
                a_scr, b_scr, ac_scr, bc_scr, cin_scr, *, rows, cl, slab_block):
    w = GRID_W
    c = xv_ref.shape[-1]
    cw = cw_ref[...]
    cb = cb_ref[...]
    sp = _softplus(-lam_ref[...])
    bg = bg_ref[...]
    wg = wg_ref[...]

    def coeffs(xs, store):
        g = _dot(xs.astype(BF16), wg) + bg
        for d in range(2):
            r = jax.nn.sigmoid(g[:, (2 * d) * c:(2 * d + 1) * c])
            i = jax.nn.sigmoid(g[:, (2 * d + 1) * c:(2 * d + 2) * c])
            log_a = -LRU_C * r * sp[d:d + 1, :]
            a = jnp.exp(log_a)
            b = jnp.sqrt(-jnp.tanh(log_a) * (a * a + 1.0)) * (i * xs)
            store(d, a, b)

    def tap(r0, r1, off):
        lo, hi = r0 + off, r1 + off
        clo, chi = max(lo, 0), min(hi, rows)
        parts = []
        if clo > lo:
            parts.append(jnp.zeros((clo - lo, w, c), F32))
        if chi > clo:
            parts.append(xv_ref[clo:chi].astype(F32))
        if hi > chi:
            parts.append(jnp.zeros((hi - chi, w, c), F32))
        return parts[0] if len(parts) == 1 else jnp.concatenate(parts, axis=0)

    for r0 in range(0, rows, slab_block):
        r1 = r0 + slab_block
        xs3 = cb
        for k in range(CONV_W):
            xs3 = xs3 + cw[k:k + 1, :] * tap(r0, r1, k - CONV_LEFT)
        xs = xs3.reshape(slab_block * w, c)

        def store_lat(d, a, b, r0=r0, r1=r1):
            a_scr[d, r0 * w:r1 * w, :] = a
            b_scr[d, r0 * w:r1 * w, :] = b

        coeffs(xs, store_lat)

    xc = xc_ref[...].astype(F32)
    trow = lax.broadcasted_iota(I32, (cl, c), 0)
    xcs = cb
    for k in range(CONV_W):
        off = k - CONV_LEFT
        sh = xc if off == 0 else pltpu.roll(xc, (-off) % cl, 0)
        ok = (trow + off >= 0) & (trow + off < cl)
        xcs = xcs + cw[k:k + 1, :] * jnp.where(ok, sh, 0.0)

    def store_ctx(d, a, b):
        ac_scr[d] = a
        bc_scr[d] = b

    coeffs(xcs, store_ctx)

    for d in range(2):
        rev = d == 1

        def ctx_step(j, h, d=d, rev=rev):
            idx = (cl - 1 - j) if rev else j
            return ac_scr[d, pl.ds(idx, 1), :] * h + bc_scr[d, pl.ds(idx, 1), :]

        h0 = lax.fori_loop(0, cl, ctx_step, jnp.zeros((1, c), F32))

        def col_step(j, carry, d=d, rev=rev):
            h, p = carry
            r = (rows - 1 - j) if rev else j
            off = pl.multiple_of(r * w, w)
            a = a_scr[d, pl.ds(off, w), :]
            h = a * h + b_scr[d, pl.ds(off, w), :]
            p = a * p
            b_scr[d, pl.ds(off, w), :] = h
            a_scr[d, pl.ds(off, w), :] = p
            return h, p

        lax.fori_loop(0, rows, col_step, (jnp.zeros((w, c), F32), jnp.ones((w, c), F32)))

        last = 0 if rev else (rows - 1) * w

        def carry_step(j, cin, d=d, rev=rev, last=last):
            col = (w - 1 - j) if rev else j
            cin_scr[pl.ds(col, 1), :] = cin
            return a_scr[d, pl.ds(last + col, 1), :] * cin + b_scr[d, pl.ds(last + col, 1), :]

        lax.fori_loop(0, w, carry_step, h0)
        cin = cin_scr[...]

        if not rev:
            def fix_step(r, _, d=d, cin=cin):
                off = pl.multiple_of(r * w, w)
                b_scr[d, pl.ds(off, w), :] = b_scr[d, pl.ds(off, w), :] + a_scr[d, pl.ds(off, w), :] * cin
                return 0

            lax.fori_loop(0, rows, fix_step, 0)
        else:
            def out_step(r, _, cin=cin):
                off = pl.multiple_of(r * w, w)
                hsum = (b_scr[0, pl.ds(off, w), :] + b_scr[1, pl.ds(off, w), :]
                        + a_scr[1, pl.ds(off, w), :] * cin)
                o_ref[r] = (hsum * _gelu(xg_ref[r].astype(F32))).astype(o_ref.dtype)
                return 0

            lax.fori_loop(0, rows, out_step, 0)


def _lru(u4, uc3, conv_w, conv_b, wg, bg, lam, c_blk=256):
    bsz, rows, w, n2 = u4.shape
    d_lru = n2 // 2
    cl = uc3.shape[1]
    ncb = d_lru // c_blk
    kern = functools.partial(_lru_kernel, rows=rows, cl=cl, slab_block=8)
    n = rows * w
    return pl.pallas_call(
        kern,
        grid=(bsz, ncb),
        in_specs=[pl.BlockSpec((None, rows, w, c_blk), lambda b, j: (b, 0, 0, j)),
                  pl.BlockSpec((None, rows, w, c_blk), lambda b, j: (b, 0, 0, ncb + j)),
                  pl.BlockSpec((None, cl, c_blk), lambda b, j: (b, 0, j)),
                  pl.BlockSpec((CONV_W, c_blk), lambda b, j: (0, j)),
                  pl.BlockSpec((1, c_blk), lambda b, j: (0, j)),
                  pl.BlockSpec((None, c_blk, 4 * c_blk), lambda b, j: (j, 0, 0)),
                  pl.BlockSpec((None, 1, 4 * c_blk), lambda b, j: (j, 0, 0)),
                  pl.BlockSpec((2, c_blk), lambda b, j: (0, j))],
        out_specs=pl.BlockSpec((None, rows, w, c_blk), lambda b, j: (b, 0, 0, j)),
        out_shape=jax.ShapeDtypeStruct((bsz, rows, w, d_lru), BF16),
        scratch_shapes=[pltpu.VMEM((2, n, c_blk), F32), pltpu.VMEM((2, n, c_blk), F32),
                        pltpu.VMEM((2, cl, c_blk), F32), pltpu.VMEM((2, cl, c_blk), F32),
                        pltpu.VMEM((w, c_blk), F32)],
        compiler_params=_cparams(("parallel", "parallel")),
        name="lru",
    )(u4, u4, uc3, conv_w, conv_b, wg, bg, lam)


def _mix_kernel(ys5_ref, ylru_ref, x_ref, lng_ref, lnb_ref, g1_ref, sc2_ref, sh2_ref,
                wglu_ref, bglu_ref, wo1_ref, wo2_ref, l1g_ref, l1b_ref, rwt_ref, rb_ref, tri_ref,
                h1_ref, v_ref, eidx_ref, rank_ref, wts_ref, tcnt_ref):
    y = _gelu(ys5_ref[...].astype(F32))
    s5o = y * jax.nn.sigmoid(_dot(y.astype(BF16), wglu_ref[...]) + bglu_ref[...])
    y1 = _dot(s5o.astype(BF16), wo1_ref[...]) + _dot(ylru_ref[...], wo2_ref[...])
    h = _ln(x_ref[...], lng_ref[...], lnb_ref[...])
    h1 = _ln(ALPHA * h + g1_ref[...] * y1, l1g_ref[...], l1b_ref[...])
    h1_ref[...] = h1
    v = h1 * (1.0 + sc2_ref[...]) + sh2_ref[...]
    v_ref[...] = v.astype(v_ref.dtype)

    tm = v.shape[0]
    ne, ng, gsz = N_EXPERTS, N_GROUPS, N_EXPERTS // N_GROUPS
    scores = jax.nn.sigmoid(_dot_nt(rwt_ref[...], v, HIGHEST))
    s3 = scores.reshape(ng, gsz, tm)
    sel3 = (scores + rb_ref[...]).reshape(ng, gsz, tm)
    ii = lax.broadcasted_iota(I32, (ng, gsz, tm), 1)
    gi = lax.broadcasted_iota(I32, (ng, gsz, tm), 0)
    neg = -jnp.inf

    m1 = jnp.max(sel3, axis=1, keepdims=True)
    f1 = jnp.min(jnp.where(sel3 == m1, ii, gsz), axis=1, keepdims=True)
    m2 = jnp.max(jnp.where(ii == f1, neg, sel3), axis=1, keepdims=True)
    cur = m1 + m2
    gidx = lax.broadcasted_iota(I32, (ng, 1, tm), 0)
    gmask = jnp.zeros((ng, 1, tm), jnp.bool_)
    for _ in range(TOPK_GROUPS):
        mx = jnp.max(cur, axis=0, keepdims=True)
        fg = jnp.min(jnp.where(cur == mx, gidx, ng), axis=0, keepdims=True)
        hit = gidx == fg
        gmask = gmask | hit
        cur = jnp.where(hit, neg, cur)

    selm = jnp.where(gmask, sel3, neg)
    eid = gi * gsz + ii
    picks, erows, wrows = [], [], []
    for _ in range(TOP_K):
        mx = jnp.max(jnp.max(selm, axis=1, keepdims=True), axis=0, keepdims=True)
        fe = jnp.min(jnp.min(jnp.where(selm == mx, eid, ne), axis=1, keepdims=True),
                     axis=0, keepdims=True)
        hit = eid == fe
        selm = jnp.where(hit, neg, selm)
        picks.append(hit)
        erows.append(fe.reshape(1, tm))
        wsel = jnp.where(hit, s3, 0.0)
        wrows.append(jnp.sum(jnp.sum(wsel, axis=1, keepdims=True), axis=0).reshape(1, tm))
    denom = wrows[0]
    for k in range(1, TOP_K):
        denom = denom + wrows[k]

    chosen = picks[0]
    for k in range(1, TOP_K):
        chosen = chosen | picks[k]
    chosen_f = jnp.where(chosen, 1.0, 0.0).reshape(ne, tm)
    cnt3 = _dot(chosen_f.astype(BF16), tri_ref[...]).reshape(ng, gsz, tm)
    rrows = []
    for k in range(TOP_K):
        rsel = jnp.where(picks[k], cnt3, 0.0)
        rrows.append(jnp.sum(jnp.sum(rsel, axis=1, keepdims=True), axis=0).reshape(1, tm))
    tcnt_ref[...] = jnp.concatenate(
        [jnp.sum(chosen_f[:, j * TOKEN_TILE:(j + 1) * TOKEN_TILE], axis=1, keepdims=True)
         for j in range(tm // TOKEN_TILE)], axis=1).astype(I32)

    eidx_ref[...] = jnp.concatenate(erows, axis=0)
    rank_ref[...] = jnp.concatenate(rrows, axis=0).astype(I32)
    wts_ref[...] = jnp.concatenate([wr / denom * ROUTED_SCALE for wr in wrows], axis=0)


def _mix(ys5, ylru, x2d, ln_g, ln_b, g1, sc2, sh2, wglu, bglu, wo1, wo2, l1g, l1b, rwt, rb, seq, tm=512):
    t, d = x2d.shape
    ds = ys5.shape[1]
    ne = rwt.shape[0]
    nsub = tm // TOKEN_TILE
    r_i = lax.broadcasted_iota(I32, (tm, tm), 0)
    c_i = lax.broadcasted_iota(I32, (tm, tm), 1)
    tri = ((r_i < c_i) & (r_i // TOKEN_TILE == c_i // TOKEN_TILE)).astype(BF16)
    row = lambda n: pl.BlockSpec((tm, n), lambda i: (i, 0))
    vec = lambda n: pl.BlockSpec((1, n), lambda i: (0, 0))
    mod = pl.BlockSpec((None, 1, d), lambda i: ((i * tm) // seq, 0, 0))
    full = lambda a, b: pl.BlockSpec((a, b), lambda i: (0, 0))
    tok = pl.BlockSpec((TOP_K, tm), lambda i: (0, i))
    return pl.pallas_call(
        _mix_kernel,
        grid=(t // tm,),
        in_specs=[row(ds), row(ds), row(d), vec(d), vec(d), mod, mod, mod,
                  full(ds, ds), vec(ds), full(ds, d), full(ds, d), vec(d), vec(d),
                  full(ne, d), full(ne, 1), full(tm, tm)],
        out_specs=[row(d), row(d), tok, tok, tok,
                   pl.BlockSpec((None, ne, nsub), lambda i: (i, 0, 0))],
        out_shape=[jax.ShapeDtypeStruct((t, d), F32), jax.ShapeDtypeStruct((t, d), BF16),
                   jax.ShapeDtypeStruct((TOP_K, t), I32), jax.ShapeDtypeStruct((TOP_K, t), I32),
                   jax.ShapeDtypeStruct((TOP_K, t), F32),
                   jax.ShapeDtypeStruct((t // tm, ne, nsub), I32)],
        compiler_params=_cparams(("parallel",)),
        name="mix",
    )(ys5, ylru, x2d, ln_g.reshape(1, d), ln_b.reshape(1, d), g1, sc2, sh2,
      wglu, bglu.reshape(1, ds), wo1, wo2, l1g.reshape(1, d), l1b.reshape(1, d), rwt, rb, tri)


def _plan_kernel(tc_ref, dst_ref, soff_ref, utot_ref, start_ref, total_ref):
    tc = tc_ref[...]
    nt, ne = tc.shape
    seg = jnp.bitwise_and(tc + (SEG_ALIGN - 1), -SEG_ALIGN).astype(F32)
    earlier_tile = (lax.broadcasted_iota(I32, (nt, nt), 0) > lax.broadcasted_iota(I32, (nt, nt), 1))
    earlier_exp = (lax.broadcasted_iota(I32, (ne, ne), 0) < lax.broadcasted_iota(I32, (ne, ne), 1))
    lt = jnp.where(earlier_tile, 1.0, 0.0)
    ut = jnp.where(earlier_exp, 1.0, 0.0)
    before = jnp.dot(lt, seg, precision=HIGHEST, preferred_element_type=F32)
    total = jnp.broadcast_to(jnp.sum(seg, axis=0, keepdims=True), (8, ne))
    start = jnp.dot(total, ut, precision=HIGHEST, preferred_element_type=F32)
    dst_ref[...] = (before + start[0:1, :]).astype(I32)
    soff_ref[...] = jnp.dot(seg, ut, precision=HIGHEST, preferred_element_type=F32).astype(I32)
    utot_ref[...] = (jnp.sum(seg, axis=1, keepdims=True) * (1.0 / SEG_ALIGN)).astype(I32)
    start_ref[...] = start[0:1, :].astype(I32)
    total_ref[...] = total[0:1, :].astype(I32)


def _plan(tc):
    nt, ne = tc.shape
    full = lambda a, b: pl.BlockSpec((a, b), lambda: (0, 0))
    return pl.pallas_call(
        _plan_kernel,
        in_specs=[full(nt, ne)],
        out_specs=[full(nt, ne), full(nt, ne), full(nt, 1), full(1, ne), full(1, ne)],
        out_shape=[jax.ShapeDtypeStruct((nt, ne), I32), jax.ShapeDtypeStruct((nt, ne), I32),
                   jax.ShapeDtypeStruct((nt, 1), I32), jax.ShapeDtypeStruct((1, ne), I32),
                   jax.ShapeDtypeStruct((1, ne), I32)],
        name="plan",
    )(tc)


def _visits_kernel(start_ref, total_ref, tail_ref, blk_ref, xblk_ref, exp_ref, lo_ref, hi_ref,
                   *, n_blocks, nv):
    ne = N_EXPERTS
    shift = ROW_BLOCK.bit_length() - 1

    def put(pos, blk, xblk, e, lo, hi):
        blk_ref[pos] = blk
        xblk_ref[pos] = xblk
        exp_ref[pos] = e
        lo_ref[pos] = lo
        hi_ref[pos] = hi

    def per_expert(e, carry):
        pos, done = carry
        off = start_ref[e]
        cnt = total_ref[e]
        first = lax.shift_right_logical(off, shift)
        last = lax.shift_right_logical(off + cnt - 1, shift)
        nvis = jnp.where(cnt > 0, last - first + 1, 0)

        def put_vis(k, c):
            put(pos + k, first + k, first + k, e, off, off + cnt)
            return c

        lax.fori_loop(0, nvis, put_vis, 0)
        return pos + nvis, jnp.where(cnt > 0, last + 1, done)

    pos, done = lax.fori_loop(0, ne, per_expert, (jnp.int32(0), jnp.int32(0)))
    tail_ref[0] = start_ref[ne - 1] + total_ref[ne - 1]

    def put_tail(k, c):
        put(pos + k, done + k, 0, ne - 1, 0, 0)
        return c

    lax.fori_loop(0, n_blocks - done, put_tail, 0)

    def put_rest(j, c):
        put(j, n_blocks - 1, 0, ne - 1, 0, 0)
        return c

    lax.fori_loop(pos + (n_blocks - done), nv, put_rest, 0)


def _sorted_rows(n_tokens):
    nt = n_tokens // TOKEN_TILE
    rows = n_tokens * TOP_K + nt * N_EXPERTS * (SEG_ALIGN - 1)
    return -(-rows // ROW_BLOCK) * ROW_BLOCK


def _visits(start, total, n_rows):
    ne = start.shape[0]
    n_blocks = n_rows // ROW_BLOCK
    nv = n_blocks + ne
    smem = pl.BlockSpec(memory_space=pltpu.SMEM)
    vec = jax.ShapeDtypeStruct((nv,), I32)
    return pl.pallas_call(
        functools.partial(_visits_kernel, n_blocks=n_blocks, nv=nv),
        in_specs=[smem, smem],
        out_specs=[smem] * 6,
        out_shape=[jax.ShapeDtypeStruct((1,), I32), vec, vec, vec, vec, vec],
        name="visits",
    )(start, total)


def _start_segment_copies(tc_ref, dst_ref, soff_ref, tile, hbm_ref, buf, to_hbm, sem):
    unit_shift = SEG_ALIGN.bit_length() - 1

    def body(e, c):
        units = lax.shift_right_logical(tc_ref[tile, e] + (SEG_ALIGN - 1), unit_shift)
        srow = soff_ref[tile, e]
        drow = dst_ref[tile, e]
        for k in range(SEG_BITS):
            @pl.when(jnp.bitwise_and(units, 1 << k) != 0)
            def _(k=k):
                off = jnp.bitwise_and(units, (1 << k) - 1) * SEG_ALIGN
                size = SEG_ALIGN << k
                slot_rows = buf.at[pl.ds(pl.multiple_of(srow + off, SEG_ALIGN), size), :]
                sorted_rows = hbm_ref.at[pl.ds(pl.multiple_of(drow + off, SEG_ALIGN), size), :]
                src, dst = (slot_rows, sorted_rows) if to_hbm else (sorted_rows, slot_rows)
                pltpu.make_async_copy(src, dst, sem).start()
        return c

    lax.fori_loop(0, N_EXPERTS, body, 0)


def _wait_segment_copies(units, hbm_ref, buf, to_hbm, sem):
    for k in range(NSLOT.bit_length() - SEG_ALIGN.bit_length() + 1):
        @pl.when(jnp.bitwise_and(units, 1 << k) != 0)
        def _(k=k):
            size = SEG_ALIGN << k
            slot_rows = buf.at[pl.ds(0, size), :]
            sorted_rows = hbm_ref.at[pl.ds(0, size), :]
            src, dst = (slot_rows, sorted_rows) if to_hbm else (sorted_rows, slot_rows)
            pltpu.make_async_copy(src, dst, sem).wait()


def _dispatch_kernel(tc_ref, dst_ref, soff_ref, utot_ref, tail_ref, eidx_ref, lr_ref, v_ref,
                     xs_ref, srow_ref, cbuf, zbuf, sems, zsem):
    i = pl.program_id(0)
    nt = pl.num_programs(0)
    tm = v_ref.shape[0]
    s = i % 2

    @pl.when(i >= 2)
    def _():
        _wait_segment_copies(utot_ref[i - 2, 0], xs_ref, cbuf.at[s], True, sems.at[s])

    e8 = eidx_ref[...]
    base = jnp.zeros_like(e8)
    for e in range(N_EXPERTS):
        base = jnp.where(e8 == e, soff_ref[i, e], base)
    tr = base + lr_ref[...]
    srow_ref[...] = tr

    vb = v_ref[...]
    for c0 in range(0, NSLOT, SLOT_CHUNK):
        rows = lax.broadcasted_iota(I32, (SLOT_CHUNK, tm), 0) + c0
        onehot = jnp.zeros((SLOT_CHUNK, tm), F32)
        for k in range(TOP_K):
            onehot = jnp.where(rows == tr[k:k + 1, :], 1.0, onehot)
        cbuf[s, c0:c0 + SLOT_CHUNK, :] = _dot(onehot.astype(BF16), vb)
    _start_segment_copies(tc_ref, dst_ref, soff_ref, i, xs_ref, cbuf.at[s], True, sems.at[s])

    @pl.when(i == nt - 1)
    def _():
        @pl.when(i >= 1)
        def _():
            _wait_segment_copies(utot_ref[i - 1, 0], xs_ref, cbuf.at[1 - s], True, sems.at[1 - s])

        _wait_segment_copies(utot_ref[i, 0], xs_ref, cbuf.at[s], True, sems.at[s])
        zbuf[...] = jnp.zeros_like(zbuf)

        def zero_rows(start, size):
            rows = pl.ds(pl.multiple_of(start, SEG_ALIGN), size)
            return pltpu.make_async_copy(zbuf.at[pl.ds(0, size), :], xs_ref.at[rows, :], zsem)

        tail = tail_ref[0]
        n_small = lax.shift_right_logical(jnp.bitwise_and(-tail, ROW_BLOCK - 1),
                                          SEG_ALIGN.bit_length() - 1)
        tail_blk = tail + n_small * SEG_ALIGN
        n_big = lax.shift_right_logical(xs_ref.shape[0] - tail_blk, ROW_BLOCK.bit_length() - 1)

        def each(fn):
            lax.fori_loop(0, n_small, lambda q, c: fn(zero_rows(tail + q * SEG_ALIGN, SEG_ALIGN), c), 0)
            lax.fori_loop(0, n_big, lambda q, c: fn(zero_rows(tail_blk + q * ROW_BLOCK, ROW_BLOCK), c), 0)

        each(lambda cp, c: (cp.start(), c)[1])
        each(lambda cp, c: (cp.wait(), c)[1])


def _dispatch(tc, dst, soff, utot, tail, eidx, lrank, v, n_rows):
    t, d = v.shape
    tm = TOKEN_TILE
    tok = pl.BlockSpec((TOP_K, tm), lambda i, *_: (0, i))
    return pl.pallas_call(
        _dispatch_kernel,
        grid_spec=pltpu.PrefetchScalarGridSpec(
            num_scalar_prefetch=5, grid=(t // tm,),
            in_specs=[tok, tok, pl.BlockSpec((tm, d), lambda i, *_: (i, 0))],
            out_specs=[pl.BlockSpec(memory_space=pl.ANY), tok],
            scratch_shapes=[pltpu.VMEM((2, NSLOT, d), F32), pltpu.VMEM((ROW_BLOCK, d), F32),
                            pltpu.SemaphoreType.DMA((2,)), pltpu.SemaphoreType.DMA]),
        out_shape=[jax.ShapeDtypeStruct((n_rows, d), F32), jax.ShapeDtypeStruct((TOP_K, t), I32)],
        compiler_params=_cparams(("arbitrary",)),
        name="dispatch",
    )(tc, dst, soff, utot, tail, eidx, lrank, v)


def _gmm_kernel(vb_ref, vx_ref, ve_ref, vlo_ref, vhi_ref, x_ref, wg_ref, wu_ref, wd_ref, y_ref,
                wg_b, wu_b, wd_b):
    j = pl.program_id(0)
    prev = jnp.maximum(j - 1, 0)
    first = jnp.logical_or(j == 0, vb_ref[prev] != vb_ref[j])
    live = vhi_ref[j] > vlo_ref[j]

    @pl.when(jnp.logical_or(j == 0, ve_ref[prev] != ve_ref[j]))
    def _():
        wg_b[...] = wg_ref[...].astype(BF16)
        wu_b[...] = wu_ref[...].astype(BF16)
        wd_b[...] = wd_ref[...].astype(BF16)

    @pl.when(live)
    def _():
        x = x_ref[...].astype(BF16)
        hidden = _silu(_dot(x, wg_b[...])) * _dot(x, wu_b[...])
        y = _dot(hidden.astype(BF16), wd_b[...])
        rows = vb_ref[j] * ROW_BLOCK + lax.broadcasted_iota(I32, (ROW_BLOCK, 1), 0)
        mine = (rows >= vlo_ref[j]) & (rows < vhi_ref[j])

        @pl.when(first)
        def _():
            y_ref[...] = jnp.where(mine, y, 0.0)

        @pl.when(jnp.logical_not(first))
        def _():
            y_ref[...] = jnp.where(mine, y, y_ref[...])

    @pl.when(jnp.logical_and(jnp.logical_not(live), first))
    def _():
        y_ref[...] = jnp.zeros_like(y_ref)


def _gmm(vblock, vxblock, vexp, vlo, vhi, xs, wg, wu, wd):
    n, d = xs.shape
    de = wg.shape[2]
    nv = vblock.shape[0]
    return pl.pallas_call(
        _gmm_kernel,
        grid_spec=pltpu.PrefetchScalarGridSpec(
            num_scalar_prefetch=5, grid=(nv,),
            in_specs=[pl.BlockSpec((ROW_BLOCK, d), lambda j, vb, vx, ve, lo, hi: (vx[j], 0)),
                      pl.BlockSpec((None, d, de), lambda j, vb, vx, ve, lo, hi: (ve[j], 0, 0)),
                      pl.BlockSpec((None, d, de), lambda j, vb, vx, ve, lo, hi: (ve[j], 0, 0)),
                      pl.BlockSpec((None, de, d), lambda j, vb, vx, ve, lo, hi: (ve[j], 0, 0))],
            out_specs=pl.BlockSpec((ROW_BLOCK, d), lambda j, vb, vx, ve, lo, hi: (vb[j], 0)),
            scratch_shapes=[pltpu.VMEM((d, de), BF16), pltpu.VMEM((d, de), BF16),
                            pltpu.VMEM((de, d), BF16)]),
        out_shape=jax.ShapeDtypeStruct((n, d), F32),
        compiler_params=_cparams(("arbitrary",)),
        name="gmm",
    )(vblock, vxblock, vexp, vlo, vhi, xs, wg, wu, wd)


def _combine_kernel(tc_ref, dst_ref, soff_ref, utot_ref, srow_ref, w_ref, v_ref, h1_ref, g2_ref,
                    shg_ref, shu_ref, shd_ref, l2g_ref, l2b_ref, ys_ref, o_ref, ybuf, sems):
    i = pl.program_id(0)
    nt = pl.num_programs(0)
    tm, d = v_ref.shape
    cur = i % 2

    def fetch(tile, b):
        _start_segment_copies(tc_ref, dst_ref, soff_ref, tile, ys_ref, ybuf.at[b], False, sems.at[b])

    @pl.when(i == 0)
    def _():
        ybuf[...] = jnp.zeros_like(ybuf)
        fetch(0, 0)

    @pl.when(i + 1 < nt)
    def _():
        fetch(i + 1, 1 - cur)

    vb = v_ref[...]
    hidden = _silu(_dot(vb, shg_ref[...])) * _dot(vb, shu_ref[...])
    f = _dot(hidden.astype(BF16), shd_ref[...])

    tr = srow_ref[...]
    w8 = w_ref[...]
    _wait_segment_copies(utot_ref[i, 0], ys_ref, ybuf.at[cur], False, sems.at[cur])
    for c0 in range(0, NSLOT, SLOT_CHUNK):
        lanes = lax.broadcasted_iota(I32, (tm, SLOT_CHUNK), 1) + c0
        pw = jnp.zeros((tm, SLOT_CHUNK), F32)
        for k in range(TOP_K):
            pw = jnp.where(lanes == tr[:, k:k + 1], w8[:, k:k + 1], pw)
        f = f + _dot(pw.astype(BF16), ybuf[cur, c0:c0 + SLOT_CHUNK, :].astype(BF16))
    o_ref[...] = _ln(ALPHA * h1_ref[...] + g2_ref[...] * f, l2g_ref[...], l2b_ref[...])


def _combine(tc, dst, soff, utot, srow_t, wts_t, v, h1, g2, shg, shu, shd, l2g, l2b, ys, seq):
    t, d = v.shape
    tm = TOKEN_TILE
    dsh = shg.shape[1]
    row = pl.BlockSpec((tm, d), lambda i, *_: (i, 0))
    vec = pl.BlockSpec((1, d), lambda i, *_: (0, 0))
    tok = pl.BlockSpec((tm, TOP_K), lambda i, *_: (i, 0))
    return pl.pallas_call(
        _combine_kernel,
        grid_spec=pltpu.PrefetchScalarGridSpec(
            num_scalar_prefetch=4, grid=(t // tm,),
            in_specs=[tok, tok, row, row,
                      pl.BlockSpec((None, 1, d), lambda i, *_: ((i * tm) // seq, 0, 0)),
                      pl.BlockSpec((d, dsh), lambda i, *_: (0, 0)),
                      pl.BlockSpec((d, dsh), lambda i, *_: (0, 0)),
                      pl.BlockSpec((dsh, d), lambda i, *_: (0, 0)), vec, vec,
                      pl.BlockSpec(memory_space=pl.ANY)],
            out_specs=row,
            scratch_shapes=[pltpu.VMEM((2, NSLOT, d), F32), pltpu.SemaphoreType.DMA((2,))]),
        out_shape=jax.ShapeDtypeStruct((t, d), F32),
        compiler_params=_cparams(("arbitrary",)),
        name="combine",
    )(tc, dst, soff, utot, srow_t, wts_t, v, h1, g2, shg, shu, shd,
      l2g.reshape(1, d), l2b.reshape(1, d), ys)


def _quad(a0, a1):
    return jnp.concatenate([a0, a1, a0, a1], axis=-1)


def kernel(x, c, ctx, c_ctx, ln_in_g, ln_in_b, ada_w, ada_b, w_in, s5_lam_re, s5_lam_im, s5_log_dt, s5_b_re, s5_b_im, s5_c_re, s5_c_im, s5_d, s5_w_glu, s5_b_glu, lru_conv_w, lru_conv_b, lru_w_a, lru_b_a, lru_w_x, lru_b_x, lru_lam, w_out, ln1_g, ln1_b, router_w, router_bias, exp_w_gate, exp_w_up, exp_w_down, sh_w_gate, sh_w_up, sh_w_down, ln2_g, ln2_b):
    bsz, seq, dm = x.shape
    cl = ctx.shape[1]
    assert ada_w.shape[0] == DEPTH
    d_s5 = s5_w_glu.shape[1]
    d_lru = lru_lam.shape[2]
    ngrp = d_s5 // S5_GROUP
    rows = seq // GRID_W
    t = bsz * seq
    lc = S5_CHUNK
    nkl, nkc = seq // lc, cl // lc

    pad = (-(bsz + 1)) % 8
    cc = jnp.concatenate([c, c_ctx[None, :], jnp.zeros((pad, dm), F32)], axis=0)
    mods = _ada(cc, ada_w[0], ada_b[0])
    sh1, sc1, g1, sh2, sc2, g2 = [mods[:bsz, k * dm:(k + 1) * dm].reshape(bsz, 1, dm) for k in range(6)]
    csh1, csc1 = [mods[bsz:bsz + 1, k * dm:(k + 1) * dm].reshape(1, 1, dm) for k in range(2)]

    x2d = x.reshape(t, dm)
    w_in_b = w_in[0].astype(BF16)
    us5, ulru = _in_proj(x2d, ln_in_g, ln_in_b, sc1, sh1, w_in_b, seq, d_s5)
    ucs5, uclru = _in_proj(ctx.reshape(bsz * cl, dm), ln_in_g, ln_in_b, csc1, csh1, w_in_b, bsz * cl, d_s5)

    def to_chunks(u, nk):
        u = u.reshape(bsz, nk, lc, ngrp, S5_GROUP).transpose(3, 1, 0, 2, 4)
        return u.reshape(ngrp, nk * bsz, lc * S5_GROUP)

    lam3 = jnp.stack([_quad(s5_lam_re[0, 0], s5_lam_re[0, 1]), _quad(s5_lam_im[0, 0], s5_lam_im[0, 1]),
                      _quad(*[jnp.broadcast_to(s5_log_dt[0, k][:, None], (ngrp, S5_STATE)) for k in range(2)])],
                     axis=1)
    bt_re = jnp.swapaxes(s5_b_re[0], -1, -2)
    bt_im = jnp.swapaxes(s5_b_im[0], -1, -2)
    tt, win, wout, a16 = _s5_prep(lam3, _quad(bt_re[0], bt_re[1]), _quad(bt_im[0], bt_im[1]),
                                  _quad(s5_c_re[0, 0], s5_c_re[0, 1]), _quad(s5_c_im[0, 0], s5_c_im[0, 1]),
                                  s5_d[0].reshape(ngrp, S5_GROUP, 1))
    ys = _s5(to_chunks(us5, nkl), to_chunks(ucs5, nkc), tt, win, wout, a16, bsz)
    ys5 = ys.reshape(ngrp, nkl, bsz, lc, S5_GROUP).transpose(2, 1, 3, 0, 4).reshape(t, d_s5)

    c_blk = 256
    hd = d_lru // LRU_HEADS
    hpb = c_blk // hd
    ncb = d_lru // c_blk

    def blockdiag(wh):
        wh = wh.reshape(ncb, hpb, hd, hd)
        eye = jnp.eye(hpb, dtype=wh.dtype)
        return jnp.einsum("nhij,hk->nhikj", wh, eye).reshape(ncb, c_blk, c_blk)

    wg = jnp.concatenate([blockdiag(lru_w_a[0, 0]), blockdiag(lru_w_x[0, 0]),
                          blockdiag(lru_w_a[0, 1]), blockdiag(lru_w_x[0, 1])], axis=-1).astype(BF16)
    bgate = jnp.concatenate([lru_b_a[0, 0].reshape(ncb, 1, c_blk), lru_b_x[0, 0].reshape(ncb, 1, c_blk),
                             lru_b_a[0, 1].reshape(ncb, 1, c_blk), lru_b_x[0, 1].reshape(ncb, 1, c_blk)], axis=-1)
    ylru = _lru(ulru.reshape(bsz, rows, GRID_W, 2 * d_lru), uclru.reshape(bsz, cl, 2 * d_lru),
                lru_conv_w[0], lru_conv_b[0].reshape(1, d_lru), wg, bgate, lru_lam[0], c_blk)
    ylru = ylru.reshape(t, d_lru)

    w_out_b = w_out[0].astype(BF16)
    h1, v, eidx, lrank, wts, tcnt = _mix(
        ys5, ylru, x2d, ln_in_g, ln_in_b, g1, sc2, sh2, s5_w_glu[0].astype(BF16), s5_b_glu[0],
        w_out_b[:d_s5], w_out_b[d_s5:], ln1_g[0], ln1_b[0], router_w[0].T,
        router_bias[0].reshape(N_EXPERTS, 1), seq)

    n_rows = _sorted_rows(t)
    tc = tcnt.transpose(0, 2, 1).reshape(t // TOKEN_TILE, N_EXPERTS)
    dst, soff, utot, start, total = _plan(tc)
    tail, vblock, vxblock, vexp, vlo, vhi = _visits(start.reshape(-1), total.reshape(-1), n_rows)
    xs, srow = _dispatch(tc, dst, soff, utot, tail, eidx, lrank, v, n_rows)
    ysort = _gmm(vblock, vxblock, vexp, vlo, vhi, xs, exp_w_gate[0], exp_w_up[0], exp_w_down[0])
    out = _combine(tc, dst, soff, utot, srow.T, wts.T, v, h1, g2, sh_w_gate[0].astype(BF16),
                   sh_w_up[0].astype(BF16), sh_w_down[0].astype(BF16), ln2_g[0], ln2_b[0], ysort, seq)
    return out.reshape(bsz, seq, dm).astype(x.dtype)
```

```python
import functools
import math

import jax
import jax.numpy as jnp
from jax import lax
from jax.experimental import pallas as pl
from jax.experimental.pallas import tpu as pltpu

F32 = jnp.float32
BF16 = jnp.bfloat16
I32 = jnp.int32
HIGHEST = lax.Precision.HIGHEST

GRID_W = 64
S5_GROUP = 16
S5_STATE = 64
S5_CHUNK = 16
LRU_HEADS = 8
LRU_C = 8.0
CONV_W = 4
CONV_LEFT = 2
N_EXPERTS = 64
TOP_K = 8
N_GROUPS = 8
TOPK_GROUPS = 4
ROUTED_SCALE = 2.5
LN_EPS = 1e-5
DEPTH = 1
ALPHA = (2.0 * DEPTH) ** 0.25

ROW_BLOCK = 256
TOKEN_TILE = 256
SEG_ALIGN = 16
SEG_BITS = (TOKEN_TILE // SEG_ALIGN).bit_length()
NSLOT = TOKEN_TILE * TOP_K + N_EXPERTS * SEG_ALIGN
SLOT_CHUNK = NSLOT // 4
VMEM_LIMIT = 52 * 1024 * 1024


def _cparams(sem):
    return pltpu.CompilerParams(dimension_semantics=sem, vmem_limit_bytes=VMEM_LIMIT)


def _ln(x, g, b):
    mu = jnp.mean(x, axis=-1, keepdims=True)
    xc = x - mu
    var = jnp.mean(xc * xc, axis=-1, keepdims=True)
    return xc * lax.rsqrt(var + LN_EPS) * g + b


def _gelu(x):
    return x * (0.5 * (1.0 + jnp.tanh(math.sqrt(2.0 / math.pi) * (x + 0.044715 * (x * x * x)))))


def _silu(x):
    return x * jax.nn.sigmoid(x)


def _softplus(x):
    return jnp.maximum(x, 0.0) + jnp.log1p(jnp.exp(-jnp.abs(x)))


def _dot(a, b):
    return jnp.dot(a, b, preferred_element_type=F32)


def _dot_nt(a, b, precision=None):
    return lax.dot_general(a, b, (((1,), (1,)), ((), ())), precision=precision,
                           preferred_element_type=F32)


def _ada_kernel(c_ref, w_ref, b_ref, o_ref):
    s = _silu(c_ref[...])
    o_ref[...] = jnp.dot(s, w_ref[...], precision=HIGHEST, preferred_element_type=F32) + b_ref[...]


def _ada(cc, ada_w, ada_b):
    r, d = cc.shape
    n = ada_w.shape[1]
    tn = 512
    return pl.pallas_call(
        _ada_kernel,
        grid=(n // tn,),
        in_specs=[pl.BlockSpec((r, d), lambda j: (0, 0)),
                  pl.BlockSpec((d, tn), lambda j: (0, j)),
                  pl.BlockSpec((1, tn), lambda j: (0, j))],
        out_specs=pl.BlockSpec((r, tn), lambda j: (0, j)),
        out_shape=jax.ShapeDtypeStruct((r, n), F32),
        compiler_params=_cparams(("parallel",)),
        name="ada",
    )(cc, ada_w, ada_b.reshape(1, n))


def _in_proj_kernel(x_ref, g_ref, b_ref, sc_ref, sh_ref, w_ref, us5_ref, ulru_ref):
    h = _ln(x_ref[...], g_ref[...], b_ref[...])
    m = h * (1.0 + sc_ref[...]) + sh_ref[...]
    u = _dot(m.astype(BF16), w_ref[...])
    ds5 = us5_ref.shape[-1]
    us5_ref[...] = u[:, :ds5].astype(BF16)
    ulru_ref[...] = u[:, ds5:].astype(BF16)


def _in_proj(x2d, ln_g, ln_b, sc, sh, w_bf16, rows_per_mod, d_s5, tm=512):
    t, d = x2d.shape
    n = w_bf16.shape[1]
    tm = min(tm, t)
    mod_spec = pl.BlockSpec((None, 1, d), lambda i: ((i * tm) // rows_per_mod, 0, 0))
    vec = pl.BlockSpec((1, d), lambda i: (0, 0))
    return pl.pallas_call(
        _in_proj_kernel,
        grid=(t // tm,),
        in_specs=[pl.BlockSpec((tm, d), lambda i: (i, 0)), vec, vec, mod_spec, mod_spec,
                  pl.BlockSpec((d, n), lambda i: (0, 0))],
        out_specs=[pl.BlockSpec((tm, d_s5), lambda i: (i, 0)),
                   pl.BlockSpec((tm, n - d_s5), lambda i: (i, 0))],
        out_shape=[jax.ShapeDtypeStruct((t, d_s5), BF16),
                   jax.ShapeDtypeStruct((t, n - d_s5), BF16)],
        compiler_params=_cparams(("parallel",)),
        name="in_proj",
    )(x2d, ln_g.reshape(1, d), ln_b.reshape(1, d), sc, sh, w_bf16)


def _s5_prep_kernel(lam_ref, bre_ref, bim_ref, cre_ref, cim_ref, d_ref,
                    tt_ref, win_ref, wout_ref, a16_ref, cp_scr):
    p, lc, gs = S5_STATE, S5_CHUNK, S5_GROUP
    nl = 4 * p
    lre = lam_ref[0:1, :]
    lim = lam_ref[1:2, :]
    dt = jnp.exp(lam_ref[2:3, :])
    lane = lax.broadcasted_iota(I32, (1, nl), 1)
    is_re = lane < 2 * p
    is_f = (lane % (2 * p)) < p

    def powers(ef, eb):
        e = jnp.where(is_f, float(ef), float(eb))
        mag = jnp.exp(lre * dt * e)
        ang = lim * dt * e
        return mag * jnp.cos(ang), mag * jnp.sin(ang)

    bre, bim = bre_ref[...], bim_ref[...]
    cre, cim = cre_ref[...], cim_ref[...]
    ar, ai = powers(1, 1)
    den = lre * lre + lim * lim
    nr = ar - 1.0
    fr = (nr * lre + ai * lim) / den
    fi = (ai * lre - nr * lim) / den
    bbre = fr * bre - fi * bim
    bbim = fr * bim + fi * bre

    def c_times(pc, ps):
        return cre * jnp.where(is_re, pc, -ps) + cim * jnp.where(is_re, -ps, -pc)

    for s in range(lc):
        pc, ps = powers(lc - 1 - s, s)
        qa = jnp.where(is_re, pc, ps)
        qb = jnp.where(is_re, -ps, pc)
        win_ref[gs * s:gs * (s + 1), :] = (bbre * qa + bbim * qb).astype(BF16)
    for t in range(lc):
        pc, ps = powers(t + 1, lc - t)
        wout_ref[gs * t:gs * (t + 1), :] = c_times(pc, ps).astype(BF16)
    for j in range(lc):
        pc, ps = powers(j, lc - 1 - j)
        cp_scr[gs * j:gs * (j + 1), :] = c_times(pc, ps)

    bbcat = jnp.where(is_re, bbre, bbim)
    cp = cp_scr[...]
    ktf = _dot_nt(jnp.where(is_f, bbcat, 0.0), cp, HIGHEST)
    ktb = _dot_nt(jnp.where(is_f, 0.0, bbcat), cp, HIGHEST)
    lane2 = lax.broadcasted_iota(I32, (gs, gs * lc), 1)
    row2 = lax.broadcasted_iota(I32, (gs, gs * lc), 0)
    dcol = d_ref[...]
    width = gs * lc
    for s in range(lc):
        sf = gs * s
        tf = ktf if sf == 0 else pltpu.roll(ktf, sf, 1)
        tf = jnp.where(lane2 >= sf, tf, 0.0)
        sb = (gs * (s + 1)) % width
        tb = ktb if sb == 0 else pltpu.roll(ktb, sb, 1)
        tb = jnp.where(lane2 < gs * (s + 1), tb, 0.0)
        skip = jnp.where(lane2 == sf + row2, dcol, 0.0)
        tt_ref[gs * s:gs * (s + 1), :] = (tf + tb + skip).astype(BF16)
    pc, ps = powers(lc, lc)
    a16_ref[0:1, :] = pc[:, :2 * p]
    a16_ref[1:2, :] = ps[:, :2 * p]


def _s5_prep(lam3, b4re, b4im, c4re, c4im, dcol):
    g = lam3.shape[0]
    nl = 4 * S5_STATE
    k = S5_GROUP * S5_CHUNK
    m3 = lambda i: (i, 0, 0)
    return pl.pallas_call(
        _s5_prep_kernel,
        grid=(g,),
        in_specs=[pl.BlockSpec((None, 3, nl), m3)]
        + [pl.BlockSpec((None, S5_GROUP, nl), m3)] * 4
        + [pl.BlockSpec((None, S5_GROUP, 1), m3)],
        out_specs=[pl.BlockSpec((None, k, k), m3), pl.BlockSpec((None, k, nl), m3),
                   pl.BlockSpec((None, k, nl), m3), pl.BlockSpec((None, 2, nl // 2), m3)],
        out_shape=[jax.ShapeDtypeStruct((g, k, k), BF16), jax.ShapeDtypeStruct((g, k, nl), BF16),
                   jax.ShapeDtypeStruct((g, k, nl), BF16), jax.ShapeDtypeStruct((g, 2, nl // 2), F32)],
        scratch_shapes=[pltpu.VMEM((k, nl), F32)],
        compiler_params=_cparams(("parallel",)),
        name="s5_prep",
    )(lam3, b4re, b4im, c4re, c4im, dcol)


def _s5_kernel(xl_ref, xc_ref, tt_ref, win_ref, wout_ref, a16_ref, y_ref,
               z_scr, zc_scr, sf_scr, sb_scr, *, nkl, nkc, bsz):
    h = 2 * S5_STATE
    xl = xl_ref[...]
    win = win_ref[...]
    z_scr[...] = _dot(xl, win)
    zc_scr[...] = _dot(xc_ref[...], win)
    is_f = lax.broadcasted_iota(I32, (bsz, h), 1) < S5_STATE
    ar = a16_ref[0:1, :]
    ai = a16_ref[1:2, :]

    def pick(z_ref, i, n):
        zf = z_ref[pl.ds(i * bsz, bsz), :]
        zb = z_ref[pl.ds((n - 1 - i) * bsz, bsz), :]
        return jnp.where(is_f, zf[:, :h], zb[:, :h]), jnp.where(is_f, zf[:, h:], zb[:, h:])

    def update(sr, si, zr, zi):
        return ar * sr - ai * si + zr, ar * si + ai * sr + zi

    def ctx_step(i, carry):
        zr, zi = pick(zc_scr, i, nkc)
        return update(*carry, zr, zi)

    def lat_step(i, carry):
        sr, si = carry
        st = jnp.concatenate([sr, si], axis=1)
        sf_scr[pl.ds(i * bsz, bsz), :] = st
        sb_scr[pl.ds((nkl - 1 - i) * bsz, bsz), :] = st
        zr, zi = pick(z_scr, i, nkl)
        return update(sr, si, zr, zi)

    zero = jnp.zeros((bsz, h), F32)
    carry = lax.fori_loop(0, nkc, ctx_step, (zero, zero))
    lax.fori_loop(0, nkl, lat_step, carry)

    is_f4 = (lax.broadcasted_iota(I32, (1, 2 * h), 1) % h) < S5_STATE
    xs = jnp.where(is_f4, sf_scr[...], sb_scr[...]).astype(BF16)
    y = _dot(xl, tt_ref[...]) + _dot_nt(xs, wout_ref[...])
    y_ref[...] = y.astype(BF16)


def _s5(xl, xc, tt, win, wout, a16, bsz):
    g, nl_rows, k = xl.shape
    nc_rows = xc.shape[1]
    nl = 4 * S5_STATE
    m3 = lambda i: (i, 0, 0)
    kern = functools.partial(_s5_kernel, nkl=nl_rows // bsz, nkc=nc_rows // bsz, bsz=bsz)
    return pl.pallas_call(
        kern,
        grid=(g,),
        in_specs=[pl.BlockSpec((None, nl_rows, k), m3), pl.BlockSpec((None, nc_rows, k), m3),
                  pl.BlockSpec((None, k, k), m3), pl.BlockSpec((None, k, nl), m3),
                  pl.BlockSpec((None, k, nl), m3), pl.BlockSpec((None, 2, nl // 2), m3)],
        out_specs=pl.BlockSpec((None, nl_rows, k), m3),
        out_shape=jax.ShapeDtypeStruct((g, nl_rows, k), BF16),
        scratch_shapes=[pltpu.VMEM((nl_rows, nl), F32), pltpu.VMEM((nc_rows, nl), F32),
                        pltpu.VMEM((nl_rows, nl), F32), pltpu.VMEM((nl_rows, nl), F32)],
        compiler_params=_cparams(("parallel",)),
        name="s5",
    )(xl, xc, tt, win, wout, a16)


def _lru_kernel(xv_ref, xg_ref, xc_ref, cw_ref, cb_ref, wg_ref, bg_ref, lam_ref, o_ref,
                a_scr, b_scr, ac_scr, bc_scr, cin_scr, *, rows, cl, slab_block):
    w = GRID_W
    c = xv_ref.shape[-1]
    cw = cw_ref[...]
    cb = cb_ref[...]
    sp = _softplus(-lam_ref[...])
    bg = bg_ref[...]
    wg = wg_ref[...]

    def coeffs(xs, store):
        g = _dot(xs.astype(BF16), wg) + bg
        for d in range(2):
            r = jax.nn.sigmoid(g[:, (2 * d) * c:(2 * d + 1) * c])
            i = jax.nn.sigmoid(g[:, (2 * d + 1) * c:(2 * d + 2) * c])
            log_a = -LRU_C * r * sp[d:d + 1, :]
            a = jnp.exp(log_a)
            b = jnp.sqrt(-jnp.tanh(log_a) * (a * a + 1.0)) * (i * xs)
            store(d, a, b)

    def tap(r0, r1, off):
        lo, hi = r0 + off, r1 + off
        clo, chi = max(lo, 0), min(hi, rows)
        parts = []
        if clo > lo:
            parts.append(jnp.zeros((clo - lo, w, c), F32))
        if chi > clo:
            parts.append(xv_ref[clo:chi].astype(F32))
        if hi > chi:
            parts.append(jnp.zeros((hi - chi, w, c), F32))
        return parts[0] if len(parts) == 1 else jnp.concatenate(parts, axis=0)

    for r0 in range(0, rows, slab_block):
        r1 = r0 + slab_block
        xs3 = cb
        for k in range(CONV_W):
            xs3 = xs3 + cw[k:k + 1, :] * tap(r0, r1, k - CONV_LEFT)
        xs = xs3.reshape(slab_block * w, c)

        def store_lat(d, a, b, r0=r0, r1=r1):
            a_scr[d, r0 * w:r1 * w, :] = a
            b_scr[d, r0 * w:r1 * w, :] = b

        coeffs(xs, store_lat)

    xc = xc_ref[...].astype(F32)
    trow = lax.broadcasted_iota(I32, (cl, c), 0)
    xcs = cb
    for k in range(CONV_W):
        off = k - CONV_LEFT
        sh = xc if off == 0 else pltpu.roll(xc, (-off) % cl, 0)
        ok = (trow + off >= 0) & (trow + off < cl)
        xcs = xcs + cw[k:k + 1, :] * jnp.where(ok, sh, 0.0)

    def store_ctx(d, a, b):
        ac_scr[d] = a
        bc_scr[d] = b

    coeffs(xcs, store_ctx)

    for d in range(2):
        rev = d == 1

        def ctx_step(j, h, d=d, rev=rev):
            idx = (cl - 1 - j) if rev else j
            return ac_scr[d, pl.ds(idx, 1), :] * h + bc_scr[d, pl.ds(idx, 1), :]

        h0 = lax.fori_loop(0, cl, ctx_step, jnp.zeros((1, c), F32))

        def col_step(j, carry, d=d, rev=rev):
            h, p = carry
            r = (rows - 1 - j) if rev else j
            off = pl.multiple_of(r * w, w)
            a = a_scr[d, pl.ds(off, w), :]
            h = a * h + b_scr[d, pl.ds(off, w), :]
            p = a * p
            b_scr[d, pl.ds(off, w), :] = h
            a_scr[d, pl.ds(off, w), :] = p
            return h, p

        lax.fori_loop(0, rows, col_step, (jnp.zeros((w, c), F32), jnp.ones((w, c), F32)))

        last = 0 if rev else (rows - 1) * w

        def carry_step(j, cin, d=d, rev=rev, last=last):
            col = (w - 1 - j) if rev else j
            cin_scr[pl.ds(col, 1), :] = cin
            return a_scr[d, pl.ds(last + col, 1), :] * cin + b_scr[d, pl.ds(last + col, 1), :]

        lax.fori_loop(0, w, carry_step, h0)
        cin = cin_scr[...]

        if not rev:
            def fix_step(r, _, d=d, cin=cin):
                off = pl.multiple_of(r * w, w)
                b_scr[d, pl.ds(off, w), :] = b_scr[d, pl.ds(off, w), :] + a_scr[d, pl.ds(off, w), :] * cin
                return 0

            lax.fori_loop(0, rows, fix_step, 0)
        else:
            def out_step(r, _, cin=cin):
                off = pl.multiple_of(r * w, w)
                hsum = (b_scr[0, pl.ds(off, w), :] + b_scr[1, pl.ds(off, w), :]
                        + a_scr[1, pl.ds(off, w), :] * cin)
                o_ref[r] = (hsum * _gelu(xg_ref[r].astype(F32))).astype(o_ref.dtype)
                return 0

            lax.fori_loop(0, rows, out_step, 0)


def _lru(u4, uc3, conv_w, conv_b, wg, bg, lam, c_blk=256):
    bsz, rows, w, n2 = u4.shape
    d_lru = n2 // 2
    cl = uc3.shape[1]
    ncb = d_lru // c_blk
    kern = functools.partial(_lru_kernel, rows=rows, cl=cl, slab_block=8)
    n = rows * w
    return pl.pallas_call(
        kern,
        grid=(bsz, ncb),
        in_specs=[pl.BlockSpec((None, rows, w, c_blk), lambda b, j: (b, 0, 0, j)),
                  pl.BlockSpec((None, rows, w, c_blk), lambda b, j: (b, 0, 0, ncb + j)),
                  pl.BlockSpec((None, cl, c_blk), lambda b, j: (b, 0, j)),
                  pl.BlockSpec((CONV_W, c_blk), lambda b, j: (0, j)),
                  pl.BlockSpec((1, c_blk), lambda b, j: (0, j)),
                  pl.BlockSpec((None, c_blk, 4 * c_blk), lambda b, j: (j, 0, 0)),
                  pl.BlockSpec((None, 1, 4 * c_blk), lambda b, j: (j, 0, 0)),
                  pl.BlockSpec((2, c_blk), lambda b, j: (0, j))],
        out_specs=pl.BlockSpec((None, rows, w, c_blk), lambda b, j: (b, 0, 0, j)),
        out_shape=jax.ShapeDtypeStruct((bsz, rows, w, d_lru), BF16),
        scratch_shapes=[pltpu.VMEM((2, n, c_blk), F32), pltpu.VMEM((2, n, c_blk), F32),
                        pltpu.VMEM((2, cl, c_blk), F32), pltpu.VMEM((2, cl, c_blk), F32),
                        pltpu.VMEM((w, c_blk), F32)],
        compiler_params=_cparams(("parallel", "parallel")),
        name="lru",
    )(u4, u4, uc3, conv_w, conv_b, wg, bg, lam)


def _mix_kernel(ys5_ref, ylru_ref, x_ref, lng_ref, lnb_ref, g1_ref, sc2_ref, sh2_ref,
                wglu_ref, bglu_ref, wo1_ref, wo2_ref, l1g_ref, l1b_ref, rwt_ref, rb_ref, tri_ref,
                h1_ref, v_ref, eidx_ref, rank_ref, wts_ref, tcnt_ref):
    y = _gelu(ys5_ref[...].astype(F32))
    s5o = y * jax.nn.sigmoid(_dot(y.astype(BF16), wglu_ref[...]) + bglu_ref[...])
    y1 = _dot(s5o.astype(BF16), wo1_ref[...]) + _dot(ylru_ref[...], wo2_ref[...])
    h = _ln(x_ref[...], lng_ref[...], lnb_ref[...])
    h1 = _ln(ALPHA * h + g1_ref[...] * y1, l1g_ref[...], l1b_ref[...])
    h1_ref[...] = h1
    v = h1 * (1.0 + sc2_ref[...]) + sh2_ref[...]
    v_ref[...] = v.astype(v_ref.dtype)

    tm = v.shape[0]
    ne, ng, gsz = N_EXPERTS, N_GROUPS, N_EXPERTS // N_GROUPS
    scores = jax.nn.sigmoid(_dot_nt(rwt_ref[...], v, HIGHEST))
    s3 = scores.reshape(ng, gsz, tm)
    sel3 = (scores + rb_ref[...]).reshape(ng, gsz, tm)
    ii = lax.broadcasted_iota(I32, (ng, gsz, tm), 1)
    gi = lax.broadcasted_iota(I32, (ng, gsz, tm), 0)
    neg = -jnp.inf

    m1 = jnp.max(sel3, axis=1, keepdims=True)
    f1 = jnp.min(jnp.where(sel3 == m1, ii, gsz), axis=1, keepdims=True)
    m2 = jnp.max(jnp.where(ii == f1, neg, sel3), axis=1, keepdims=True)
    cur = m1 + m2
    gidx = lax.broadcasted_iota(I32, (ng, 1, tm), 0)
    gmask = jnp.zeros((ng, 1, tm), jnp.bool_)
    for _ in range(TOPK_GROUPS):
        mx = jnp.max(cur, axis=0, keepdims=True)
        fg = jnp.min(jnp.where(cur == mx, gidx, ng), axis=0, keepdims=True)
        hit = gidx == fg
        gmask = gmask | hit
        cur = jnp.where(hit, neg, cur)

    selm = jnp.where(gmask, sel3, neg)
    eid = gi * gsz + ii
    picks, erows, wrows = [], [], []
    for _ in range(TOP_K):
        mx = jnp.max(jnp.max(selm, axis=1, keepdims=True), axis=0, keepdims=True)
        fe = jnp.min(jnp.min(jnp.where(selm == mx, eid, ne), axis=1, keepdims=True),
                     axis=0, keepdims=True)
        hit = eid == fe
        selm = jnp.where(hit, neg, selm)
        picks.append(hit)
        erows.append(fe.reshape(1, tm))
        wsel = jnp.where(hit, s3, 0.0)
        wrows.append(jnp.sum(jnp.sum(wsel, axis=1, keepdims=True), axis=0).reshape(1, tm))
    denom = wrows[0]
    for k in range(1, TOP_K):
        denom = denom + wrows[k]

    chosen = picks[0]
    for k in range(1, TOP_K):
        chosen = chosen | picks[k]
    chosen_f = jnp.where(chosen, 1.0, 0.0).reshape(ne, tm)
    cnt3 = _dot(chosen_f.astype(BF16), tri_ref[...]).reshape(ng, gsz, tm)
    rrows = []
    for k in range(TOP_K):
        rsel = jnp.where(picks[k], cnt3, 0.0)
        rrows.append(jnp.sum(jnp.sum(rsel, axis=1, keepdims=True), axis=0).reshape(1, tm))
    tcnt_ref[...] = jnp.concatenate(
        [jnp.sum(chosen_f[:, j * TOKEN_TILE:(j + 1) * TOKEN_TILE], axis=1, keepdims=True)
         for j in range(tm // TOKEN_TILE)], axis=1).astype(I32)

    eidx_ref[...] = jnp.concatenate(erows, axis=0)
    rank_ref[...] = jnp.concatenate(rrows, axis=0).astype(I32)
    wts_ref[...] = jnp.concatenate([wr / denom * ROUTED_SCALE for wr in wrows], axis=0)


def _mix(ys5, ylru, x2d, ln_g, ln_b, g1, sc2, sh2, wglu, bglu, wo1, wo2, l1g, l1b, rwt, rb, seq, tm=512):
    t, d = x2d.shape
    ds = ys5.shape[1]
    ne = rwt.shape[0]
    nsub = tm // TOKEN_TILE
    r_i = lax.broadcasted_iota(I32, (tm, tm), 0)
    c_i = lax.broadcasted_iota(I32, (tm, tm), 1)
    tri = ((r_i < c_i) & (r_i // TOKEN_TILE == c_i // TOKEN_TILE)).astype(BF16)
    row = lambda n: pl.BlockSpec((tm, n), lambda i: (i, 0))
    vec = lambda n: pl.BlockSpec((1, n), lambda i: (0, 0))
    mod = pl.BlockSpec((None, 1, d), lambda i: ((i * tm) // seq, 0, 0))
    full = lambda a, b: pl.BlockSpec((a, b), lambda i: (0, 0))
    tok = pl.BlockSpec((TOP_K, tm), lambda i: (0, i))
    return pl.pallas_call(
        _mix_kernel,
        grid=(t // tm,),
        in_specs=[row(ds), row(ds), row(d), vec(d), vec(d), mod, mod, mod,
                  full(ds, ds), vec(ds), full(ds, d), full(ds, d), vec(d), vec(d),
                  full(ne, d), full(ne, 1), full(tm, tm)],
        out_specs=[row(d), row(d), tok, tok, tok,
                   pl.BlockSpec((None, ne, nsub), lambda i: (i, 0, 0))],
        out_shape=[jax.ShapeDtypeStruct((t, d), F32), jax.ShapeDtypeStruct((t, d), BF16),
                   jax.ShapeDtypeStruct((TOP_K, t), I32), jax.ShapeDtypeStruct((TOP_K, t), I32),
                   jax.ShapeDtypeStruct((TOP_K, t), F32),
                   jax.ShapeDtypeStruct((t // tm, ne, nsub), I32)],
        compiler_params=_cparams(("parallel",)),
        name="mix",
    )(ys5, ylru, x2d, ln_g.reshape(1, d), ln_b.reshape(1, d), g1, sc2, sh2,
      wglu, bglu.reshape(1, ds), wo1, wo2, l1g.reshape(1, d), l1b.reshape(1, d), rwt, rb, tri)


def _plan_kernel(tc_ref, dst_ref, soff_ref, utot_ref, start_ref, total_ref):
    tc = tc_ref[...]
    nt, ne = tc.shape
    seg = jnp.bitwise_and(tc + (SEG_ALIGN - 1), -SEG_ALIGN).astype(F32)
    earlier_tile = (lax.broadcasted_iota(I32, (nt, nt), 0) > lax.broadcasted_iota(I32, (nt, nt), 1))
    earlier_exp = (lax.broadcasted_iota(I32, (ne, ne), 0) < lax.broadcasted_iota(I32, (ne, ne), 1))
    lt = jnp.where(earlier_tile, 1.0, 0.0)
    ut = jnp.where(earlier_exp, 1.0, 0.0)
    before = jnp.dot(lt, seg, precision=HIGHEST, preferred_element_type=F32)
    total = jnp.broadcast_to(jnp.sum(seg, axis=0, keepdims=True), (8, ne))
    start = jnp.dot(total, ut, precision=HIGHEST, preferred_element_type=F32)
    dst_ref[...] = (before + start[0:1, :]).astype(I32)
    soff_ref[...] = jnp.dot(seg, ut, precision=HIGHEST, preferred_element_type=F32).astype(I32)
    utot_ref[...] = (jnp.sum(seg, axis=1, keepdims=True) * (1.0 / SEG_ALIGN)).astype(I32)
    start_ref[...] = start[0:1, :].astype(I32)
    total_ref[...] = total[0:1, :].astype(I32)


def _plan(tc):
    nt, ne = tc.shape
    full = lambda a, b: pl.BlockSpec((a, b), lambda: (0, 0))
    return pl.pallas_call(
        _plan_kernel,
        in_specs=[full(nt, ne)],
        out_specs=[full(nt, ne), full(nt, ne), full(nt, 1), full(1, ne), full(1, ne)],
        out_shape=[jax.ShapeDtypeStruct((nt, ne), I32), jax.ShapeDtypeStruct((nt, ne), I32),
                   jax.ShapeDtypeStruct((nt, 1), I32), jax.ShapeDtypeStruct((1, ne), I32),
                   jax.ShapeDtypeStruct((1, ne), I32)],
        name="plan",
    )(tc)


def _visits_kernel(start_ref, total_ref, tail_ref, blk_ref, xblk_ref, exp_ref, lo_ref, hi_ref,
                   *, n_blocks, nv):
    ne = N_EXPERTS
    shift = ROW_BLOCK.bit_length() - 1

    def put(pos, blk, xblk, e, lo, hi):
        blk_ref[pos] = blk
        xblk_ref[pos] = xblk
        exp_ref[pos] = e
        lo_ref[pos] = lo
        hi_ref[pos] = hi

    def per_expert(e, carry):
        pos, done = carry
        off = start_ref[e]
        cnt = total_ref[e]
        first = lax.shift_right_logical(off, shift)
        last = lax.shift_right_logical(off + cnt - 1, shift)
        nvis = jnp.where(cnt > 0, last - first + 1, 0)

        def put_vis(k, c):
            put(pos + k, first + k, first + k, e, off, off + cnt)
            return c

        lax.fori_loop(0, nvis, put_vis, 0)
        return pos + nvis, jnp.where(cnt > 0, last + 1, done)

    pos, done = lax.fori_loop(0, ne, per_expert, (jnp.int32(0), jnp.int32(0)))
    tail_ref[0] = start_ref[ne - 1] + total_ref[ne - 1]

    def put_tail(k, c):
        put(pos + k, done + k, 0, ne - 1, 0, 0)
        return c

    lax.fori_loop(0, n_blocks - done, put_tail, 0)

    def put_rest(j, c):
        put(j, n_blocks - 1, 0, ne - 1, 0, 0)
        return c

    lax.fori_loop(pos + (n_blocks - done), nv, put_rest, 0)


def _sorted_rows(n_tokens):
    nt = n_tokens // TOKEN_TILE
    rows = n_tokens * TOP_K + nt * N_EXPERTS * (SEG_ALIGN - 1)
    return -(-rows // ROW_BLOCK) * ROW_BLOCK


def _visits(start, total, n_rows):
    ne = start.shape[0]
    n_blocks = n_rows // ROW_BLOCK
    nv = n_blocks + ne
    smem = pl.BlockSpec(memory_space=pltpu.SMEM)
    vec = jax.ShapeDtypeStruct((nv,), I32)
    return pl.pallas_call(
        functools.partial(_visits_kernel, n_blocks=n_blocks, nv=nv),
        in_specs=[smem, smem],
        out_specs=[smem] * 6,
        out_shape=[jax.ShapeDtypeStruct((1,), I32), vec, vec, vec, vec, vec],
        name="visits",
    )(start, total)


def _start_segment_copies(tc_ref, dst_ref, soff_ref, tile, hbm_ref, buf, to_hbm, sem):
    unit_shift = SEG_ALIGN.bit_length() - 1

    def body(e, c):
        units = lax.shift_right_logical(tc_ref[tile, e] + (SEG_ALIGN - 1), unit_shift)
        srow = soff_ref[tile, e]
        drow = dst_ref[tile, e]
        for k in range(SEG_BITS):
            @pl.when(jnp.bitwise_and(units, 1 << k) != 0)
            def _(k=k):
                off = jnp.bitwise_and(units, (1 << k) - 1) * SEG_ALIGN
                size = SEG_ALIGN << k
                slot_rows = buf.at[pl.ds(pl.multiple_of(srow + off, SEG_ALIGN), size), :]
                sorted_rows = hbm_ref.at[pl.ds(pl.multiple_of(drow + off, SEG_ALIGN), size), :]
                src, dst = (slot_rows, sorted_rows) if to_hbm else (sorted_rows, slot_rows)
                pltpu.make_async_copy(src, dst, sem).start()
        return c

    lax.fori_loop(0, N_EXPERTS, body, 0)


def _wait_segment_copies(units, hbm_ref, buf, to_hbm, sem):
    for k in range(NSLOT.bit_length() - SEG_ALIGN.bit_length() + 1):
        @pl.when(jnp.bitwise_and(units, 1 << k) != 0)
        def _(k=k):
            size = SEG_ALIGN << k
            slot_rows = buf.at[pl.ds(0, size), :]
            sorted_rows = hbm_ref.at[pl.ds(0, size), :]
            src, dst = (slot_rows, sorted_rows) if to_hbm else (sorted_rows, slot_rows)
            pltpu.make_async_copy(src, dst, sem).wait()


def _dispatch_kernel(tc_ref, dst_ref, soff_ref, utot_ref, tail_ref, eidx_ref, lr_ref, v_ref,
                     xs_ref, srow_ref, cbuf, zbuf, sems, zsem):
    i = pl.program_id(0)
    nt = pl.num_programs(0)
    tm = v_ref.shape[0]
    s = i % 2

    @pl.when(i >= 2)
    def _():
        _wait_segment_copies(utot_ref[i - 2, 0], xs_ref, cbuf.at[s], True, sems.at[s])

    e8 = eidx_ref[...]
    base = jnp.zeros_like(e8)
    for e in range(N_EXPERTS):
        base = jnp.where(e8 == e, soff_ref[i, e], base)
    tr = base + lr_ref[...]
    srow_ref[...] = tr

    vb = v_ref[...]
    for c0 in range(0, NSLOT, SLOT_CHUNK):
        rows = lax.broadcasted_iota(I32, (SLOT_CHUNK, tm), 0) + c0
        onehot = jnp.zeros((SLOT_CHUNK, tm), F32)
        for k in range(TOP_K):
            onehot = jnp.where(rows == tr[k:k + 1, :], 1.0, onehot)
        cbuf[s, c0:c0 + SLOT_CHUNK, :] = _dot(onehot.astype(BF16), vb).astype(BF16)
    _start_segment_copies(tc_ref, dst_ref, soff_ref, i, xs_ref, cbuf.at[s], True, sems.at[s])

    @pl.when(i == nt - 1)
    def _():
        @pl.when(i >= 1)
        def _():
            _wait_segment_copies(utot_ref[i - 1, 0], xs_ref, cbuf.at[1 - s], True, sems.at[1 - s])

        _wait_segment_copies(utot_ref[i, 0], xs_ref, cbuf.at[s], True, sems.at[s])
        zbuf[...] = jnp.zeros_like(zbuf)

        def zero_rows(start, size):
            rows = pl.ds(pl.multiple_of(start, SEG_ALIGN), size)
            return pltpu.make_async_copy(zbuf.at[pl.ds(0, size), :], xs_ref.at[rows, :], zsem)

        tail = tail_ref[0]
        n_small = lax.shift_right_logical(jnp.bitwise_and(-tail, ROW_BLOCK - 1),
                                          SEG_ALIGN.bit_length() - 1)
        tail_blk = tail + n_small * SEG_ALIGN
        n_big = lax.shift_right_logical(xs_ref.shape[0] - tail_blk, ROW_BLOCK.bit_length() - 1)

        def each(fn):
            lax.fori_loop(0, n_small, lambda q, c: fn(zero_rows(tail + q * SEG_ALIGN, SEG_ALIGN), c), 0)
            lax.fori_loop(0, n_big, lambda q, c: fn(zero_rows(tail_blk + q * ROW_BLOCK, ROW_BLOCK), c), 0)

        each(lambda cp, c: (cp.start(), c)[1])
        each(lambda cp, c: (cp.wait(), c)[1])


def _dispatch(tc, dst, soff, utot, tail, eidx, lrank, v, n_rows):
    t, d = v.shape
    tm = TOKEN_TILE
    tok = pl.BlockSpec((TOP_K, tm), lambda i, *_: (0, i))
    return pl.pallas_call(
        _dispatch_kernel,
        grid_spec=pltpu.PrefetchScalarGridSpec(
            num_scalar_prefetch=5, grid=(t // tm,),
            in_specs=[tok, tok, pl.BlockSpec((tm, d), lambda i, *_: (i, 0))],
            out_specs=[pl.BlockSpec(memory_space=pl.ANY), tok],
            scratch_shapes=[pltpu.VMEM((2, NSLOT, d), BF16), pltpu.VMEM((ROW_BLOCK, d), BF16),
                            pltpu.SemaphoreType.DMA((2,)), pltpu.SemaphoreType.DMA]),
        out_shape=[jax.ShapeDtypeStruct((n_rows, d), BF16), jax.ShapeDtypeStruct((TOP_K, t), I32)],
        compiler_params=_cparams(("arbitrary",)),
        name="dispatch",
    )(tc, dst, soff, utot, tail, eidx, lrank, v)


def _gmm_kernel(vb_ref, vx_ref, ve_ref, vlo_ref, vhi_ref, x_ref, wg_ref, wu_ref, wd_ref, y_ref,
                wg_b, wu_b, wd_b):
    j = pl.program_id(0)
    prev = jnp.maximum(j - 1, 0)
    first = jnp.logical_or(j == 0, vb_ref[prev] != vb_ref[j])
    live = vhi_ref[j] > vlo_ref[j]

    @pl.when(jnp.logical_or(j == 0, ve_ref[prev] != ve_ref[j]))
    def _():
        wg_b[...] = wg_ref[...].astype(BF16)
        wu_b[...] = wu_ref[...].astype(BF16)
        wd_b[...] = wd_ref[...].astype(BF16)

    @pl.when(live)
    def _():
        x = x_ref[...]
        hidden = _silu(_dot(x, wg_b[...])) * _dot(x, wu_b[...])
        y = _dot(hidden.astype(BF16), wd_b[...])
        rows = vb_ref[j] * ROW_BLOCK + lax.broadcasted_iota(I32, (ROW_BLOCK, 1), 0)
        mine = (rows >= vlo_ref[j]) & (rows < vhi_ref[j])

        @pl.when(first)
        def _():
            y_ref[...] = jnp.where(mine, y, 0.0).astype(y_ref.dtype)

        @pl.when(jnp.logical_not(first))
        def _():
            y_ref[...] = jnp.where(mine, y, y_ref[...].astype(F32)).astype(y_ref.dtype)

    @pl.when(jnp.logical_and(jnp.logical_not(live), first))
    def _():
        y_ref[...] = jnp.zeros_like(y_ref)


def _gmm(vblock, vxblock, vexp, vlo, vhi, xs, wg, wu, wd):
    n, d = xs.shape
    de = wg.shape[2]
    nv = vblock.shape[0]
    return pl.pallas_call(
        _gmm_kernel,
        grid_spec=pltpu.PrefetchScalarGridSpec(
            num_scalar_prefetch=5, grid=(nv,),
            in_specs=[pl.BlockSpec((ROW_BLOCK, d), lambda j, vb, vx, ve, lo, hi: (vx[j], 0)),
                      pl.BlockSpec((None, d, de), lambda j, vb, vx, ve, lo, hi: (ve[j], 0, 0)),
                      pl.BlockSpec((None, d, de), lambda j, vb, vx, ve, lo, hi: (ve[j], 0, 0)),
                      pl.BlockSpec((None, de, d), lambda j, vb, vx, ve, lo, hi: (ve[j], 0, 0))],
            out_specs=pl.BlockSpec((ROW_BLOCK, d), lambda j, vb, vx, ve, lo, hi: (vb[j], 0)),
            scratch_shapes=[pltpu.VMEM((d, de), BF16), pltpu.VMEM((d, de), BF16),
                            pltpu.VMEM((de, d), BF16)]),
        out_shape=jax.ShapeDtypeStruct((n, d), BF16),
        compiler_params=_cparams(("arbitrary",)),
        name="gmm",
    )(vblock, vxblock, vexp, vlo, vhi, xs, wg, wu, wd)


def _combine_kernel(tc_ref, dst_ref, soff_ref, utot_ref, srow_ref, w_ref, v_ref, h1_ref, g2_ref,
                    shg_ref, shu_ref, shd_ref, l2g_ref, l2b_ref, ys_ref, o_ref, ybuf, sems):
    i = pl.program_id(0)
    nt = pl.num_programs(0)
    tm, d = v_ref.shape
    cur = i % 2

    def fetch(tile, b):
        _start_segment_copies(tc_ref, dst_ref, soff_ref, tile, ys_ref, ybuf.at[b], False, sems.at[b])

    @pl.when(i == 0)
    def _():
        ybuf[...] = jnp.zeros_like(ybuf)
        fetch(0, 0)

    @pl.when(i + 1 < nt)
    def _():
        fetch(i + 1, 1 - cur)

    vb = v_ref[...]
    hidden = _silu(_dot(vb, shg_ref[...])) * _dot(vb, shu_ref[...])
    f = _dot(hidden.astype(BF16), shd_ref[...])

    tr = srow_ref[...]
    w8 = w_ref[...]
    _wait_segment_copies(utot_ref[i, 0], ys_ref, ybuf.at[cur], False, sems.at[cur])
    for c0 in range(0, NSLOT, SLOT_CHUNK):
        lanes = lax.broadcasted_iota(I32, (tm, SLOT_CHUNK), 1) + c0
        pw = jnp.zeros((tm, SLOT_CHUNK), F32)
        for k in range(TOP_K):
            pw = jnp.where(lanes == tr[:, k:k + 1], w8[:, k:k + 1], pw)
        f = f + _dot(pw.astype(BF16), ybuf[cur, c0:c0 + SLOT_CHUNK, :])
    o_ref[...] = _ln(ALPHA * h1_ref[...] + g2_ref[...] * f, l2g_ref[...], l2b_ref[...])


def _combine(tc, dst, soff, utot, srow_t, wts_t, v, h1, g2, shg, shu, shd, l2g, l2b, ys, seq):
    t, d = v.shape
    tm = TOKEN_TILE
    dsh = shg.shape[1]
    row = pl.BlockSpec((tm, d), lambda i, *_: (i, 0))
    vec = pl.BlockSpec((1, d), lambda i, *_: (0, 0))
    tok = pl.BlockSpec((tm, TOP_K), lambda i, *_: (i, 0))
    return pl.pallas_call(
        _combine_kernel,
        grid_spec=pltpu.PrefetchScalarGridSpec(
            num_scalar_prefetch=4, grid=(t // tm,),
            in_specs=[tok, tok, row, row,
                      pl.BlockSpec((None, 1, d), lambda i, *_: ((i * tm) // seq, 0, 0)),
                      pl.BlockSpec((d, dsh), lambda i, *_: (0, 0)),
                      pl.BlockSpec((d, dsh), lambda i, *_: (0, 0)),
                      pl.BlockSpec((dsh, d), lambda i, *_: (0, 0)), vec, vec,
                      pl.BlockSpec(memory_space=pl.ANY)],
            out_specs=row,
            scratch_shapes=[pltpu.VMEM((2, NSLOT, d), BF16), pltpu.SemaphoreType.DMA((2,))]),
        out_shape=jax.ShapeDtypeStruct((t, d), F32),
        compiler_params=_cparams(("arbitrary",)),
        name="combine",
    )(tc, dst, soff, utot, srow_t, wts_t, v, h1, g2, shg, shu, shd,
      l2g.reshape(1, d), l2b.reshape(1, d), ys)


def _quad(a0, a1):
    return jnp.concatenate([a0, a1, a0, a1], axis=-1)


def kernel(x, c, ctx, c_ctx, ln_in_g, ln_in_b, ada_w, ada_b, w_in, s5_lam_re, s5_lam_im, s5_log_dt, s5_b_re, s5_b_im, s5_c_re, s5_c_im, s5_d, s5_w_glu, s5_b_glu, lru_conv_w, lru_conv_b, lru_w_a, lru_b_a, lru_w_x, lru_b_x, lru_lam, w_out, ln1_g, ln1_b, router_w, router_bias, exp_w_gate, exp_w_up, exp_w_down, sh_w_gate, sh_w_up, sh_w_down, ln2_g, ln2_b):
    bsz, seq, dm = x.shape
    cl = ctx.shape[1]
    assert ada_w.shape[0] == DEPTH
    d_s5 = s5_w_glu.shape[1]
    d_lru = lru_lam.shape[2]
    ngrp = d_s5 // S5_GROUP
    rows = seq // GRID_W
    t = bsz * seq
    lc = S5_CHUNK
    nkl, nkc = seq // lc, cl // lc

    pad = (-(bsz + 1)) % 8
    cc = jnp.concatenate([c, c_ctx[None, :], jnp.zeros((pad, dm), F32)], axis=0)
    mods = _ada(cc, ada_w[0], ada_b[0])
    sh1, sc1, g1, sh2, sc2, g2 = [mods[:bsz, k * dm:(k + 1) * dm].reshape(bsz, 1, dm) for k in range(6)]
    csh1, csc1 = [mods[bsz:bsz + 1, k * dm:(k + 1) * dm].reshape(1, 1, dm) for k in range(2)]

    x2d = x.reshape(t, dm)
    w_in_b = w_in[0].astype(BF16)
    us5, ulru = _in_proj(x2d, ln_in_g, ln_in_b, sc1, sh1, w_in_b, seq, d_s5)
    ucs5, uclru = _in_proj(ctx.reshape(bsz * cl, dm), ln_in_g, ln_in_b, csc1, csh1, w_in_b, bsz * cl, d_s5)

    def to_chunks(u, nk):
        u = u.reshape(bsz, nk, lc, ngrp, S5_GROUP).transpose(3, 1, 0, 2, 4)
        return u.reshape(ngrp, nk * bsz, lc * S5_GROUP)

    lam3 = jnp.stack([_quad(s5_lam_re[0, 0], s5_lam_re[0, 1]), _quad(s5_lam_im[0, 0], s5_lam_im[0, 1]),
                      _quad(*[jnp.broadcast_to(s5_log_dt[0, k][:, None], (ngrp, S5_STATE)) for k in range(2)])],
                     axis=1)
    bt_re = jnp.swapaxes(s5_b_re[0], -1, -2)
    bt_im = jnp.swapaxes(s5_b_im[0], -1, -2)
    tt, win, wout, a16 = _s5_prep(lam3, _quad(bt_re[0], bt_re[1]), _quad(bt_im[0], bt_im[1]),
                                  _quad(s5_c_re[0, 0], s5_c_re[0, 1]), _quad(s5_c_im[0, 0], s5_c_im[0, 1]),
                                  s5_d[0].reshape(ngrp, S5_GROUP, 1))
    ys = _s5(to_chunks(us5, nkl), to_chunks(ucs5, nkc), tt, win, wout, a16, bsz)
    ys5 = ys.reshape(ngrp, nkl, bsz, lc, S5_GROUP).transpose(2, 1, 3, 0, 4).reshape(t, d_s5)

    c_blk = 256
    hd = d_lru // LRU_HEADS
    hpb = c_blk // hd
    ncb = d_lru // c_blk

    def blockdiag(wh):
        wh = wh.reshape(ncb, hpb, hd, hd)
        eye = jnp.eye(hpb, dtype=wh.dtype)
        return jnp.einsum("nhij,hk->nhikj", wh, eye).reshape(ncb, c_blk, c_blk)

    wg = jnp.concatenate([blockdiag(lru_w_a[0, 0]), blockdiag(lru_w_x[0, 0]),
                          blockdiag(lru_w_a[0, 1]), blockdiag(lru_w_x[0, 1])], axis=-1).astype(BF16)
    bgate = jnp.concatenate([lru_b_a[0, 0].reshape(ncb, 1, c_blk), lru_b_x[0, 0].reshape(ncb, 1, c_blk),
                             lru_b_a[0, 1].reshape(ncb, 1, c_blk), lru_b_x[0, 1].reshape(ncb, 1, c_blk)], axis=-1)
    ylru = _lru(ulru.reshape(bsz, rows, GRID_W, 2 * d_lru), uclru.reshape(bsz, cl, 2 * d_lru),
                lru_conv_w[0], lru_conv_b[0].reshape(1, d_lru), wg, bgate, lru_lam[0], c_blk)
    ylru = ylru.reshape(t, d_lru)

    w_out_b = w_out[0].astype(BF16)
    h1, v, eidx, lrank, wts, tcnt = _mix(
        ys5, ylru, x2d, ln_in_g, ln_in_b, g1, sc2, sh2, s5_w_glu[0].astype(BF16), s5_b_glu[0],
        w_out_b[:d_s5], w_out_b[d_s5:], ln1_g[0], ln1_b[0], router_w[0].T,
        router_bias[0].reshape(N_EXPERTS, 1), seq)

    n_rows = _sorted_rows(t)
    tc = tcnt.transpose(0, 2, 1).reshape(t // TOKEN_TILE, N_EXPERTS)
    dst, soff, utot, start, total = _plan(tc)
    tail, vblock, vxblock, vexp, vlo, vhi = _visits(start.reshape(-1), total.reshape(-1), n_rows)
    xs, srow = _dispatch(tc, dst, soff, utot, tail, eidx, lrank, v, n_rows)
    ysort = _gmm(vblock, vxblock, vexp, vlo, vhi, xs, exp_w_gate[0], exp_w_up[0], exp_w_down[0])
    out = _combine(tc, dst, soff, utot, srow.T, wts.T, v, h1, g2, sh_w_gate[0].astype(BF16),
                   sh_w_up[0].astype(BF16), sh_w_down[0].astype(BF16), ln2_g[0], ln2_b[0], ysort, seq)
    return out.reshape(bsz, seq, dm).astype(x.dtype)
```

```python
import functools
import math

import jax
import jax.numpy as jnp
from jax import lax
from jax.experimental import pallas as pl
from jax.experimental.pallas import tpu as pltpu

F32 = jnp.float32
BF16 = jnp.bfloat16
I32 = jnp.int32
HIGHEST = lax.Precision.HIGHEST

GRID_W = 64
S5_GROUP = 16
S5_STATE = 64
S5_CHUNK = 16
LRU_HEADS = 8
LRU_C = 8.0
CONV_W = 4
CONV_LEFT = 2
N_EXPERTS = 64
TOP_K = 8
N_GROUPS = 8
TOPK_GROUPS = 4
ROUTED_SCALE = 2.5
LN_EPS = 1e-5
DEPTH = 1
ALPHA = (2.0 * DEPTH) ** 0.25

ROW_BLOCK = 512
GMM_SPLIT = 2
TOKEN_TILE = 256
SEG_ALIGN = 16
NSLOT = TOKEN_TILE * TOP_K + N_EXPERTS * SEG_ALIGN
SLOT_CHUNK = NSLOT // 4
VMEM_LIMIT = 52 * 1024 * 1024


def _cparams(sem):
    return pltpu.CompilerParams(dimension_semantics=sem, vmem_limit_bytes=VMEM_LIMIT)


def _ln(x, g, b):
    mu = jnp.mean(x, axis=-1, keepdims=True)
    xc = x - mu
    var = jnp.mean(xc * xc, axis=-1, keepdims=True)
    return xc * lax.rsqrt(var + LN_EPS) * g + b


def _gelu(x):
    return x * (0.5 * (1.0 + jnp.tanh(math.sqrt(2.0 / math.pi) * (x + 0.044715 * (x * x * x)))))


def _silu(x):
    return x * jax.nn.sigmoid(x)


def _softplus(x):
    return jnp.maximum(x, 0.0) + jnp.log1p(jnp.exp(-jnp.abs(x)))


def _dot(a, b):
    return jnp.dot(a, b, preferred_element_type=F32)


def _dot_nt(a, b, precision=None):
    return lax.dot_general(a, b, (((1,), (1,)), ((), ())), precision=precision,
                           preferred_element_type=F32)


def _ada_kernel(c_ref, w_ref, b_ref, o_ref):
    s = _silu(c_ref[...])
    o_ref[...] = jnp.dot(s, w_ref[...], precision=HIGHEST, preferred_element_type=F32) + b_ref[...]


def _ada(cc, ada_w, ada_b):
    r, d = cc.shape
    n = ada_w.shape[1]
    tn = 512
    return pl.pallas_call(
        _ada_kernel,
        grid=(n // tn,),
        in_specs=[pl.BlockSpec((r, d), lambda j: (0, 0)),
                  pl.BlockSpec((d, tn), lambda j: (0, j)),
                  pl.BlockSpec((1, tn), lambda j: (0, j))],
        out_specs=pl.BlockSpec((r, tn), lambda j: (0, j)),
        out_shape=jax.ShapeDtypeStruct((r, n), F32),
        compiler_params=_cparams(("parallel",)),
        name="ada",
    )(cc, ada_w, ada_b.reshape(1, n))


def _in_proj_kernel(x_ref, g_ref, b_ref, sc_ref, sh_ref, w_ref, us5_ref, ulru_ref):
    h = _ln(x_ref[...], g_ref[...], b_ref[...])
    m = h * (1.0 + sc_ref[...]) + sh_ref[...]
    u = _dot(m.astype(BF16), w_ref[...])
    ds5 = us5_ref.shape[-1]
    us5_ref[...] = u[:, :ds5].astype(BF16)
    ulru_ref[...] = u[:, ds5:].astype(BF16)


def _in_proj(x2d, ln_g, ln_b, sc, sh, w_bf16, rows_per_mod, d_s5, tm=512):
    t, d = x2d.shape
    n = w_bf16.shape[1]
    tm = min(tm, t)
    mod_spec = pl.BlockSpec((None, 1, d), lambda i: ((i * tm) // rows_per_mod, 0, 0))
    vec = pl.BlockSpec((1, d), lambda i: (0, 0))
    return pl.pallas_call(
        _in_proj_kernel,
        grid=(t // tm,),
        in_specs=[pl.BlockSpec((tm, d), lambda i: (i, 0)), vec, vec, mod_spec, mod_spec,
                  pl.BlockSpec((d, n), lambda i: (0, 0))],
        out_specs=[pl.BlockSpec((tm, d_s5), lambda i: (i, 0)),
                   pl.BlockSpec((tm, n - d_s5), lambda i: (i, 0))],
        out_shape=[jax.ShapeDtypeStruct((t, d_s5), BF16),
                   jax.ShapeDtypeStruct((t, n - d_s5), BF16)],
        compiler_params=_cparams(("parallel",)),
        name="in_proj",
    )(x2d, ln_g.reshape(1, d), ln_b.reshape(1, d), sc, sh, w_bf16)


def _s5_prep_kernel(lam_ref, bre_ref, bim_ref, cre_ref, cim_ref, d_ref,
                    tt_ref, win_ref, wout_ref, a16_ref, cp_scr):
    p, lc, gs = S5_STATE, S5_CHUNK, S5_GROUP
    nl = 4 * p
    lre = lam_ref[0:1, :]
    lim = lam_ref[1:2, :]
    dt = jnp.exp(lam_ref[2:3, :])
    lane = lax.broadcasted_iota(I32, (1, nl), 1)
    is_re = lane < 2 * p
    is_f = (lane % (2 * p)) < p

    def powers(ef, eb):
        e = jnp.where(is_f, float(ef), float(eb))
        mag = jnp.exp(lre * dt * e)
        ang = lim * dt * e
        return mag * jnp.cos(ang), mag * jnp.sin(ang)

    bre, bim = bre_ref[...], bim_ref[...]
    cre, cim = cre_ref[...], cim_ref[...]
    ar, ai = powers(1, 1)
    den = lre * lre + lim * lim
    nr = ar - 1.0
    fr = (nr * lre + ai * lim) / den
    fi = (ai * lre - nr * lim) / den
    bbre = fr * bre - fi * bim
    bbim = fr * bim + fi * bre

    def c_times(pc, ps):
        return cre * jnp.where(is_re, pc, -ps) + cim * jnp.where(is_re, -ps, -pc)

    for s in range(lc):
        pc, ps = powers(lc - 1 - s, s)
        qa = jnp.where(is_re, pc, ps)
        qb = jnp.where(is_re, -ps, pc)
        win_ref[gs * s:gs * (s + 1), :] = (bbre * qa + bbim * qb).astype(BF16)
    for t in range(lc):
        pc, ps = powers(t + 1, lc - t)
        wout_ref[gs * t:gs * (t + 1), :] = c_times(pc, ps).astype(BF16)
    for j in range(lc):
        pc, ps = powers(j, lc - 1 - j)
        cp_scr[gs * j:gs * (j + 1), :] = c_times(pc, ps)

    bbcat = jnp.where(is_re, bbre, bbim)
    cp = cp_scr[...]
    ktf = _dot_nt(jnp.where(is_f, bbcat, 0.0), cp, HIGHEST)
    ktb = _dot_nt(jnp.where(is_f, 0.0, bbcat), cp, HIGHEST)
    lane2 = lax.broadcasted_iota(I32, (gs, gs * lc), 1)
    row2 = lax.broadcasted_iota(I32, (gs, gs * lc), 0)
    dcol = d_ref[...]
    width = gs * lc
    for s in range(lc):
        sf = gs * s
        tf = ktf if sf == 0 else pltpu.roll(ktf, sf, 1)
        tf = jnp.where(lane2 >= sf, tf, 0.0)
        sb = (gs * (s + 1)) % width
        tb = ktb if sb == 0 else pltpu.roll(ktb, sb, 1)
        tb = jnp.where(lane2 < gs * (s + 1), tb, 0.0)
        skip = jnp.where(lane2 == sf + row2, dcol, 0.0)
        tt_ref[gs * s:gs * (s + 1), :] = (tf + tb + skip).astype(BF16)
    pc, ps = powers(lc, lc)
    a16_ref[0:1, :] = pc[:, :2 * p]
    a16_ref[1:2, :] = ps[:, :2 * p]


def _s5_prep(lam3, b4re, b4im, c4re, c4im, dcol):
    g = lam3.shape[0]
    nl = 4 * S5_STATE
    k = S5_GROUP * S5_CHUNK
    m3 = lambda i: (i, 0, 0)
    return pl.pallas_call(
        _s5_prep_kernel,
        grid=(g,),
        in_specs=[pl.BlockSpec((None, 3, nl), m3)]
        + [pl.BlockSpec((None, S5_GROUP, nl), m3)] * 4
        + [pl.BlockSpec((None, S5_GROUP, 1), m3)],
        out_specs=[pl.BlockSpec((None, k, k), m3), pl.BlockSpec((None, k, nl), m3),
                   pl.BlockSpec((None, k, nl), m3), pl.BlockSpec((None, 2, nl // 2), m3)],
        out_shape=[jax.ShapeDtypeStruct((g, k, k), BF16), jax.ShapeDtypeStruct((g, k, nl), BF16),
                   jax.ShapeDtypeStruct((g, k, nl), BF16), jax.ShapeDtypeStruct((g, 2, nl // 2), F32)],
        scratch_shapes=[pltpu.VMEM((k, nl), F32)],
        compiler_params=_cparams(("parallel",)),
        name="s5_prep",
    )(lam3, b4re, b4im, c4re, c4im, dcol)


def _s5_kernel(xl_ref, xc_ref, tt_ref, win_ref, wout_ref, a16_ref, y_ref,
               z_scr, zc_scr, sf_scr, sb_scr, *, nkl, nkc, bsz):
    h = 2 * S5_STATE
    xl = xl_ref[...]
    win = win_ref[...]
    z_scr[...] = _dot(xl, win)
    zc_scr[...] = _dot(xc_ref[...], win)
    is_f = lax.broadcasted_iota(I32, (bsz, h), 1) < S5_STATE
    ar = a16_ref[0:1, :]
    ai = a16_ref[1:2, :]

    def pick(z_ref, i, n):
        zf = z_ref[pl.ds(i * bsz, bsz), :]
        zb = z_ref[pl.ds((n - 1 - i) * bsz, bsz), :]
        return jnp.where(is_f, zf[:, :h], zb[:, :h]), jnp.where(is_f, zf[:, h:], zb[:, h:])

    def update(sr, si, zr, zi):
        return ar * sr - ai * si + zr, ar * si + ai * sr + zi

    def ctx_step(i, carry):
        zr, zi = pick(zc_scr, i, nkc)
        return update(*carry, zr, zi)

    def lat_step(i, carry):
        sr, si = carry
        st = jnp.concatenate([sr, si], axis=1)
        sf_scr[pl.ds(i * bsz, bsz), :] = st
        sb_scr[pl.ds((nkl - 1 - i) * bsz, bsz), :] = st
        zr, zi = pick(z_scr, i, nkl)
        return update(sr, si, zr, zi)

    zero = jnp.zeros((bsz, h), F32)
    carry = lax.fori_loop(0, nkc, ctx_step, (zero, zero))
    lax.fori_loop(0, nkl, lat_step, carry)

    is_f4 = (lax.broadcasted_iota(I32, (1, 2 * h), 1) % h) < S5_STATE
    xs = jnp.where(is_f4, sf_scr[...], sb_scr[...]).astype(BF16)
    y = _dot(xl, tt_ref[...]) + _dot_nt(xs, wout_ref[...])
    y_ref[...] = y.astype(BF16)


def _s5(xl, xc, tt, win, wout, a16, bsz):
    g, nl_rows, k = xl.shape
    nc_rows = xc.shape[1]
    nl = 4 * S5_STATE
    m3 = lambda i: (i, 0, 0)
    kern = functools.partial(_s5_kernel, nkl=nl_rows // bsz, nkc=nc_rows // bsz, bsz=bsz)
    return pl.pallas_call(
        kern,
        grid=(g,),
        in_specs=[pl.BlockSpec((None, nl_rows, k), m3), pl.BlockSpec((None, nc_rows, k), m3),
                  pl.BlockSpec((None, k, k), m3), pl.BlockSpec((None, k, nl), m3),
                  pl.BlockSpec((None, k, nl), m3), pl.BlockSpec((None, 2, nl // 2), m3)],
        out_specs=pl.BlockSpec((None, nl_rows, k), m3),
        out_shape=jax.ShapeDtypeStruct((g, nl_rows, k), BF16),
        scratch_shapes=[pltpu.VMEM((nl_rows, nl), F32), pltpu.VMEM((nc_rows, nl), F32),
                        pltpu.VMEM((nl_rows, nl), F32), pltpu.VMEM((nl_rows, nl), F32)],
        compiler_params=_cparams(("parallel",)),
        name="s5",
    )(xl, xc, tt, win, wout, a16)


def _lru_kernel(xv_ref, xg_ref, xc_ref, cw_ref, cb_ref, wg_ref, bg_ref, lam_ref, o_ref,
                a_scr, b_scr, ac_scr, bc_scr, cin_scr, *, rows, cl, slab_block):
    w = GRID_W
    c = xv_ref.shape[-1]
    cw = cw_ref[...]
    cb = cb_ref[...]
    sp = _softplus(-lam_ref[...])
    bg = bg_ref[...]
    wg = wg_ref[...]

    def coeffs(xs, store):
        g = _dot(xs.astype(BF16), wg) + bg
        for d in range(2):
            r = jax.nn.sigmoid(g[:, (2 * d) * c:(2 * d + 1) * c])
            i = jax.nn.sigmoid(g[:, (2 * d + 1) * c:(2 * d + 2) * c])
            log_a = -LRU_C * r * sp[d:d + 1, :]
            a = jnp.exp(log_a)
            b = jnp.sqrt(-jnp.tanh(log_a) * (a * a + 1.0)) * (i * xs)
            store(d, a, b)

    def tap(r0, r1, off):
        lo, hi = r0 + off, r1 + off
        clo, chi = max(lo, 0), min(hi, rows)
        parts = []
        if clo > lo:
            parts.append(jnp.zeros((clo - lo, w, c), F32))
        if chi > clo:
            parts.append(xv_ref[clo:chi].astype(F32))
        if hi > chi:
            parts.append(jnp.zeros((hi - chi, w, c), F32))
        return parts[0] if len(parts) == 1 else jnp.concatenate(parts, axis=0)

    for r0 in range(0, rows, slab_block):
        r1 = r0 + slab_block
        xs3 = cb
        for k in range(CONV_W):
            xs3 = xs3 + cw[k:k + 1, :] * tap(r0, r1, k - CONV_LEFT)
        xs = xs3.reshape(slab_block * w, c)

        def store_lat(d, a, b, r0=r0, r1=r1):
            a_scr[d, r0 * w:r1 * w, :] = a
            b_scr[d, r0 * w:r1 * w, :] = b

        coeffs(xs, store_lat)

    xc = xc_ref[...].astype(F32)
    trow = lax.broadcasted_iota(I32, (cl, c), 0)
    xcs = cb
    for k in range(CONV_W):
        off = k - CONV_LEFT
        sh = xc if off == 0 else pltpu.roll(xc, (-off) % cl, 0)
        ok = (trow + off >= 0) & (trow + off < cl)
        xcs = xcs + cw[k:k + 1, :] * jnp.where(ok, sh, 0.0)

    def store_ctx(d, a, b):
        ac_scr[d] = a
        bc_scr[d] = b

    coeffs(xcs, store_ctx)

    for d in range(2):
        rev = d == 1

        def ctx_step(j, h, d=d, rev=rev):
            idx = (cl - 1 - j) if rev else j
            return ac_scr[d, pl.ds(idx, 1), :] * h + bc_scr[d, pl.ds(idx, 1), :]

        h0 = lax.fori_loop(0, cl, ctx_step, jnp.zeros((1, c), F32))

        def col_step(j, carry, d=d, rev=rev):
            h, p = carry
            r = (rows - 1 - j) if rev else j
            off = pl.multiple_of(r * w, w)
            a = a_scr[d, pl.ds(off, w), :]
            h = a * h + b_scr[d, pl.ds(off, w), :]
            p = a * p
            b_scr[d, pl.ds(off, w), :] = h
            a_scr[d, pl.ds(off, w), :] = p
            return h, p

        lax.fori_loop(0, rows, col_step, (jnp.zeros((w, c), F32), jnp.ones((w, c), F32)))

        last = 0 if rev else (rows - 1) * w

        def carry_step(j, cin, d=d, rev=rev, last=last):
            col = (w - 1 - j) if rev else j
            cin_scr[pl.ds(col, 1), :] = cin
            return a_scr[d, pl.ds(last + col, 1), :] * cin + b_scr[d, pl.ds(last + col, 1), :]

        lax.fori_loop(0, w, carry_step, h0)
        cin = cin_scr[...]

        if not rev:
            def fix_step(r, _, d=d, cin=cin):
                off = pl.multiple_of(r * w, w)
                b_scr[d, pl.ds(off, w), :] = b_scr[d, pl.ds(off, w), :] + a_scr[d, pl.ds(off, w), :] * cin
                return 0

            lax.fori_loop(0, rows, fix_step, 0)
        else:
            def out_step(r, _, cin=cin):
                off = pl.multiple_of(r * w, w)
                hsum = (b_scr[0, pl.ds(off, w), :] + b_scr[1, pl.ds(off, w), :]
                        + a_scr[1, pl.ds(off, w), :] * cin)
                o_ref[r] = (hsum * _gelu(xg_ref[r].astype(F32))).astype(o_ref.dtype)
                return 0

            lax.fori_loop(0, rows, out_step, 0)


def _lru(u4, uc3, conv_w, conv_b, wg, bg, lam, c_blk=256):
    bsz, rows, w, n2 = u4.shape
    d_lru = n2 // 2
    cl = uc3.shape[1]
    ncb = d_lru // c_blk
    kern = functools.partial(_lru_kernel, rows=rows, cl=cl, slab_block=8)
    n = rows * w
    return pl.pallas_call(
        kern,
        grid=(bsz, ncb),
        in_specs=[pl.BlockSpec((None, rows, w, c_blk), lambda b, j: (b, 0, 0, j)),
                  pl.BlockSpec((None, rows, w, c_blk), lambda b, j: (b, 0, 0, ncb + j)),
                  pl.BlockSpec((None, cl, c_blk), lambda b, j: (b, 0, j)),
                  pl.BlockSpec((CONV_W, c_blk), lambda b, j: (0, j)),
                  pl.BlockSpec((1, c_blk), lambda b, j: (0, j)),
                  pl.BlockSpec((None, c_blk, 4 * c_blk), lambda b, j: (j, 0, 0)),
                  pl.BlockSpec((None, 1, 4 * c_blk), lambda b, j: (j, 0, 0)),
                  pl.BlockSpec((2, c_blk), lambda b, j: (0, j))],
        out_specs=pl.BlockSpec((None, rows, w, c_blk), lambda b, j: (b, 0, 0, j)),
        out_shape=jax.ShapeDtypeStruct((bsz, rows, w, d_lru), BF16),
        scratch_shapes=[pltpu.VMEM((2, n, c_blk), F32), pltpu.VMEM((2, n, c_blk), F32),
                        pltpu.VMEM((2, cl, c_blk), F32), pltpu.VMEM((2, cl, c_blk), F32),
                        pltpu.VMEM((w, c_blk), F32)],
        compiler_params=_cparams(("parallel", "parallel")),
        name="lru",
    )(u4, u4, uc3, conv_w, conv_b, wg, bg, lam)


def _mix_kernel(ys5_ref, ylru_ref, x_ref, lng_ref, lnb_ref, g1_ref, sc2_ref, sh2_ref,
                wglu_ref, bglu_ref, wo1_ref, wo2_ref, l1g_ref, l1b_ref, rwt_ref, rb_ref, tri_ref,
                h1_ref, v_ref, eidx_ref, rank_ref, wts_ref, tcnt_ref):
    y = _gelu(ys5_ref[...].astype(F32))
    s5o = y * jax.nn.sigmoid(_dot(y.astype(BF16), wglu_ref[...]) + bglu_ref[...])
    y1 = _dot(s5o.astype(BF16), wo1_ref[...]) + _dot(ylru_ref[...], wo2_ref[...])
    h = _ln(x_ref[...], lng_ref[...], lnb_ref[...])
    h1 = _ln(ALPHA * h + g1_ref[...] * y1, l1g_ref[...], l1b_ref[...])
    h1_ref[...] = h1
    v = h1 * (1.0 + sc2_ref[...]) + sh2_ref[...]
    v_ref[...] = v.astype(v_ref.dtype)

    tm = v.shape[0]
    ne, ng, gsz = N_EXPERTS, N_GROUPS, N_EXPERTS // N_GROUPS
    scores = jax.nn.sigmoid(_dot_nt(rwt_ref[...], v, HIGHEST))
    s3 = scores.reshape(ng, gsz, tm)
    sel3 = (scores + rb_ref[...]).reshape(ng, gsz, tm)
    ii = lax.broadcasted_iota(I32, (ng, gsz, tm), 1)
    gi = lax.broadcasted_iota(I32, (ng, gsz, tm), 0)
    neg = -jnp.inf

    m1 = jnp.max(sel3, axis=1, keepdims=True)
    f1 = jnp.min(jnp.where(sel3 == m1, ii, gsz), axis=1, keepdims=True)
    m2 = jnp.max(jnp.where(ii == f1, neg, sel3), axis=1, keepdims=True)
    cur = m1 + m2
    gidx = lax.broadcasted_iota(I32, (ng, 1, tm), 0)
    gmask = jnp.zeros((ng, 1, tm), jnp.bool_)
    for _ in range(TOPK_GROUPS):
        mx = jnp.max(cur, axis=0, keepdims=True)
        fg = jnp.min(jnp.where(cur == mx, gidx, ng), axis=0, keepdims=True)
        hit = gidx == fg
        gmask = gmask | hit
        cur = jnp.where(hit, neg, cur)

    selm = jnp.where(gmask, sel3, neg)
    eid = gi * gsz + ii
    picks, erows, wrows = [], [], []
    for _ in range(TOP_K):
        mx = jnp.max(jnp.max(selm, axis=1, keepdims=True), axis=0, keepdims=True)
        fe = jnp.min(jnp.min(jnp.where(selm == mx, eid, ne), axis=1, keepdims=True),
                     axis=0, keepdims=True)
        hit = eid == fe
        selm = jnp.where(hit, neg, selm)
        picks.append(hit)
        erows.append(fe.reshape(1, tm))
        wsel = jnp.where(hit, s3, 0.0)
        wrows.append(jnp.sum(jnp.sum(wsel, axis=1, keepdims=True), axis=0).reshape(1, tm))
    denom = wrows[0]
    for k in range(1, TOP_K):
        denom = denom + wrows[k]

    chosen = picks[0]
    for k in range(1, TOP_K):
        chosen = chosen | picks[k]
    chosen_f = jnp.where(chosen, 1.0, 0.0).reshape(ne, tm)
    cnt3 = _dot(chosen_f.astype(BF16), tri_ref[...]).reshape(ng, gsz, tm)
    rrows = []
    for k in range(TOP_K):
        rsel = jnp.where(picks[k], cnt3, 0.0)
        rrows.append(jnp.sum(jnp.sum(rsel, axis=1, keepdims=True), axis=0).reshape(1, tm))
    tcnt_ref[...] = jnp.concatenate(
        [jnp.sum(chosen_f[:, j * TOKEN_TILE:(j + 1) * TOKEN_TILE], axis=1, keepdims=True)
         for j in range(tm // TOKEN_TILE)], axis=1).astype(I32)

    eidx_ref[...] = jnp.concatenate(erows, axis=0)
    rank_ref[...] = jnp.concatenate(rrows, axis=0).astype(I32)
    wts_ref[...] = jnp.concatenate([wr / denom * ROUTED_SCALE for wr in wrows], axis=0)


def _mix(ys5, ylru, x2d, ln_g, ln_b, g1, sc2, sh2, wglu, bglu, wo1, wo2, l1g, l1b, rwt, rb, seq, tm=512):
    t, d = x2d.shape
    ds = ys5.shape[1]
    ne = rwt.shape[0]
    nsub = tm // TOKEN_TILE
    r_i = lax.broadcasted_iota(I32, (tm, tm), 0)
    c_i = lax.broadcasted_iota(I32, (tm, tm), 1)
    tri = ((r_i < c_i) & (r_i // TOKEN_TILE == c_i // TOKEN_TILE)).astype(BF16)
    row = lambda n: pl.BlockSpec((tm, n), lambda i: (i, 0))
    vec = lambda n: pl.BlockSpec((1, n), lambda i: (0, 0))
    mod = pl.BlockSpec((None, 1, d), lambda i: ((i * tm) // seq, 0, 0))
    full = lambda a, b: pl.BlockSpec((a, b), lambda i: (0, 0))
    tok = pl.BlockSpec((TOP_K, tm), lambda i: (0, i))
    return pl.pallas_call(
        _mix_kernel,
        grid=(t // tm,),
        in_specs=[row(ds), row(ds), row(d), vec(d), vec(d), mod, mod, mod,
                  full(ds, ds), vec(ds), full(ds, d), full(ds, d), vec(d), vec(d),
                  full(ne, d), full(ne, 1), full(tm, tm)],
        out_specs=[row(d), row(d), tok, tok, tok,
                   pl.BlockSpec((None, ne, nsub), lambda i: (i, 0, 0))],
        out_shape=[jax.ShapeDtypeStruct((t, d), F32), jax.ShapeDtypeStruct((t, d), BF16),
                   jax.ShapeDtypeStruct((TOP_K, t), I32), jax.ShapeDtypeStruct((TOP_K, t), I32),
                   jax.ShapeDtypeStruct((TOP_K, t), F32),
                   jax.ShapeDtypeStruct((t // tm, ne, nsub), I32)],
        compiler_params=_cparams(("parallel",)),
        name="mix",
    )(ys5, ylru, x2d, ln_g.reshape(1, d), ln_b.reshape(1, d), g1, sc2, sh2,
      wglu, bglu.reshape(1, ds), wo1, wo2, l1g.reshape(1, d), l1b.reshape(1, d), rwt, rb, tri)


def _plan_kernel(tc_ref, dst_ref, soff_ref, utot_ref, start_ref, total_ref):
    tc = tc_ref[...]
    nt, ne = tc.shape
    seg = jnp.bitwise_and(tc + (SEG_ALIGN - 1), -SEG_ALIGN).astype(F32)
    earlier_tile = (lax.broadcasted_iota(I32, (nt, nt), 0) > lax.broadcasted_iota(I32, (nt, nt), 1))
    earlier_exp = (lax.broadcasted_iota(I32, (ne, ne), 0) < lax.broadcasted_iota(I32, (ne, ne), 1))
    lt = jnp.where(earlier_tile, 1.0, 0.0)
    ut = jnp.where(earlier_exp, 1.0, 0.0)
    before = jnp.dot(lt, seg, precision=HIGHEST, preferred_element_type=F32)
    total = jnp.broadcast_to(jnp.sum(seg, axis=0, keepdims=True), (8, ne))
    start = jnp.dot(total, ut, precision=HIGHEST, preferred_element_type=F32)
    dst_ref[...] = (before + start[0:1, :]).astype(I32)
    soff_ref[...] = jnp.dot(seg, ut, precision=HIGHEST, preferred_element_type=F32).astype(I32)
    utot_ref[...] = (jnp.sum(seg, axis=1, keepdims=True) * (1.0 / SEG_ALIGN)).astype(I32)
    start_ref[...] = start[0:1, :].astype(I32)
    total_ref[...] = total[0:1, :].astype(I32)


def _plan(tc):
    nt, ne = tc.shape
    full = lambda a, b: pl.BlockSpec((a, b), lambda: (0, 0))
    return pl.pallas_call(
        _plan_kernel,
        in_specs=[full(nt, ne)],
        out_specs=[full(nt, ne), full(nt, ne), full(nt, 1), full(1, ne), full(1, ne)],
        out_shape=[jax.ShapeDtypeStruct((nt, ne), I32), jax.ShapeDtypeStruct((nt, ne), I32),
                   jax.ShapeDtypeStruct((nt, 1), I32), jax.ShapeDtypeStruct((1, ne), I32),
                   jax.ShapeDtypeStruct((1, ne), I32)],
        name="plan",
    )(tc)


def _visits_kernel(start_ref, total_ref, tail_ref, blk_ref, xblk_ref, exp_ref, lo_ref, hi_ref,
                   *, n_blocks, nv):
    ne = N_EXPERTS
    shift = ROW_BLOCK.bit_length() - 1

    def put(pos, blk, xblk, e, lo, hi):
        blk_ref[pos] = blk
        xblk_ref[pos] = xblk
        exp_ref[pos] = e
        lo_ref[pos] = lo
        hi_ref[pos] = hi

    def per_expert(e, carry):
        pos, done = carry
        off = start_ref[e]
        cnt = total_ref[e]
        first = lax.shift_right_logical(off, shift)
        last = lax.shift_right_logical(off + cnt - 1, shift)
        nvis = jnp.where(cnt > 0, last - first + 1, 0)

        def put_vis(k, c):
            put(pos + k, first + k, first + k, e, off, off + cnt)
            return c

        lax.fori_loop(0, nvis, put_vis, 0)
        return pos + nvis, jnp.where(cnt > 0, last + 1, done)

    pos, done = lax.fori_loop(0, ne, per_expert, (jnp.int32(0), jnp.int32(0)))
    tail_ref[0] = start_ref[ne - 1] + total_ref[ne - 1]

    def put_tail(k, c):
        put(pos + k, done + k, 0, ne - 1, 0, 0)
        return c

    lax.fori_loop(0, n_blocks - done, put_tail, 0)

    def put_rest(j, c):
        put(j, n_blocks - 1, 0, ne - 1, 0, 0)
        return c

    lax.fori_loop(pos + (n_blocks - done), nv, put_rest, 0)


def _sorted_rows(n_tokens):
    nt = n_tokens // TOKEN_TILE
    rows = n_tokens * TOP_K + nt * N_EXPERTS * (SEG_ALIGN - 1)
    return -(-rows // ROW_BLOCK) * ROW_BLOCK


def _visits(start, total, n_rows):
    ne = start.shape[0]
    n_blocks = n_rows // ROW_BLOCK
    nv = n_blocks + ne
    smem = pl.BlockSpec(memory_space=pltpu.SMEM)
    vec = jax.ShapeDtypeStruct((nv,), I32)
    return pl.pallas_call(
        functools.partial(_visits_kernel, n_blocks=n_blocks, nv=nv),
        in_specs=[smem, smem],
        out_specs=[smem] * 6,
        out_shape=[jax.ShapeDtypeStruct((1,), I32), vec, vec, vec, vec, vec],
        name="visits",
    )(start, total)


def _start_segment_copies(tc_ref, dst_ref, soff_ref, tile, hbm_ref, buf, to_hbm, sem):
    unit_shift = SEG_ALIGN.bit_length() - 1

    def body(e, c):
        units = lax.shift_right_logical(tc_ref[tile, e] + (SEG_ALIGN - 1), unit_shift)
        srow = pl.multiple_of(soff_ref[tile, e], SEG_ALIGN)
        drow = pl.multiple_of(dst_ref[tile, e], SEG_ALIGN)

        def copy(u):
            def start():
                slot_rows = buf.at[pl.ds(srow, u * SEG_ALIGN), :]
                sorted_rows = hbm_ref.at[pl.ds(drow, u * SEG_ALIGN), :]
                src, dst = (slot_rows, sorted_rows) if to_hbm else (sorted_rows, slot_rows)
                pltpu.make_async_copy(src, dst, sem).start()
            return start

        def tree(lo, hi):
            if lo == hi:
                return copy(lo)
            mid = (lo + hi) // 2
            return lambda: lax.cond(units <= mid, tree(lo, mid), tree(mid + 1, hi))

        @pl.when(units > 0)
        def _():
            tree(1, TOKEN_TILE // SEG_ALIGN)()

        return c

    lax.fori_loop(0, N_EXPERTS, body, 0)


def _wait_segment_copies(units, hbm_ref, buf, to_hbm, sem):
    for k in range(NSLOT.bit_length() - SEG_ALIGN.bit_length() + 1):
        @pl.when(jnp.bitwise_and(units, 1 << k) != 0)
        def _(k=k):
            size = SEG_ALIGN << k
            slot_rows = buf.at[pl.ds(0, size), :]
            sorted_rows = hbm_ref.at[pl.ds(0, size), :]
            src, dst = (slot_rows, sorted_rows) if to_hbm else (sorted_rows, slot_rows)
            pltpu.make_async_copy(src, dst, sem).wait()


def _dispatch_kernel(tc_ref, dst_ref, soff_ref, utot_ref, tail_ref, eidx_ref, lr_ref, v_ref,
                     xs_ref, srow_ref, cbuf, zbuf, sems, zsem):
    i = pl.program_id(0)
    nt = pl.num_programs(0)
    tm = v_ref.shape[0]
    s = i % 2

    @pl.when(i >= 2)
    def _():
        _wait_segment_copies(utot_ref[i - 2, 0], xs_ref, cbuf.at[s], True, sems.at[s])

    e8 = eidx_ref[...]
    base = jnp.zeros_like(e8)
    for e in range(N_EXPERTS):
        base = jnp.where(e8 == e, soff_ref[i, e], base)
    tr = base + lr_ref[...]
    srow_ref[...] = tr

    vb = v_ref[...]
    for c0 in range(0, NSLOT, SLOT_CHUNK):
        rows = lax.broadcasted_iota(I32, (SLOT_CHUNK, tm), 0) + c0
        onehot = jnp.zeros((SLOT_CHUNK, tm), F32)
        for k in range(TOP_K):
            onehot = jnp.where(rows == tr[k:k + 1, :], 1.0, onehot)
        cbuf[s, c0:c0 + SLOT_CHUNK, :] = _dot(onehot.astype(BF16), vb).astype(BF16)
    _start_segment_copies(tc_ref, dst_ref, soff_ref, i, xs_ref, cbuf.at[s], True, sems.at[s])

    @pl.when(i == nt - 1)
    def _():
        @pl.when(i >= 1)
        def _():
            _wait_segment_copies(utot_ref[i - 1, 0], xs_ref, cbuf.at[1 - s], True, sems.at[1 - s])

        _wait_segment_copies(utot_ref[i, 0], xs_ref, cbuf.at[s], True, sems.at[s])
        zbuf[...] = jnp.zeros_like(zbuf)

        def zero_rows(start, size):
            rows = pl.ds(pl.multiple_of(start, SEG_ALIGN), size)
            return pltpu.make_async_copy(zbuf.at[pl.ds(0, size), :], xs_ref.at[rows, :], zsem)

        tail = tail_ref[0]
        n_small = lax.shift_right_logical(jnp.bitwise_and(-tail, ROW_BLOCK - 1),
                                          SEG_ALIGN.bit_length() - 1)
        tail_blk = tail + n_small * SEG_ALIGN
        n_big = lax.shift_right_logical(xs_ref.shape[0] - tail_blk, ROW_BLOCK.bit_length() - 1)

        def each(fn):
            lax.fori_loop(0, n_small, lambda q, c: fn(zero_rows(tail + q * SEG_ALIGN, SEG_ALIGN), c), 0)
            lax.fori_loop(0, n_big, lambda q, c: fn(zero_rows(tail_blk + q * ROW_BLOCK, ROW_BLOCK), c), 0)

        each(lambda cp, c: (cp.start(), c)[1])
        each(lambda cp, c: (cp.wait(), c)[1])


def _dispatch(tc, dst, soff, utot, tail, eidx, lrank, v, n_rows):
    t, d = v.shape
    tm = TOKEN_TILE
    tok = pl.BlockSpec((TOP_K, tm), lambda i, *_: (0, i))
    return pl.pallas_call(
        _dispatch_kernel,
        grid_spec=pltpu.PrefetchScalarGridSpec(
            num_scalar_prefetch=5, grid=(t // tm,),
            in_specs=[tok, tok, pl.BlockSpec((tm, d), lambda i, *_: (i, 0))],
            out_specs=[pl.BlockSpec(memory_space=pl.ANY), tok],
            scratch_shapes=[pltpu.VMEM((2, NSLOT, d), BF16), pltpu.VMEM((ROW_BLOCK, d), BF16),
                            pltpu.SemaphoreType.DMA((2,)), pltpu.SemaphoreType.DMA]),
        out_shape=[jax.ShapeDtypeStruct((n_rows, d), BF16), jax.ShapeDtypeStruct((TOP_K, t), I32)],
        compiler_params=_cparams(("arbitrary",)),
        name="dispatch",
    )(tc, dst, soff, utot, tail, eidx, lrank, v)


def _gmm_kernel(vb_ref, vx_ref, ve_ref, vlo_ref, vhi_ref, x_ref, wg_ref, wu_ref, wd_ref, y_ref,
                wg_b, wu_b, wd_b):
    j = pl.program_id(0)
    prev = jnp.maximum(j - 1, 0)
    first = jnp.logical_or(j == 0, vb_ref[prev] != vb_ref[j])
    live = vhi_ref[j] > vlo_ref[j]

    @pl.when(jnp.logical_or(j == 0, ve_ref[prev] != ve_ref[j]))
    def _():
        wg_b[...] = wg_ref[...].astype(BF16)
        wu_b[...] = wu_ref[...].astype(BF16)
        wd_b[...] = wd_ref[...].astype(BF16)

    @pl.when(live)
    def _():
        sub = ROW_BLOCK // GMM_SPLIT
        parts = []
        for r0 in range(0, ROW_BLOCK, sub):
            x = x_ref[r0:r0 + sub, :]
            hidden = _silu(_dot(x, wg_b[...])) * _dot(x, wu_b[...])
            y = _dot(hidden.astype(BF16), wd_b[...])
            rows = vb_ref[j] * ROW_BLOCK + r0 + lax.broadcasted_iota(I32, (sub, 1), 0)
            parts.append((r0, y, (rows >= vlo_ref[j]) & (rows < vhi_ref[j])))

        @pl.when(first)
        def _():
            for r0, y, mine in parts:
                y_ref[r0:r0 + sub, :] = jnp.where(mine, y, 0.0).astype(y_ref.dtype)

        @pl.when(jnp.logical_not(first))
        def _():
            for r0, y, mine in parts:
                keep = y_ref[r0:r0 + sub, :].astype(F32)
                y_ref[r0:r0 + sub, :] = jnp.where(mine, y, keep).astype(y_ref.dtype)

    @pl.when(jnp.logical_and(jnp.logical_not(live), first))
    def _():
        y_ref[...] = jnp.zeros_like(y_ref)


def _gmm(vblock, vxblock, vexp, vlo, vhi, xs, wg, wu, wd):
    n, d = xs.shape
    de = wg.shape[2]
    nv = vblock.shape[0]
    return pl.pallas_call(
        _gmm_kernel,
        grid_spec=pltpu.PrefetchScalarGridSpec(
            num_scalar_prefetch=5, grid=(nv,),
            in_specs=[pl.BlockSpec((ROW_BLOCK, d), lambda j, vb, vx, ve, lo, hi: (vx[j], 0)),
                      pl.BlockSpec((None, d, de), lambda j, vb, vx, ve, lo, hi: (ve[j], 0, 0)),
                      pl.BlockSpec((None, d, de), lambda j, vb, vx, ve, lo, hi: (ve[j], 0, 0)),
                      pl.BlockSpec((None, de, d), lambda j, vb, vx, ve, lo, hi: (ve[j], 0, 0))],
            out_specs=pl.BlockSpec((ROW_BLOCK, d), lambda j, vb, vx, ve, lo, hi: (vb[j], 0)),
            scratch_shapes=[pltpu.VMEM((d, de), BF16), pltpu.VMEM((d, de), BF16),
                            pltpu.VMEM((de, d), BF16)]),
        out_shape=jax.ShapeDtypeStruct((n, d), BF16),
        compiler_params=_cparams(("arbitrary",)),
        name="gmm",
    )(vblock, vxblock, vexp, vlo, vhi, xs, wg, wu, wd)


def _combine_kernel(tc_ref, dst_ref, soff_ref, utot_ref, srow_ref, w_ref, v_ref, h1_ref, g2_ref,
                    shg_ref, shu_ref, shd_ref, l2g_ref, l2b_ref, ys_ref, o_ref, ybuf, sems):
    i = pl.program_id(0)
    nt = pl.num_programs(0)
    tm, d = v_ref.shape
    cur = i % 2

    def fetch(tile, b):
        _start_segment_copies(tc_ref, dst_ref, soff_ref, tile, ys_ref, ybuf.at[b], False, sems.at[b])

    @pl.when(i == 0)
    def _():
        ybuf[...] = jnp.zeros_like(ybuf)
        fetch(0, 0)

    @pl.when(i + 1 < nt)
    def _():
        fetch(i + 1, 1 - cur)

    vb = v_ref[...]
    hidden = _silu(_dot(vb, shg_ref[...])) * _dot(vb, shu_ref[...])
    f = _dot(hidden.astype(BF16), shd_ref[...])

    tr = srow_ref[...]
    w8 = w_ref[...]
    _wait_segment_copies(utot_ref[i, 0], ys_ref, ybuf.at[cur], False, sems.at[cur])
    for c0 in range(0, NSLOT, SLOT_CHUNK):
        lanes = lax.broadcasted_iota(I32, (tm, SLOT_CHUNK), 1) + c0
        pw = jnp.zeros((tm, SLOT_CHUNK), F32)
        for k in range(TOP_K):
            pw = jnp.where(lanes == tr[:, k:k + 1], w8[:, k:k + 1], pw)
        f = f + _dot(pw.astype(BF16), ybuf[cur, c0:c0 + SLOT_CHUNK, :])
    o_ref[...] = _ln(ALPHA * h1_ref[...] + g2_ref[...] * f, l2g_ref[...], l2b_ref[...])


def _combine(tc, dst, soff, utot, srow_t, wts_t, v, h1, g2, shg, shu, shd, l2g, l2b, ys, seq):
    t, d = v.shape
    tm = TOKEN_TILE
    dsh = shg.shape[1]
    row = pl.BlockSpec((tm, d), lambda i, *_: (i, 0))
    vec = pl.BlockSpec((1, d), lambda i, *_: (0, 0))
    tok = pl.BlockSpec((tm, TOP_K), lambda i, *_: (i, 0))
    return pl.pallas_call(
        _combine_kernel,
        grid_spec=pltpu.PrefetchScalarGridSpec(
            num_scalar_prefetch=4, grid=(t // tm,),
            in_specs=[tok, tok, row, row,
                      pl.BlockSpec((None, 1, d), lambda i, *_: ((i * tm) // seq, 0, 0)),
                      pl.BlockSpec((d, dsh), lambda i, *_: (0, 0)),
                      pl.BlockSpec((d, dsh), lambda i, *_: (0, 0)),
                      pl.BlockSpec((dsh, d), lambda i, *_: (0, 0)), vec, vec,
                      pl.BlockSpec(memory_space=pl.ANY)],
            out_specs=row,
            scratch_shapes=[pltpu.VMEM((2, NSLOT, d), BF16), pltpu.SemaphoreType.DMA((2,))]),
        out_shape=jax.ShapeDtypeStruct((t, d), F32),
        compiler_params=_cparams(("arbitrary",)),
        name="combine",
    )(tc, dst, soff, utot, srow_t, wts_t, v, h1, g2, shg, shu, shd,
      l2g.reshape(1, d), l2b.reshape(1, d), ys)


def _quad(a0, a1):
    return jnp.concatenate([a0, a1, a0, a1], axis=-1)


def kernel(x, c, ctx, c_ctx, ln_in_g, ln_in_b, ada_w, ada_b, w_in, s5_lam_re, s5_lam_im, s5_log_dt, s5_b_re, s5_b_im, s5_c_re, s5_c_im, s5_d, s5_w_glu, s5_b_glu, lru_conv_w, lru_conv_b, lru_w_a, lru_b_a, lru_w_x, lru_b_x, lru_lam, w_out, ln1_g, ln1_b, router_w, router_bias, exp_w_gate, exp_w_up, exp_w_down, sh_w_gate, sh_w_up, sh_w_down, ln2_g, ln2_b):
    bsz, seq, dm = x.shape
    cl = ctx.shape[1]
    assert ada_w.shape[0] == DEPTH
    d_s5 = s5_w_glu.shape[1]
    d_lru = lru_lam.shape[2]
    ngrp = d_s5 // S5_GROUP
    rows = seq // GRID_W
    t = bsz * seq
    lc = S5_CHUNK
    nkl, nkc = seq // lc, cl // lc

    pad = (-(bsz + 1)) % 8
    cc = jnp.concatenate([c, c_ctx[None, :], jnp.zeros((pad, dm), F32)], axis=0)
    mods = _ada(cc, ada_w[0], ada_b[0])
    sh1, sc1, g1, sh2, sc2, g2 = [mods[:bsz, k * dm:(k + 1) * dm].reshape(bsz, 1, dm) for k in range(6)]
    csh1, csc1 = [mods[bsz:bsz + 1, k * dm:(k + 1) * dm].reshape(1, 1, dm) for k in range(2)]

    x2d = x.reshape(t, dm)
    w_in_b = w_in[0].astype(BF16)
    us5, ulru = _in_proj(x2d, ln_in_g, ln_in_b, sc1, sh1, w_in_b, seq, d_s5)
    ucs5, uclru = _in_proj(ctx.reshape(bsz * cl, dm), ln_in_g, ln_in_b, csc1, csh1, w_in_b, bsz * cl, d_s5)

    def to_chunks(u, nk):
        u = u.reshape(bsz, nk, lc, ngrp, S5_GROUP).transpose(3, 1, 0, 2, 4)
        return u.reshape(ngrp, nk * bsz, lc * S5_GROUP)

    lam3 = jnp.stack([_quad(s5_lam_re[0, 0], s5_lam_re[0, 1]), _quad(s5_lam_im[0, 0], s5_lam_im[0, 1]),
                      _quad(*[jnp.broadcast_to(s5_log_dt[0, k][:, None], (ngrp, S5_STATE)) for k in range(2)])],
                     axis=1)
    bt_re = jnp.swapaxes(s5_b_re[0], -1, -2)
    bt_im = jnp.swapaxes(s5_b_im[0], -1, -2)
    tt, win, wout, a16 = _s5_prep(lam3, _quad(bt_re[0], bt_re[1]), _quad(bt_im[0], bt_im[1]),
                                  _quad(s5_c_re[0, 0], s5_c_re[0, 1]), _quad(s5_c_im[0, 0], s5_c_im[0, 1]),
                                  s5_d[0].reshape(ngrp, S5_GROUP, 1))
    ys = _s5(to_chunks(us5, nkl), to_chunks(ucs5, nkc), tt, win, wout, a16, bsz)
    ys5 = ys.reshape(ngrp, nkl, bsz, lc, S5_GROUP).transpose(2, 1, 3, 0, 4).reshape(t, d_s5)

    c_blk = 256
    hd = d_lru // LRU_HEADS
    hpb = c_blk // hd
    ncb = d_lru // c_blk

    def blockdiag(wh):
        wh = wh.reshape(ncb, hpb, hd, hd)
        eye = jnp.eye(hpb, dtype=wh.dtype)
        return jnp.einsum("nhij,hk->nhikj", wh, eye).reshape(ncb, c_blk, c_blk)

    wg = jnp.concatenate([blockdiag(lru_w_a[0, 0]), blockdiag(lru_w_x[0, 0]),
                          blockdiag(lru_w_a[0, 1]), blockdiag(lru_w_x[0, 1])], axis=-1).astype(BF16)
    bgate = jnp.concatenate([lru_b_a[0, 0].reshape(ncb, 1, c_blk), lru_b_x[0, 0].reshape(ncb, 1, c_blk),
                             lru_b_a[0, 1].reshape(ncb, 1, c_blk), lru_b_x[0, 1].reshape(ncb, 1, c_blk)], axis=-1)
    ylru = _lru(ulru.reshape(bsz, rows, GRID_W, 2 * d_lru), uclru.reshape(bsz, cl, 2 * d_lru),
                lru_conv_w[0], lru_conv_b[0].reshape(1, d_lru), wg, bgate, lru_lam[0], c_blk)
    ylru = ylru.reshape(t, d_lru)

    w_out_b = w_out[0].astype(BF16)
    h1, v, eidx, lrank, wts, tcnt = _mix(
        ys5, ylru, x2d, ln_in_g, ln_in_b, g1, sc2, sh2, s5_w_glu[0].astype(BF16), s5_b_glu[0],
        w_out_b[:d_s5], w_out_b[d_s5:], ln1_g[0], ln1_b[0], router_w[0].T,
        router_bias[0].reshape(N_EXPERTS, 1), seq)

    n_rows = _sorted_rows(t)
    tc = tcnt.transpose(0, 2, 1).reshape(t // TOKEN_TILE, N_EXPERTS)
    dst, soff, utot, start, total = _plan(tc)
    tail, vblock, vxblock, vexp, vlo, vhi = _visits(start.reshape(-1), total.reshape(-1), n_rows)
    xs, srow = _dispatch(tc, dst, soff, utot, tail, eidx, lrank, v, n_rows)
    ysort = _gmm(vblock, vxblock, vexp, vlo, vhi, xs, exp_w_gate[0], exp_w_up[0], exp_w_down[0])
    out = _combine(tc, dst, soff, utot, srow.T, wts.T, v, h1, g2, sh_w_gate[0].astype(BF16),
                   sh_w_up[0].astype(BF16), sh_w_down[0].astype(BF16), ln2_g[0], ln2_b[0], ysort, seq)
    return out.reshape(bsz, seq, dm).astype(x.dtype)
```

```python
import functools
import math

import jax
import jax.numpy as jnp
from jax import lax
from jax.experimental import pallas as pl
from jax.experimental.pallas import tpu as pltpu

F32 = jnp.float32
BF16 = jnp.bfloat16
I32 = jnp.int32
HIGHEST = lax.Precision.HIGHEST

GRID_W = 64
S5_GROUP = 16
S5_STATE = 64
S5_CHUNK = 16
LRU_HEADS = 8
LRU_C = 8.0
CONV_W = 4
CONV_LEFT = 2
N_EXPERTS = 64
TOP_K = 8
N_GROUPS = 8
TOPK_GROUPS = 4
ROUTED_SCALE = 2.5
LN_EPS = 1e-5
DEPTH = 1
ALPHA = (2.0 * DEPTH) ** 0.25

ROW_BLOCK = 512
GMM_SPLIT = 2
TOKEN_TILE = 256
SEG_ALIGN = 16
NSLOT = TOKEN_TILE * TOP_K + N_EXPERTS * SEG_ALIGN
SLOT_CHUNK = NSLOT // 4
PIECE_SIZES = (TOKEN_TILE // SEG_ALIGN).bit_length()
PIECE_PACK = 1 << 19
VMEM_LIMIT = 52 * 1024 * 1024


def _cparams(sem):
    return pltpu.CompilerParams(dimension_semantics=sem, vmem_limit_bytes=VMEM_LIMIT)


def _ln(x, g, b):
    mu = jnp.mean(x, axis=-1, keepdims=True)
    xc = x - mu
    var = jnp.mean(xc * xc, axis=-1, keepdims=True)
    return xc * lax.rsqrt(var + LN_EPS) * g + b


def _gelu(x):
    return x * (0.5 * (1.0 + jnp.tanh(math.sqrt(2.0 / math.pi) * (x + 0.044715 * (x * x * x)))))


def _silu(x):
    return x * jax.nn.sigmoid(x)


def _softplus(x):
    return jnp.maximum(x, 0.0) + jnp.log1p(jnp.exp(-jnp.abs(x)))


def _dot(a, b):
    return jnp.dot(a, b, preferred_element_type=F32)


def _dot_nt(a, b, precision=None):
    return lax.dot_general(a, b, (((1,), (1,)), ((), ())), precision=precision,
                           preferred_element_type=F32)


def _ada_kernel(c_ref, w_ref, b_ref, o_ref):
    s = _silu(c_ref[...])
    o_ref[...] = jnp.dot(s, w_ref[...], precision=HIGHEST, preferred_element_type=F32) + b_ref[...]


def _ada(cc, ada_w, ada_b):
    r, d = cc.shape
    n = ada_w.shape[1]
    tn = 512
    return pl.pallas_call(
        _ada_kernel,
        grid=(n // tn,),
        in_specs=[pl.BlockSpec((r, d), lambda j: (0, 0)),
                  pl.BlockSpec((d, tn), lambda j: (0, j)),
                  pl.BlockSpec((1, tn), lambda j: (0, j))],
        out_specs=pl.BlockSpec((r, tn), lambda j: (0, j)),
        out_shape=jax.ShapeDtypeStruct((r, n), F32),
        compiler_params=_cparams(("parallel",)),
        name="ada",
    )(cc, ada_w, ada_b.reshape(1, n))


def _in_proj_kernel(x_ref, g_ref, b_ref, sc_ref, sh_ref, w_ref, us5_ref, ulru_ref):
    h = _ln(x_ref[...], g_ref[...], b_ref[...])
    m = h * (1.0 + sc_ref[...]) + sh_ref[...]
    u = _dot(m.astype(BF16), w_ref[...])
    ds5 = us5_ref.shape[-1]
    us5_ref[...] = u[:, :ds5].astype(BF16)
    ulru_ref[...] = u[:, ds5:].astype(BF16)


def _in_proj(x2d, ln_g, ln_b, sc, sh, w_bf16, rows_per_mod, d_s5, tm=512):
    t, d = x2d.shape
    n = w_bf16.shape[1]
    tm = min(tm, t)
    mod_spec = pl.BlockSpec((None, 1, d), lambda i: ((i * tm) // rows_per_mod, 0, 0))
    vec = pl.BlockSpec((1, d), lambda i: (0, 0))
    return pl.pallas_call(
        _in_proj_kernel,
        grid=(t // tm,),
        in_specs=[pl.BlockSpec((tm, d), lambda i: (i, 0)), vec, vec, mod_spec, mod_spec,
                  pl.BlockSpec((d, n), lambda i: (0, 0))],
        out_specs=[pl.BlockSpec((tm, d_s5), lambda i: (i, 0)),
                   pl.BlockSpec((tm, n - d_s5), lambda i: (i, 0))],
        out_shape=[jax.ShapeDtypeStruct((t, d_s5), BF16),
                   jax.ShapeDtypeStruct((t, n - d_s5), BF16)],
        compiler_params=_cparams(("parallel",)),
        name="in_proj",
    )(x2d, ln_g.reshape(1, d), ln_b.reshape(1, d), sc, sh, w_bf16)


def _s5_prep_kernel(lam_ref, bre_ref, bim_ref, cre_ref, cim_ref, d_ref,
                    tt_ref, win_ref, wout_ref, a16_ref, cp_scr):
    p, lc, gs = S5_STATE, S5_CHUNK, S5_GROUP
    nl = 4 * p
    lre = lam_ref[0:1, :]
    lim = lam_ref[1:2, :]
    dt = jnp.exp(lam_ref[2:3, :])
    lane = lax.broadcasted_iota(I32, (1, nl), 1)
    is_re = lane < 2 * p
    is_f = (lane % (2 * p)) < p

    def powers(ef, eb):
        e = jnp.where(is_f, float(ef), float(eb))
        mag = jnp.exp(lre * dt * e)
        ang = lim * dt * e
        return mag * jnp.cos(ang), mag * jnp.sin(ang)

    bre, bim = bre_ref[...], bim_ref[...]
    cre, cim = cre_ref[...], cim_ref[...]
    ar, ai = powers(1, 1)
    den = lre * lre + lim * lim
    nr = ar - 1.0
    fr = (nr * lre + ai * lim) / den
    fi = (ai * lre - nr * lim) / den
    bbre = fr * bre - fi * bim
    bbim = fr * bim + fi * bre

    def c_times(pc, ps):
        return cre * jnp.where(is_re, pc, -ps) + cim * jnp.where(is_re, -ps, -pc)

    for s in range(lc):
        pc, ps = powers(lc - 1 - s, s)
        qa = jnp.where(is_re, pc, ps)
        qb = jnp.where(is_re, -ps, pc)
        win_ref[gs * s:gs * (s + 1), :] = (bbre * qa + bbim * qb).astype(BF16)
    for t in range(lc):
        pc, ps = powers(t + 1, lc - t)
        wout_ref[gs * t:gs * (t + 1), :] = c_times(pc, ps).astype(BF16)
    for j in range(lc):
        pc, ps = powers(j, lc - 1 - j)
        cp_scr[gs * j:gs * (j + 1), :] = c_times(pc, ps)

    bbcat = jnp.where(is_re, bbre, bbim)
    cp = cp_scr[...]
    ktf = _dot_nt(jnp.where(is_f, bbcat, 0.0), cp, HIGHEST)
    ktb = _dot_nt(jnp.where(is_f, 0.0, bbcat), cp, HIGHEST)
    lane2 = lax.broadcasted_iota(I32, (gs, gs * lc), 1)
    row2 = lax.broadcasted_iota(I32, (gs, gs * lc), 0)
    dcol = d_ref[...]
    width = gs * lc
    for s in range(lc):
        sf = gs * s
        tf = ktf if sf == 0 else pltpu.roll(ktf, sf, 1)
        tf = jnp.where(lane2 >= sf, tf, 0.0)
        sb = (gs * (s + 1)) % width
        tb = ktb if sb == 0 else pltpu.roll(ktb, sb, 1)
        tb = jnp.where(lane2 < gs * (s + 1), tb, 0.0)
        skip = jnp.where(lane2 == sf + row2, dcol, 0.0)
        tt_ref[gs * s:gs * (s + 1), :] = (tf + tb + skip).astype(BF16)
    pc, ps = powers(lc, lc)
    a16_ref[0:1, :] = pc[:, :2 * p]
    a16_ref[1:2, :] = ps[:, :2 * p]


def _s5_prep(lam3, b4re, b4im, c4re, c4im, dcol):
    g = lam3.shape[0]
    nl = 4 * S5_STATE
    k = S5_GROUP * S5_CHUNK
    m3 = lambda i: (i, 0, 0)
    return pl.pallas_call(
        _s5_prep_kernel,
        grid=(g,),
        in_specs=[pl.BlockSpec((None, 3, nl), m3)]
        + [pl.BlockSpec((None, S5_GROUP, nl), m3)] * 4
        + [pl.BlockSpec((None, S5_GROUP, 1), m3)],
        out_specs=[pl.BlockSpec((None, k, k), m3), pl.BlockSpec((None, k, nl), m3),
                   pl.BlockSpec((None, k, nl), m3), pl.BlockSpec((None, 2, nl // 2), m3)],
        out_shape=[jax.ShapeDtypeStruct((g, k, k), BF16), jax.ShapeDtypeStruct((g, k, nl), BF16),
                   jax.ShapeDtypeStruct((g, k, nl), BF16), jax.ShapeDtypeStruct((g, 2, nl // 2), F32)],
        scratch_shapes=[pltpu.VMEM((k, nl), F32)],
        compiler_params=_cparams(("parallel",)),
        name="s5_prep",
    )(lam3, b4re, b4im, c4re, c4im, dcol)


def _s5_kernel(xl_ref, xc_ref, tt_ref, win_ref, wout_ref, a16_ref, y_ref,
               z_scr, zc_scr, sf_scr, sb_scr, *, nkl, nkc, bsz):
    h = 2 * S5_STATE
    xl = xl_ref[...]
    win = win_ref[...]
    z_scr[...] = _dot(xl, win)
    zc_scr[...] = _dot(xc_ref[...], win)
    is_f = lax.broadcasted_iota(I32, (bsz, h), 1) < S5_STATE
    ar = a16_ref[0:1, :]
    ai = a16_ref[1:2, :]

    def pick(z_ref, i, n):
        zf = z_ref[pl.ds(i * bsz, bsz), :]
        zb = z_ref[pl.ds((n - 1 - i) * bsz, bsz), :]
        return jnp.where(is_f, zf[:, :h], zb[:, :h]), jnp.where(is_f, zf[:, h:], zb[:, h:])

    def update(sr, si, zr, zi):
        return ar * sr - ai * si + zr, ar * si + ai * sr + zi

    def ctx_step(i, carry):
        zr, zi = pick(zc_scr, i, nkc)
        return update(*carry, zr, zi)

    def lat_step(i, carry):
        sr, si = carry
        st = jnp.concatenate([sr, si], axis=1)
        sf_scr[pl.ds(i * bsz, bsz), :] = st
        sb_scr[pl.ds((nkl - 1 - i) * bsz, bsz), :] = st
        zr, zi = pick(z_scr, i, nkl)
        return update(sr, si, zr, zi)

    zero = jnp.zeros((bsz, h), F32)
    carry = lax.fori_loop(0, nkc, ctx_step, (zero, zero))
    lax.fori_loop(0, nkl, lat_step, carry)

    is_f4 = (lax.broadcasted_iota(I32, (1, 2 * h), 1) % h) < S5_STATE
    xs = jnp.where(is_f4, sf_scr[...], sb_scr[...]).astype(BF16)
    y = _dot(xl, tt_ref[...]) + _dot_nt(xs, wout_ref[...])
    y_ref[...] = y.astype(BF16)


def _s5(xl, xc, tt, win, wout, a16, bsz):
    g, nl_rows, k = xl.shape
    nc_rows = xc.shape[1]
    nl = 4 * S5_STATE
    m3 = lambda i: (i, 0, 0)
    kern = functools.partial(_s5_kernel, nkl=nl_rows // bsz, nkc=nc_rows // bsz, bsz=bsz)
    return pl.pallas_call(
        kern,
        grid=(g,),
        in_specs=[pl.BlockSpec((None, nl_rows, k), m3), pl.BlockSpec((None, nc_rows, k), m3),
                  pl.BlockSpec((None, k, k), m3), pl.BlockSpec((None, k, nl), m3),
                  pl.BlockSpec((None, k, nl), m3), pl.BlockSpec((None, 2, nl // 2), m3)],
        out_specs=pl.BlockSpec((None, nl_rows, k), m3),
        out_shape=jax.ShapeDtypeStruct((g, nl_rows, k), BF16),
        scratch_shapes=[pltpu.VMEM((nl_rows, nl), F32), pltpu.VMEM((nc_rows, nl), F32),
                        pltpu.VMEM((nl_rows, nl), F32), pltpu.VMEM((nl_rows, nl), F32)],
        compiler_params=_cparams(("parallel",)),
        name="s5",
    )(xl, xc, tt, win, wout, a16)


def _lru_kernel(xv_ref, xg_ref, xc_ref, cw_ref, cb_ref, wg_ref, bg_ref, lam_ref, o_ref,
                a_scr, b_scr, ac_scr, bc_scr, cin_scr, *, rows, cl, slab_block):
    w = GRID_W
    c = xv_ref.shape[-1]
    cw = cw_ref[...]
    cb = cb_ref[...]
    sp = _softplus(-lam_ref[...])
    bg = bg_ref[...]
    wg = wg_ref[...]

    def coeffs(xs, store):
        g = _dot(xs.astype(BF16), wg) + bg
        for d in range(2):
            r = jax.nn.sigmoid(g[:, (2 * d) * c:(2 * d + 1) * c])
            i = jax.nn.sigmoid(g[:, (2 * d + 1) * c:(2 * d + 2) * c])
            log_a = -LRU_C * r * sp[d:d + 1, :]
            a = jnp.exp(log_a)
            b = jnp.sqrt(-jnp.tanh(log_a) * (a * a + 1.0)) * (i * xs)
            store(d, a, b)

    def tap(r0, r1, off):
        lo, hi = r0 + off, r1 + off
        clo, chi = max(lo, 0), min(hi, rows)
        parts = []
        if clo > lo:
            parts.append(jnp.zeros((clo - lo, w, c), F32))
        if chi > clo:
            parts.append(xv_ref[clo:chi].astype(F32))
        if hi > chi:
            parts.append(jnp.zeros((hi - chi, w, c), F32))
        return parts[0] if len(parts) == 1 else jnp.concatenate(parts, axis=0)

    for r0 in range(0, rows, slab_block):
        r1 = r0 + slab_block
        xs3 = cb
        for k in range(CONV_W):
            xs3 = xs3 + cw[k:k + 1, :] * tap(r0, r1, k - CONV_LEFT)
        xs = xs3.reshape(slab_block * w, c)

        def store_lat(d, a, b, r0=r0, r1=r1):
            a_scr[d, r0 * w:r1 * w, :] = a
            b_scr[d, r0 * w:r1 * w, :] = b

        coeffs(xs, store_lat)

    xc = xc_ref[...].astype(F32)
    trow = lax.broadcasted_iota(I32, (cl, c), 0)
    xcs = cb
    for k in range(CONV_W):
        off = k - CONV_LEFT
        sh = xc if off == 0 else pltpu.roll(xc, (-off) % cl, 0)
        ok = (trow + off >= 0) & (trow + off < cl)
        xcs = xcs + cw[k:k + 1, :] * jnp.where(ok, sh, 0.0)

    def store_ctx(d, a, b):
        ac_scr[d] = a
        bc_scr[d] = b

    coeffs(xcs, store_ctx)

    for d in range(2):
        rev = d == 1

        def ctx_step(j, h, d=d, rev=rev):
            idx = (cl - 1 - j) if rev else j
            return ac_scr[d, pl.ds(idx, 1), :] * h + bc_scr[d, pl.ds(idx, 1), :]

        h0 = lax.fori_loop(0, cl, ctx_step, jnp.zeros((1, c), F32))

        def col_step(j, carry, d=d, rev=rev):
            h, p = carry
            r = (rows - 1 - j) if rev else j
            off = pl.multiple_of(r * w, w)
            a = a_scr[d, pl.ds(off, w), :]
            h = a * h + b_scr[d, pl.ds(off, w), :]
            p = a * p
            b_scr[d, pl.ds(off, w), :] = h
            a_scr[d, pl.ds(off, w), :] = p
            return h, p

        lax.fori_loop(0, rows, col_step, (jnp.zeros((w, c), F32), jnp.ones((w, c), F32)))

        last = 0 if rev else (rows - 1) * w

        def carry_step(j, cin, d=d, rev=rev, last=last):
            col = (w - 1 - j) if rev else j
            cin_scr[pl.ds(col, 1), :] = cin
            return a_scr[d, pl.ds(last + col, 1), :] * cin + b_scr[d, pl.ds(last + col, 1), :]

        lax.fori_loop(0, w, carry_step, h0)
        cin = cin_scr[...]

        if not rev:
            def fix_step(r, _, d=d, cin=cin):
                off = pl.multiple_of(r * w, w)
                b_scr[d, pl.ds(off, w), :] = b_scr[d, pl.ds(off, w), :] + a_scr[d, pl.ds(off, w), :] * cin
                return 0

            lax.fori_loop(0, rows, fix_step, 0)
        else:
            def out_step(r, _, cin=cin):
                off = pl.multiple_of(r * w, w)
                hsum = (b_scr[0, pl.ds(off, w), :] + b_scr[1, pl.ds(off, w), :]
                        + a_scr[1, pl.ds(off, w), :] * cin)
                o_ref[r] = (hsum * _gelu(xg_ref[r].astype(F32))).astype(o_ref.dtype)
                return 0

            lax.fori_loop(0, rows, out_step, 0)


def _lru(u4, uc3, conv_w, conv_b, wg, bg, lam, c_blk=256):
    bsz, rows, w, n2 = u4.shape
    d_lru = n2 // 2
    cl = uc3.shape[1]
    ncb = d_lru // c_blk
    kern = functools.partial(_lru_kernel, rows=rows, cl=cl, slab_block=8)
    n = rows * w
    return pl.pallas_call(
        kern,
        grid=(bsz, ncb),
        in_specs=[pl.BlockSpec((None, rows, w, c_blk), lambda b, j: (b, 0, 0, j)),
                  pl.BlockSpec((None, rows, w, c_blk), lambda b, j: (b, 0, 0, ncb + j)),
                  pl.BlockSpec((None, cl, c_blk), lambda b, j: (b, 0, j)),
                  pl.BlockSpec((CONV_W, c_blk), lambda b, j: (0, j)),
                  pl.BlockSpec((1, c_blk), lambda b, j: (0, j)),
                  pl.BlockSpec((None, c_blk, 4 * c_blk), lambda b, j: (j, 0, 0)),
                  pl.BlockSpec((None, 1, 4 * c_blk), lambda b, j: (j, 0, 0)),
                  pl.BlockSpec((2, c_blk), lambda b, j: (0, j))],
        out_specs=pl.BlockSpec((None, rows, w, c_blk), lambda b, j: (b, 0, 0, j)),
        out_shape=jax.ShapeDtypeStruct((bsz, rows, w, d_lru), BF16),
        scratch_shapes=[pltpu.VMEM((2, n, c_blk), F32), pltpu.VMEM((2, n, c_blk), F32),
                        pltpu.VMEM((2, cl, c_blk), F32), pltpu.VMEM((2, cl, c_blk), F32),
                        pltpu.VMEM((w, c_blk), F32)],
        compiler_params=_cparams(("parallel", "parallel")),
        name="lru",
    )(u4, u4, uc3, conv_w, conv_b, wg, bg, lam)


def _mix_kernel(ys5_ref, ylru_ref, x_ref, lng_ref, lnb_ref, g1_ref, sc2_ref, sh2_ref,
                wglu_ref, bglu_ref, wo1_ref, wo2_ref, l1g_ref, l1b_ref, rwt_ref, rb_ref, tri_ref,
                h1_ref, v_ref, eidx_ref, rank_ref, wts_ref, tcnt_ref):
    y = _gelu(ys5_ref[...].astype(F32))
    s5o = y * jax.nn.sigmoid(_dot(y.astype(BF16), wglu_ref[...]) + bglu_ref[...])
    y1 = _dot(s5o.astype(BF16), wo1_ref[...]) + _dot(ylru_ref[...], wo2_ref[...])
    h = _ln(x_ref[...], lng_ref[...], lnb_ref[...])
    h1 = _ln(ALPHA * h + g1_ref[...] * y1, l1g_ref[...], l1b_ref[...])
    h1_ref[...] = h1
    v = h1 * (1.0 + sc2_ref[...]) + sh2_ref[...]
    v_ref[...] = v.astype(v_ref.dtype)

    tm = v.shape[0]
    ne, ng, gsz = N_EXPERTS, N_GROUPS, N_EXPERTS // N_GROUPS
    scores = jax.nn.sigmoid(_dot_nt(rwt_ref[...], v, HIGHEST))
    s3 = scores.reshape(ng, gsz, tm)
    sel3 = (scores + rb_ref[...]).reshape(ng, gsz, tm)
    ii = lax.broadcasted_iota(I32, (ng, gsz, tm), 1)
    gi = lax.broadcasted_iota(I32, (ng, gsz, tm), 0)
    neg = -jnp.inf

    m1 = jnp.max(sel3, axis=1, keepdims=True)
    f1 = jnp.min(jnp.where(sel3 == m1, ii, gsz), axis=1, keepdims=True)
    m2 = jnp.max(jnp.where(ii == f1, neg, sel3), axis=1, keepdims=True)
    cur = m1 + m2
    gidx = lax.broadcasted_iota(I32, (ng, 1, tm), 0)
    gmask = jnp.zeros((ng, 1, tm), jnp.bool_)
    for _ in range(TOPK_GROUPS):
        mx = jnp.max(cur, axis=0, keepdims=True)
        fg = jnp.min(jnp.where(cur == mx, gidx, ng), axis=0, keepdims=True)
        hit = gidx == fg
        gmask = gmask | hit
        cur = jnp.where(hit, neg, cur)

    selm = jnp.where(gmask, sel3, neg)
    eid = gi * gsz + ii
    picks, erows, wrows = [], [], []
    for _ in range(TOP_K):
        mx = jnp.max(jnp.max(selm, axis=1, keepdims=True), axis=0, keepdims=True)
        fe = jnp.min(jnp.min(jnp.where(selm == mx, eid, ne), axis=1, keepdims=True),
                     axis=0, keepdims=True)
        hit = eid == fe
        selm = jnp.where(hit, neg, selm)
        picks.append(hit)
        erows.append(fe.reshape(1, tm))
        wsel = jnp.where(hit, s3, 0.0)
        wrows.append(jnp.sum(jnp.sum(wsel, axis=1, keepdims=True), axis=0).reshape(1, tm))
    denom = wrows[0]
    for k in range(1, TOP_K):
        denom = denom + wrows[k]

    chosen = picks[0]
    for k in range(1, TOP_K):
        chosen = chosen | picks[k]
    chosen_f = jnp.where(chosen, 1.0, 0.0).reshape(ne, tm)
    cnt3 = _dot(chosen_f.astype(BF16), tri_ref[...]).reshape(ng, gsz, tm)
    rrows = []
    for k in range(TOP_K):
        rsel = jnp.where(picks[k], cnt3, 0.0)
        rrows.append(jnp.sum(jnp.sum(rsel, axis=1, keepdims=True), axis=0).reshape(1, tm))
    tcnt_ref[...] = jnp.concatenate(
        [jnp.sum(chosen_f[:, j * TOKEN_TILE:(j + 1) * TOKEN_TILE], axis=1, keepdims=True)
         for j in range(tm // TOKEN_TILE)], axis=1).astype(I32)

    eidx_ref[...] = jnp.concatenate(erows, axis=0)
    rank_ref[...] = jnp.concatenate(rrows, axis=0).astype(I32)
    wts_ref[...] = jnp.concatenate([wr / denom * ROUTED_SCALE for wr in wrows], axis=0)


def _mix(ys5, ylru, x2d, ln_g, ln_b, g1, sc2, sh2, wglu, bglu, wo1, wo2, l1g, l1b, rwt, rb, seq, tm=512):
    t, d = x2d.shape
    ds = ys5.shape[1]
    ne = rwt.shape[0]
    nsub = tm // TOKEN_TILE
    r_i = lax.broadcasted_iota(I32, (tm, tm), 0)
    c_i = lax.broadcasted_iota(I32, (tm, tm), 1)
    tri = ((r_i < c_i) & (r_i // TOKEN_TILE == c_i // TOKEN_TILE)).astype(BF16)
    row = lambda n: pl.BlockSpec((tm, n), lambda i: (i, 0))
    vec = lambda n: pl.BlockSpec((1, n), lambda i: (0, 0))
    mod = pl.BlockSpec((None, 1, d), lambda i: ((i * tm) // seq, 0, 0))
    full = lambda a, b: pl.BlockSpec((a, b), lambda i: (0, 0))
    tok = pl.BlockSpec((TOP_K, tm), lambda i: (0, i))
    return pl.pallas_call(
        _mix_kernel,
        grid=(t // tm,),
        in_specs=[row(ds), row(ds), row(d), vec(d), vec(d), mod, mod, mod,
                  full(ds, ds), vec(ds), full(ds, d), full(ds, d), vec(d), vec(d),
                  full(ne, d), full(ne, 1), full(tm, tm)],
        out_specs=[row(d), row(d), tok, tok, tok,
                   pl.BlockSpec((None, ne, nsub), lambda i: (i, 0, 0))],
        out_shape=[jax.ShapeDtypeStruct((t, d), F32), jax.ShapeDtypeStruct((t, d), BF16),
                   jax.ShapeDtypeStruct((TOP_K, t), I32), jax.ShapeDtypeStruct((TOP_K, t), I32),
                   jax.ShapeDtypeStruct((TOP_K, t), F32),
                   jax.ShapeDtypeStruct((t // tm, ne, nsub), I32)],
        compiler_params=_cparams(("parallel",)),
        name="mix",
    )(ys5, ylru, x2d, ln_g.reshape(1, d), ln_b.reshape(1, d), g1, sc2, sh2,
      wglu, bglu.reshape(1, ds), wo1, wo2, l1g.reshape(1, d), l1b.reshape(1, d), rwt, rb, tri)


def _plan_kernel(tc_ref, soff_ref, utot_ref, start_ref, total_ref, pieces_ref, npieces_ref):
    tc = tc_ref[...]
    ne, nt = tc.shape
    unit_shift = SEG_ALIGN.bit_length() - 1
    units = lax.shift_right_logical(tc + (SEG_ALIGN - 1), unit_shift)
    seg = (units * SEG_ALIGN).astype(F32)
    earlier_exp = jnp.where(lax.broadcasted_iota(I32, (ne, ne), 0) > lax.broadcasted_iota(I32, (ne, ne), 1),
                            1.0, 0.0)
    earlier_tile = jnp.where(lax.broadcasted_iota(I32, (nt, nt), 0) < lax.broadcasted_iota(I32, (nt, nt), 1),
                             1.0, 0.0)

    def exact(a, b):
        return jnp.dot(a, b, precision=HIGHEST, preferred_element_type=F32)

    soff = exact(earlier_exp, seg)
    total = jnp.sum(seg, axis=1, keepdims=True)
    start = jnp.sum(soff, axis=1, keepdims=True)
    dst = exact(seg, earlier_tile) + start
    soff_ref[...] = soff.astype(I32)
    utot_ref[...] = (jnp.sum(seg, axis=0, keepdims=True) * (1.0 / SEG_ALIGN)).astype(I32)
    start_ref[...] = start.astype(I32)
    total_ref[...] = total.astype(I32)

    counts = []
    for k in range(PIECE_SIZES):
        has = jnp.bitwise_and(lax.shift_right_logical(units, k), 1)
        below = (jnp.bitwise_and(units, (1 << k) - 1) * SEG_ALIGN).astype(F32)
        pos = exact(earlier_exp, has.astype(F32))
        counts.append(jnp.sum(has, axis=0, keepdims=True))
        rows = []
        for j in range(ne):
            sel = (has > 0) & (pos == float(j))
            srow = jnp.sum(jnp.where(sel, soff + below, 0.0), axis=0, keepdims=True).astype(I32)
            drow = jnp.sum(jnp.where(sel, dst + below, 0.0), axis=0, keepdims=True).astype(I32)
            rows.append(jnp.bitwise_or(lax.shift_left(srow, PIECE_PACK.bit_length() - 1), drow))
        pieces_ref[k * ne:(k + 1) * ne, :] = jnp.concatenate(rows, axis=0)
    counts += [jnp.zeros((1, nt), I32)] * (npieces_ref.shape[0] - PIECE_SIZES)
    npieces_ref[...] = jnp.concatenate(counts, axis=0)


def _plan(tc_t, n_rows):
    ne, nt = tc_t.shape
    assert n_rows <= PIECE_PACK
    full = lambda *s: pl.BlockSpec(s, lambda: (0,) * len(s))
    return pl.pallas_call(
        _plan_kernel,
        in_specs=[full(ne, nt)],
        out_specs=[full(ne, nt), full(1, nt), full(ne, 1), full(ne, 1), full(PIECE_SIZES * ne, nt),
                   full(8, nt)],
        out_shape=[jax.ShapeDtypeStruct((ne, nt), I32), jax.ShapeDtypeStruct((1, nt), I32),
                   jax.ShapeDtypeStruct((ne, 1), I32), jax.ShapeDtypeStruct((ne, 1), I32),
                   jax.ShapeDtypeStruct((PIECE_SIZES * ne, nt), I32), jax.ShapeDtypeStruct((8, nt), I32)],
        name="plan",
    )(tc_t)


def _visits_kernel(start_ref, total_ref, tail_ref, blk_ref, xblk_ref, exp_ref, lo_ref, hi_ref,
                   *, n_blocks, nv):
    ne = N_EXPERTS
    shift = ROW_BLOCK.bit_length() - 1

    def put(pos, blk, xblk, e, lo, hi):
        blk_ref[pos] = blk
        xblk_ref[pos] = xblk
        exp_ref[pos] = e
        lo_ref[pos] = lo
        hi_ref[pos] = hi

    def per_expert(e, carry):
        pos, done = carry
        off = start_ref[e]
        cnt = total_ref[e]
        first = lax.shift_right_logical(off, shift)
        last = lax.shift_right_logical(off + cnt - 1, shift)
        nvis = jnp.where(cnt > 0, last - first + 1, 0)

        def put_vis(k, c):
            put(pos + k, first + k, first + k, e, off, off + cnt)
            return c

        lax.fori_loop(0, nvis, put_vis, 0)
        return pos + nvis, jnp.where(cnt > 0, last + 1, done)

    pos, done = lax.fori_loop(0, ne, per_expert, (jnp.int32(0), jnp.int32(0)))
    tail_ref[0] = start_ref[ne - 1] + total_ref[ne - 1]

    def put_tail(k, c):
        put(pos + k, done + k, 0, ne - 1, 0, 0)
        return c

    lax.fori_loop(0, n_blocks - done, put_tail, 0)

    def put_rest(j, c):
        put(j, n_blocks - 1, 0, ne - 1, 0, 0)
        return c

    lax.fori_loop(pos + (n_blocks - done), nv, put_rest, 0)


def _sorted_rows(n_tokens):
    nt = n_tokens // TOKEN_TILE
    rows = n_tokens * TOP_K + nt * N_EXPERTS * (SEG_ALIGN - 1)
    return -(-rows // ROW_BLOCK) * ROW_BLOCK


def _visits(start, total, n_rows):
    ne = start.shape[0]
    n_blocks = n_rows // ROW_BLOCK
    nv = n_blocks + ne
    smem = pl.BlockSpec(memory_space=pltpu.SMEM)
    vec = jax.ShapeDtypeStruct((nv,), I32)
    return pl.pallas_call(
        functools.partial(_visits_kernel, n_blocks=n_blocks, nv=nv),
        in_specs=[smem, smem],
        out_specs=[smem] * 6,
        out_shape=[jax.ShapeDtypeStruct((1,), I32), vec, vec, vec, vec, vec],
        name="visits",
    )(start, total)


def _start_segment_copies(pieces_ref, npieces_ref, tile, hbm_ref, buf, to_hbm, sem):
    for k in range(PIECE_SIZES):
        size = SEG_ALIGN << k

        def body(j, c, k=k, size=size):
            packed = pieces_ref[k * N_EXPERTS + j, tile]
            srow = pl.multiple_of(lax.shift_right_logical(packed, PIECE_PACK.bit_length() - 1), SEG_ALIGN)
            drow = pl.multiple_of(jnp.bitwise_and(packed, PIECE_PACK - 1), SEG_ALIGN)
            slot_rows = buf.at[pl.ds(srow, size), :]
            sorted_rows = hbm_ref.at[pl.ds(drow, size), :]
            src, dst = (slot_rows, sorted_rows) if to_hbm else (sorted_rows, slot_rows)
            pltpu.make_async_copy(src, dst, sem).start()
            return c

        lax.fori_loop(0, npieces_ref[k, tile], body, 0)


def _wait_segment_copies(units, hbm_ref, buf, to_hbm, sem):
    for k in range(NSLOT.bit_length() - SEG_ALIGN.bit_length() + 1):
        @pl.when(jnp.bitwise_and(units, 1 << k) != 0)
        def _(k=k):
            size = SEG_ALIGN << k
            slot_rows = buf.at[pl.ds(0, size), :]
            sorted_rows = hbm_ref.at[pl.ds(0, size), :]
            src, dst = (slot_rows, sorted_rows) if to_hbm else (sorted_rows, slot_rows)
            pltpu.make_async_copy(src, dst, sem).wait()


def _dispatch_kernel(pieces_ref, npieces_ref, soff_ref, utot_ref, tail_ref, eidx_ref, lr_ref, v_ref,
                     xs_ref, srow_ref, cbuf, zbuf, sems, zsem):
    i = pl.program_id(0)
    nt = pl.num_programs(0)
    tm = v_ref.shape[0]
    s = i % 2

    @pl.when(i >= 2)
    def _():
        _wait_segment_copies(utot_ref[0, i - 2], xs_ref, cbuf.at[s], True, sems.at[s])

    e8 = eidx_ref[...]
    base = jnp.zeros_like(e8)
    for e in range(N_EXPERTS):
        base = jnp.where(e8 == e, soff_ref[e, i], base)
    tr = base + lr_ref[...]
    srow_ref[...] = tr

    vb = v_ref[...]
    for c0 in range(0, NSLOT, SLOT_CHUNK):
        rows = lax.broadcasted_iota(I32, (SLOT_CHUNK, tm), 0) + c0
        onehot = jnp.zeros((SLOT_CHUNK, tm), F32)
        for k in range(TOP_K):
            onehot = jnp.where(rows == tr[k:k + 1, :], 1.0, onehot)
        cbuf[s, c0:c0 + SLOT_CHUNK, :] = _dot(onehot.astype(BF16), vb).astype(BF16)
    _start_segment_copies(pieces_ref, npieces_ref, i, xs_ref, cbuf.at[s], True, sems.at[s])

    @pl.when(i == nt - 1)
    def _():
        @pl.when(i >= 1)
        def _():
            _wait_segment_copies(utot_ref[0, i - 1], xs_ref, cbuf.at[1 - s], True, sems.at[1 - s])

        _wait_segment_copies(utot_ref[0, i], xs_ref, cbuf.at[s], True, sems.at[s])
        zbuf[...] = jnp.zeros_like(zbuf)

        def zero_rows(start, size):
            rows = pl.ds(pl.multiple_of(start, SEG_ALIGN), size)
            return pltpu.make_async_copy(zbuf.at[pl.ds(0, size), :], xs_ref.at[rows, :], zsem)

        tail = tail_ref[0]
        n_small = lax.shift_right_logical(jnp.bitwise_and(-tail, ROW_BLOCK - 1),
                                          SEG_ALIGN.bit_length() - 1)
        tail_blk = tail + n_small * SEG_ALIGN
        n_big = lax.shift_right_logical(xs_ref.shape[0] - tail_blk, ROW_BLOCK.bit_length() - 1)

        def each(fn):
            lax.fori_loop(0, n_small, lambda q, c: fn(zero_rows(tail + q * SEG_ALIGN, SEG_ALIGN), c), 0)
            lax.fori_loop(0, n_big, lambda q, c: fn(zero_rows(tail_blk + q * ROW_BLOCK, ROW_BLOCK), c), 0)

        each(lambda cp, c: (cp.start(), c)[1])
        each(lambda cp, c: (cp.wait(), c)[1])


def _dispatch(pieces, npieces, soff, utot, tail, eidx, lrank, v, n_rows):
    t, d = v.shape
    tm = TOKEN_TILE
    tok = pl.BlockSpec((TOP_K, tm), lambda i, *_: (0, i))
    return pl.pallas_call(
        _dispatch_kernel,
        grid_spec=pltpu.PrefetchScalarGridSpec(
            num_scalar_prefetch=5, grid=(t // tm,),
            in_specs=[tok, tok, pl.BlockSpec((tm, d), lambda i, *_: (i, 0))],
            out_specs=[pl.BlockSpec(memory_space=pl.ANY), tok],
            scratch_shapes=[pltpu.VMEM((2, NSLOT, d), BF16), pltpu.VMEM((ROW_BLOCK, d), BF16),
                            pltpu.SemaphoreType.DMA((2,)), pltpu.SemaphoreType.DMA]),
        out_shape=[jax.ShapeDtypeStruct((n_rows, d), BF16), jax.ShapeDtypeStruct((TOP_K, t), I32)],
        compiler_params=_cparams(("arbitrary",)),
        name="dispatch",
    )(pieces, npieces, soff, utot, tail, eidx, lrank, v)


def _gmm_kernel(vb_ref, vx_ref, ve_ref, vlo_ref, vhi_ref, x_ref, wg_ref, wu_ref, wd_ref, y_ref,
                wg_b, wu_b, wd_b):
    j = pl.program_id(0)
    prev = jnp.maximum(j - 1, 0)
    first = jnp.logical_or(j == 0, vb_ref[prev] != vb_ref[j])
    live = vhi_ref[j] > vlo_ref[j]

    @pl.when(jnp.logical_or(j == 0, ve_ref[prev] != ve_ref[j]))
    def _():
        wg_b[...] = wg_ref[...].astype(BF16)
        wu_b[...] = wu_ref[...].astype(BF16)
        wd_b[...] = wd_ref[...].astype(BF16)

    @pl.when(live)
    def _():
        sub = ROW_BLOCK // GMM_SPLIT
        parts = []
        for r0 in range(0, ROW_BLOCK, sub):
            x = x_ref[r0:r0 + sub, :]
            hidden = _silu(_dot(x, wg_b[...])) * _dot(x, wu_b[...])
            y = _dot(hidden.astype(BF16), wd_b[...])
            rows = vb_ref[j] * ROW_BLOCK + r0 + lax.broadcasted_iota(I32, (sub, 1), 0)
            parts.append((r0, y, (rows >= vlo_ref[j]) & (rows < vhi_ref[j])))

        @pl.when(first)
        def _():
            for r0, y, mine in parts:
                y_ref[r0:r0 + sub, :] = jnp.where(mine, y, 0.0).astype(y_ref.dtype)

        @pl.when(jnp.logical_not(first))
        def _():
            for r0, y, mine in parts:
                keep = y_ref[r0:r0 + sub, :].astype(F32)
                y_ref[r0:r0 + sub, :] = jnp.where(mine, y, keep).astype(y_ref.dtype)

    @pl.when(jnp.logical_and(jnp.logical_not(live), first))
    def _():
        y_ref[...] = jnp.zeros_like(y_ref)


def _gmm(vblock, vxblock, vexp, vlo, vhi, xs, wg, wu, wd):
    n, d = xs.shape
    de = wg.shape[2]
    nv = vblock.shape[0]
    return pl.pallas_call(
        _gmm_kernel,
        grid_spec=pltpu.PrefetchScalarGridSpec(
            num_scalar_prefetch=5, grid=(nv,),
            in_specs=[pl.BlockSpec((ROW_BLOCK, d), lambda j, vb, vx, ve, lo, hi: (vx[j], 0)),
                      pl.BlockSpec((None, d, de), lambda j, vb, vx, ve, lo, hi: (ve[j], 0, 0)),
                      pl.BlockSpec((None, d, de), lambda j, vb, vx, ve, lo, hi: (ve[j], 0, 0)),
                      pl.BlockSpec((None, de, d), lambda j, vb, vx, ve, lo, hi: (ve[j], 0, 0))],
            out_specs=pl.BlockSpec((ROW_BLOCK, d), lambda j, vb, vx, ve, lo, hi: (vb[j], 0)),
            scratch_shapes=[pltpu.VMEM((d, de), BF16), pltpu.VMEM((d, de), BF16),
                            pltpu.VMEM((de, d), BF16)]),
        out_shape=jax.ShapeDtypeStruct((n, d), BF16),
        compiler_params=_cparams(("arbitrary",)),
        name="gmm",
    )(vblock, vxblock, vexp, vlo, vhi, xs, wg, wu, wd)


def _combine_kernel(pieces_ref, npieces_ref, utot_ref, srow_ref, w_ref, v_ref, h1_ref, g2_ref,
                    shg_ref, shu_ref, shd_ref, l2g_ref, l2b_ref, ys_ref, o_ref, ybuf, sems):
    i = pl.program_id(0)
    nt = pl.num_programs(0)
    tm, d = v_ref.shape
    cur = i % 2

    def fetch(tile, b):
        _start_segment_copies(pieces_ref, npieces_ref, tile, ys_ref, ybuf.at[b], False, sems.at[b])

    @pl.when(i == 0)
    def _():
        ybuf[...] = jnp.zeros_like(ybuf)
        fetch(0, 0)

    @pl.when(i + 1 < nt)
    def _():
        fetch(i + 1, 1 - cur)

    vb = v_ref[...]
    hidden = _silu(_dot(vb, shg_ref[...])) * _dot(vb, shu_ref[...])
    f = _dot(hidden.astype(BF16), shd_ref[...])

    tr = srow_ref[...]
    w8 = w_ref[...]
    _wait_segment_copies(utot_ref[0, i], ys_ref, ybuf.at[cur], False, sems.at[cur])
    for c0 in range(0, NSLOT, SLOT_CHUNK):
        lanes = lax.broadcasted_iota(I32, (tm, SLOT_CHUNK), 1) + c0
        pw = jnp.zeros((tm, SLOT_CHUNK), F32)
        for k in range(TOP_K):
            pw = jnp.where(lanes == tr[:, k:k + 1], w8[:, k:k + 1], pw)
        f = f + _dot(pw.astype(BF16), ybuf[cur, c0:c0 + SLOT_CHUNK, :])
    o_ref[...] = _ln(ALPHA * h1_ref[...] + g2_ref[...] * f, l2g_ref[...], l2b_ref[...])


def _combine(pieces, npieces, utot, srow_t, wts_t, v, h1, g2, shg, shu, shd, l2g, l2b, ys, seq):
    t, d = v.shape
    tm = TOKEN_TILE
    dsh = shg.shape[1]
    row = pl.BlockSpec((tm, d), lambda i, *_: (i, 0))
    vec = pl.BlockSpec((1, d), lambda i, *_: (0, 0))
    tok = pl.BlockSpec((tm, TOP_K), lambda i, *_: (i, 0))
    return pl.pallas_call(
        _combine_kernel,
        grid_spec=pltpu.PrefetchScalarGridSpec(
            num_scalar_prefetch=3, grid=(t // tm,),
            in_specs=[tok, tok, row, row,
                      pl.BlockSpec((None, 1, d), lambda i, *_: ((i * tm) // seq, 0, 0)),
                      pl.BlockSpec((d, dsh), lambda i, *_: (0, 0)),
                      pl.BlockSpec((d, dsh), lambda i, *_: (0, 0)),
                      pl.BlockSpec((dsh, d), lambda i, *_: (0, 0)), vec, vec,
                      pl.BlockSpec(memory_space=pl.ANY)],
            out_specs=row,
            scratch_shapes=[pltpu.VMEM((2, NSLOT, d), BF16), pltpu.SemaphoreType.DMA((2,))]),
        out_shape=jax.ShapeDtypeStruct((t, d), F32),
        compiler_params=_cparams(("arbitrary",)),
        name="combine",
    )(pieces, npieces, utot, srow_t, wts_t, v, h1, g2, shg, shu, shd,
      l2g.reshape(1, d), l2b.reshape(1, d), ys)


def _quad(a0, a1):
    return jnp.concatenate([a0, a1, a0, a1], axis=-1)


def kernel(x, c, ctx, c_ctx, ln_in_g, ln_in_b, ada_w, ada_b, w_in, s5_lam_re, s5_lam_im, s5_log_dt, s5_b_re, s5_b_im, s5_c_re, s5_c_im, s5_d, s5_w_glu, s5_b_glu, lru_conv_w, lru_conv_b, lru_w_a, lru_b_a, lru_w_x, lru_b_x, lru_lam, w_out, ln1_g, ln1_b, router_w, router_bias, exp_w_gate, exp_w_up, exp_w_down, sh_w_gate, sh_w_up, sh_w_down, ln2_g, ln2_b):
    bsz, seq, dm = x.shape
    cl = ctx.shape[1]
    assert ada_w.shape[0] == DEPTH
    d_s5 = s5_w_glu.shape[1]
    d_lru = lru_lam.shape[2]
    ngrp = d_s5 // S5_GROUP
    rows = seq // GRID_W
    t = bsz * seq
    lc = S5_CHUNK
    nkl, nkc = seq // lc, cl // lc

    pad = (-(bsz + 1)) % 8
    cc = jnp.concatenate([c, c_ctx[None, :], jnp.zeros((pad, dm), F32)], axis=0)
    mods = _ada(cc, ada_w[0], ada_b[0])
    sh1, sc1, g1, sh2, sc2, g2 = [mods[:bsz, k * dm:(k + 1) * dm].reshape(bsz, 1, dm) for k in range(6)]
    csh1, csc1 = [mods[bsz:bsz + 1, k * dm:(k + 1) * dm].reshape(1, 1, dm) for k in range(2)]

    x2d = x.reshape(t, dm)
    w_in_b = w_in[0].astype(BF16)
    us5, ulru = _in_proj(x2d, ln_in_g, ln_in_b, sc1, sh1, w_in_b, seq, d_s5)
    ucs5, uclru = _in_proj(ctx.reshape(bsz * cl, dm), ln_in_g, ln_in_b, csc1, csh1, w_in_b, bsz * cl, d_s5)

    def to_chunks(u, nk):
        u = u.reshape(bsz, nk, lc, ngrp, S5_GROUP).transpose(3, 1, 0, 2, 4)
        return u.reshape(ngrp, nk * bsz, lc * S5_GROUP)

    lam3 = jnp.stack([_quad(s5_lam_re[0, 0], s5_lam_re[0, 1]), _quad(s5_lam_im[0, 0], s5_lam_im[0, 1]),
                      _quad(*[jnp.broadcast_to(s5_log_dt[0, k][:, None], (ngrp, S5_STATE)) for k in range(2)])],
                     axis=1)
    bt_re = jnp.swapaxes(s5_b_re[0], -1, -2)
    bt_im = jnp.swapaxes(s5_b_im[0], -1, -2)
    tt, win, wout, a16 = _s5_prep(lam3, _quad(bt_re[0], bt_re[1]), _quad(bt_im[0], bt_im[1]),
                                  _quad(s5_c_re[0, 0], s5_c_re[0, 1]), _quad(s5_c_im[0, 0], s5_c_im[0, 1]),
                                  s5_d[0].reshape(ngrp, S5_GROUP, 1))
    ys = _s5(to_chunks(us5, nkl), to_chunks(ucs5, nkc), tt, win, wout, a16, bsz)
    ys5 = ys.reshape(ngrp, nkl, bsz, lc, S5_GROUP).transpose(2, 1, 3, 0, 4).reshape(t, d_s5)

    c_blk = 256
    hd = d_lru // LRU_HEADS
    hpb = c_blk // hd
    ncb = d_lru // c_blk

    def blockdiag(wh):
        wh = wh.reshape(ncb, hpb, hd, hd)
        eye = jnp.eye(hpb, dtype=wh.dtype)
        return jnp.einsum("nhij,hk->nhikj", wh, eye).reshape(ncb, c_blk, c_blk)

    wg = jnp.concatenate([blockdiag(lru_w_a[0, 0]), blockdiag(lru_w_x[0, 0]),
                          blockdiag(lru_w_a[0, 1]), blockdiag(lru_w_x[0, 1])], axis=-1).astype(BF16)
    bgate = jnp.concatenate([lru_b_a[0, 0].reshape(ncb, 1, c_blk), lru_b_x[0, 0].reshape(ncb, 1, c_blk),
                             lru_b_a[0, 1].reshape(ncb, 1, c_blk), lru_b_x[0, 1].reshape(ncb, 1, c_blk)], axis=-1)
    ylru = _lru(ulru.reshape(bsz, rows, GRID_W, 2 * d_lru), uclru.reshape(bsz, cl, 2 * d_lru),
                lru_conv_w[0], lru_conv_b[0].reshape(1, d_lru), wg, bgate, lru_lam[0], c_blk)
    ylru = ylru.reshape(t, d_lru)

    w_out_b = w_out[0].astype(BF16)
    h1, v, eidx, lrank, wts, tcnt = _mix(
        ys5, ylru, x2d, ln_in_g, ln_in_b, g1, sc2, sh2, s5_w_glu[0].astype(BF16), s5_b_glu[0],
        w_out_b[:d_s5], w_out_b[d_s5:], ln1_g[0], ln1_b[0], router_w[0].T,
        router_bias[0].reshape(N_EXPERTS, 1), seq)

    n_rows = _sorted_rows(t)
    tc_t = tcnt.transpose(1, 0, 2).reshape(N_EXPERTS, t // TOKEN_TILE)
    soff, utot, start, total, pieces, npieces = _plan(tc_t, n_rows)
    tail, vblock, vxblock, vexp, vlo, vhi = _visits(start.reshape(-1), total.reshape(-1), n_rows)
    xs, srow = _dispatch(pieces, npieces, soff, utot, tail, eidx, lrank, v, n_rows)
    ysort = _gmm(vblock, vxblock, vexp, vlo, vhi, xs, exp_w_gate[0], exp_w_up[0], exp_w_down[0])
    out = _combine(pieces, npieces, utot, srow.T, wts.T, v, h1, g2, sh_w_gate[0].astype(BF16),
                   sh_w_up[0].astype(BF16), sh_w_down[0].astype(BF16), ln2_g[0], ln2_b[0], ysort, seq)
    return out.reshape(bsz, seq, dm).astype(x.dtype)
```

```python
import functools
import math

import jax
import jax.numpy as jnp
from jax import lax
from jax.experimental import pallas as pl
from jax.experimental.pallas import tpu as pltpu

F32 = jnp.float32
BF16 = jnp.bfloat16
I32 = jnp.int32
HIGHEST = lax.Precision.HIGHEST

GRID_W = 64
S5_GROUP = 16
S5_STATE = 64
S5_CHUNK = 16
LRU_HEADS = 8
LRU_C = 8.0
CONV_W = 4
CONV_LEFT = 2
N_EXPERTS = 64
TOP_K = 8
N_GROUPS = 8
TOPK_GROUPS = 4
ROUTED_SCALE = 2.5
LN_EPS = 1e-5
DEPTH = 1
ALPHA = (2.0 * DEPTH) ** 0.25

ROW_BLOCK = 512
GMM_SPLIT = 2
TOKEN_TILE = 256
SEG_ALIGN = 16
NSLOT = TOKEN_TILE * TOP_K + N_EXPERTS * SEG_ALIGN
SLOT_CHUNK = NSLOT // 12
FOLD_CHUNK = NSLOT // 4
PIECE_SIZES = (TOKEN_TILE // SEG_ALIGN).bit_length()
PIECE_PACK = 1 << 19
VMEM_LIMIT = 52 * 1024 * 1024


def _cparams(sem):
    return pltpu.CompilerParams(dimension_semantics=sem, vmem_limit_bytes=VMEM_LIMIT)


def _ln(x, g, b):
    mu = jnp.mean(x, axis=-1, keepdims=True)
    xc = x - mu
    var = jnp.mean(xc * xc, axis=-1, keepdims=True)
    return xc * lax.rsqrt(var + LN_EPS) * g + b


def _gelu(x):
    return x * (0.5 * (1.0 + jnp.tanh(math.sqrt(2.0 / math.pi) * (x + 0.044715 * (x * x * x)))))


def _silu(x):
    return x * jax.nn.sigmoid(x)


def _softplus(x):
    return jnp.maximum(x, 0.0) + jnp.log1p(jnp.exp(-jnp.abs(x)))


def _dot(a, b):
    return jnp.dot(a, b, preferred_element_type=F32)


def _dot_nt(a, b, precision=None):
    return lax.dot_general(a, b, (((1,), (1,)), ((), ())), precision=precision,
                           preferred_element_type=F32)


def _ada_kernel(c_ref, w_ref, b_ref, o_ref):
    s = _silu(c_ref[...])
    o_ref[...] = jnp.dot(s, w_ref[...], precision=HIGHEST, preferred_element_type=F32) + b_ref[...]


def _ada(cc, ada_w, ada_b):
    r, d = cc.shape
    n = ada_w.shape[1]
    tn = 512
    return pl.pallas_call(
        _ada_kernel,
        grid=(n // tn,),
        in_specs=[pl.BlockSpec((r, d), lambda j: (0, 0)),
                  pl.BlockSpec((d, tn), lambda j: (0, j)),
                  pl.BlockSpec((1, tn), lambda j: (0, j))],
        out_specs=pl.BlockSpec((r, tn), lambda j: (0, j)),
        out_shape=jax.ShapeDtypeStruct((r, n), F32),
        compiler_params=_cparams(("parallel",)),
        name="ada",
    )(cc, ada_w, ada_b.reshape(1, n))


def _in_proj_kernel(x_ref, g_ref, b_ref, sc_ref, sh_ref, w_ref, us5_ref, ulru_ref):
    h = _ln(x_ref[...], g_ref[...], b_ref[...])
    m = h * (1.0 + sc_ref[...]) + sh_ref[...]
    u = _dot(m.astype(BF16), w_ref[...])
    ds5 = us5_ref.shape[-1]
    us5_ref[...] = u[:, :ds5].astype(BF16)
    ulru_ref[...] = u[:, ds5:].astype(BF16)


def _in_proj(x2d, ln_g, ln_b, sc, sh, w_bf16, rows_per_mod, d_s5, tm=512):
    t, d = x2d.shape
    n = w_bf16.shape[1]
    tm = min(tm, t)
    mod_spec = pl.BlockSpec((None, 1, d), lambda i: ((i * tm) // rows_per_mod, 0, 0))
    vec = pl.BlockSpec((1, d), lambda i: (0, 0))
    return pl.pallas_call(
        _in_proj_kernel,
        grid=(t // tm,),
        in_specs=[pl.BlockSpec((tm, d), lambda i: (i, 0)), vec, vec, mod_spec, mod_spec,
                  pl.BlockSpec((d, n), lambda i: (0, 0))],
        out_specs=[pl.BlockSpec((tm, d_s5), lambda i: (i, 0)),
                   pl.BlockSpec((tm, n - d_s5), lambda i: (i, 0))],
        out_shape=[jax.ShapeDtypeStruct((t, d_s5), BF16),
                   jax.ShapeDtypeStruct((t, n - d_s5), BF16)],
        compiler_params=_cparams(("parallel",)),
        name="in_proj",
    )(x2d, ln_g.reshape(1, d), ln_b.reshape(1, d), sc, sh, w_bf16)


def _s5_prep_kernel(lam_ref, bre_ref, bim_ref, cre_ref, cim_ref, d_ref,
                    tt_ref, win_ref, wout_ref, a16_ref, cp_scr):
    p, lc, gs = S5_STATE, S5_CHUNK, S5_GROUP
    nl = 4 * p
    lre = lam_ref[0:1, :]
    lim = lam_ref[1:2, :]
    dt = jnp.exp(lam_ref[2:3, :])
    lane = lax.broadcasted_iota(I32, (1, nl), 1)
    is_re = lane < 2 * p
    is_f = (lane % (2 * p)) < p

    def powers(ef, eb):
        e = jnp.where(is_f, float(ef), float(eb))
        mag = jnp.exp(lre * dt * e)
        ang = lim * dt * e
        return mag * jnp.cos(ang), mag * jnp.sin(ang)

    bre, bim = bre_ref[...], bim_ref[...]
    cre, cim = cre_ref[...], cim_ref[...]
    ar, ai = powers(1, 1)
    den = lre * lre + lim * lim
    nr = ar - 1.0
    fr = (nr * lre + ai * lim) / den
    fi = (ai * lre - nr * lim) / den
    bbre = fr * bre - fi * bim
    bbim = fr * bim + fi * bre

    def c_times(pc, ps):
        return cre * jnp.where(is_re, pc, -ps) + cim * jnp.where(is_re, -ps, -pc)

    for s in range(lc):
        pc, ps = powers(lc - 1 - s, s)
        qa = jnp.where(is_re, pc, ps)
        qb = jnp.where(is_re, -ps, pc)
        win_ref[gs * s:gs * (s + 1), :] = (bbre * qa + bbim * qb).astype(BF16)
    for t in range(lc):
        pc, ps = powers(t + 1, lc - t)
        wout_ref[gs * t:gs * (t + 1), :] = c_times(pc, ps).astype(BF16)
    for j in range(lc):
        pc, ps = powers(j, lc - 1 - j)
        cp_scr[gs * j:gs * (j + 1), :] = c_times(pc, ps)

    bbcat = jnp.where(is_re, bbre, bbim)
    cp = cp_scr[...]
    ktf = _dot_nt(jnp.where(is_f, bbcat, 0.0), cp, HIGHEST)
    ktb = _dot_nt(jnp.where(is_f, 0.0, bbcat), cp, HIGHEST)
    lane2 = lax.broadcasted_iota(I32, (gs, gs * lc), 1)
    row2 = lax.broadcasted_iota(I32, (gs, gs * lc), 0)
    dcol = d_ref[...]
    width = gs * lc
    for s in range(lc):
        sf = gs * s
        tf = ktf if sf == 0 else pltpu.roll(ktf, sf, 1)
        tf = jnp.where(lane2 >= sf, tf, 0.0)
        sb = (gs * (s + 1)) % width
        tb = ktb if sb == 0 else pltpu.roll(ktb, sb, 1)
        tb = jnp.where(lane2 < gs * (s + 1), tb, 0.0)
        skip = jnp.where(lane2 == sf + row2, dcol, 0.0)
        tt_ref[gs * s:gs * (s + 1), :] = (tf + tb + skip).astype(BF16)
    pc, ps = powers(lc, lc)
    a16_ref[0:1, :] = pc[:, :2 * p]
    a16_ref[1:2, :] = ps[:, :2 * p]


def _s5_prep(lam3, b4re, b4im, c4re, c4im, dcol):
    g = lam3.shape[0]
    nl = 4 * S5_STATE
    k = S5_GROUP * S5_CHUNK
    m3 = lambda i: (i, 0, 0)
    return pl.pallas_call(
        _s5_prep_kernel,
        grid=(g,),
        in_specs=[pl.BlockSpec((None, 3, nl), m3)]
        + [pl.BlockSpec((None, S5_GROUP, nl), m3)] * 4
        + [pl.BlockSpec((None, S5_GROUP, 1), m3)],
        out_specs=[pl.BlockSpec((None, k, k), m3), pl.BlockSpec((None, k, nl), m3),
                   pl.BlockSpec((None, k, nl), m3), pl.BlockSpec((None, 2, nl // 2), m3)],
        out_shape=[jax.ShapeDtypeStruct((g, k, k), BF16), jax.ShapeDtypeStruct((g, k, nl), BF16),
                   jax.ShapeDtypeStruct((g, k, nl), BF16), jax.ShapeDtypeStruct((g, 2, nl // 2), F32)],
        scratch_shapes=[pltpu.VMEM((k, nl), F32)],
        compiler_params=_cparams(("parallel",)),
        name="s5_prep",
    )(lam3, b4re, b4im, c4re, c4im, dcol)


def _s5_kernel(xl_ref, xc_ref, tt_ref, win_ref, wout_ref, a16_ref, y_ref,
               z_scr, zc_scr, sf_scr, sb_scr, *, nkl, nkc, bsz):
    h = 2 * S5_STATE
    xl = xl_ref[...]
    win = win_ref[...]
    z_scr[...] = _dot(xl, win)
    zc_scr[...] = _dot(xc_ref[...], win)
    is_f = lax.broadcasted_iota(I32, (bsz, h), 1) < S5_STATE
    ar = a16_ref[0:1, :]
    ai = a16_ref[1:2, :]

    def pick(z_ref, i, n):
        zf = z_ref[pl.ds(i * bsz, bsz), :]
        zb = z_ref[pl.ds((n - 1 - i) * bsz, bsz), :]
        return jnp.where(is_f, zf[:, :h], zb[:, :h]), jnp.where(is_f, zf[:, h:], zb[:, h:])

    def update(sr, si, zr, zi):
        return ar * sr - ai * si + zr, ar * si + ai * sr + zi

    def ctx_step(i, carry):
        zr, zi = pick(zc_scr, i, nkc)
        return update(*carry, zr, zi)

    def lat_step(i, carry):
        sr, si = carry
        st = jnp.concatenate([sr, si], axis=1)
        sf_scr[pl.ds(i * bsz, bsz), :] = st
        sb_scr[pl.ds((nkl - 1 - i) * bsz, bsz), :] = st
        zr, zi = pick(z_scr, i, nkl)
        return update(sr, si, zr, zi)

    zero = jnp.zeros((bsz, h), F32)
    carry = lax.fori_loop(0, nkc, ctx_step, (zero, zero))
    lax.fori_loop(0, nkl, lat_step, carry)

    is_f4 = (lax.broadcasted_iota(I32, (1, 2 * h), 1) % h) < S5_STATE
    xs = jnp.where(is_f4, sf_scr[...], sb_scr[...]).astype(BF16)
    y = _dot(xl, tt_ref[...]) + _dot_nt(xs, wout_ref[...])
    y_ref[...] = y.astype(BF16)


def _s5(xl, xc, tt, win, wout, a16, bsz):
    g, nl_rows, k = xl.shape
    nc_rows = xc.shape[1]
    nl = 4 * S5_STATE
    m3 = lambda i: (i, 0, 0)
    kern = functools.partial(_s5_kernel, nkl=nl_rows // bsz, nkc=nc_rows // bsz, bsz=bsz)
    return pl.pallas_call(
        kern,
        grid=(g,),
        in_specs=[pl.BlockSpec((None, nl_rows, k), m3), pl.BlockSpec((None, nc_rows, k), m3),
                  pl.BlockSpec((None, k, k), m3), pl.BlockSpec((None, k, nl), m3),
                  pl.BlockSpec((None, k, nl), m3), pl.BlockSpec((None, 2, nl // 2), m3)],
        out_specs=pl.BlockSpec((None, nl_rows, k), m3),
        out_shape=jax.ShapeDtypeStruct((g, nl_rows, k), BF16),
        scratch_shapes=[pltpu.VMEM((nl_rows, nl), F32), pltpu.VMEM((nc_rows, nl), F32),
                        pltpu.VMEM((nl_rows, nl), F32), pltpu.VMEM((nl_rows, nl), F32)],
        compiler_params=_cparams(("parallel",)),
        name="s5",
    )(xl, xc, tt, win, wout, a16)


def _lru_kernel(xv_ref, xg_ref, xc_ref, cw_ref, cb_ref, wg_ref, bg_ref, lam_ref, o_ref,
                a_scr, b_scr, ac_scr, bc_scr, cin_scr, *, rows, cl, slab_block):
    w = GRID_W
    c = xv_ref.shape[-1]
    cw = cw_ref[...]
    cb = cb_ref[...]
    sp = _softplus(-lam_ref[...])
    bg = bg_ref[...]
    wg = wg_ref[...]

    def coeffs(xs, store):
        g = _dot(xs.astype(BF16), wg) + bg
        for d in range(2):
            r = jax.nn.sigmoid(g[:, (2 * d) * c:(2 * d + 1) * c])
            i = jax.nn.sigmoid(g[:, (2 * d + 1) * c:(2 * d + 2) * c])
            log_a = -LRU_C * r * sp[d:d + 1, :]
            a = jnp.exp(log_a)
            b = jnp.sqrt(-jnp.tanh(log_a) * (a * a + 1.0)) * (i * xs)
            store(d, a, b)

    def tap(r0, r1, off):
        lo, hi = r0 + off, r1 + off
        clo, chi = max(lo, 0), min(hi, rows)
        parts = []
        if clo > lo:
            parts.append(jnp.zeros((clo - lo, w, c), F32))
        if chi > clo:
            parts.append(xv_ref[clo:chi].astype(F32))
        if hi > chi:
            parts.append(jnp.zeros((hi - chi, w, c), F32))
        return parts[0] if len(parts) == 1 else jnp.concatenate(parts, axis=0)

    for r0 in range(0, rows, slab_block):
        r1 = r0 + slab_block
        xs3 = cb
        for k in range(CONV_W):
            xs3 = xs3 + cw[k:k + 1, :] * tap(r0, r1, k - CONV_LEFT)
        xs = xs3.reshape(slab_block * w, c)

        def store_lat(d, a, b, r0=r0, r1=r1):
            a_scr[d, r0 * w:r1 * w, :] = a
            b_scr[d, r0 * w:r1 * w, :] = b

        coeffs(xs, store_lat)

    xc = xc_ref[...].astype(F32)
    trow = lax.broadcasted_iota(I32, (cl, c), 0)
    xcs = cb
    for k in range(CONV_W):
        off = k - CONV_LEFT
        sh = xc if off == 0 else pltpu.roll(xc, (-off) % cl, 0)
        ok = (trow + off >= 0) & (trow + off < cl)
        xcs = xcs + cw[k:k + 1, :] * jnp.where(ok, sh, 0.0)

    def store_ctx(d, a, b):
        ac_scr[d] = a
        bc_scr[d] = b

    coeffs(xcs, store_ctx)

    for d in range(2):
        rev = d == 1

        def ctx_step(j, h, d=d, rev=rev):
            idx = (cl - 1 - j) if rev else j
            return ac_scr[d, pl.ds(idx, 1), :] * h + bc_scr[d, pl.ds(idx, 1), :]

        h0 = lax.fori_loop(0, cl, ctx_step, jnp.zeros((1, c), F32))

        def col_step(j, carry, d=d, rev=rev):
            h, p = carry
            r = (rows - 1 - j) if rev else j
            off = pl.multiple_of(r * w, w)
            a = a_scr[d, pl.ds(off, w), :]
            h = a * h + b_scr[d, pl.ds(off, w), :]
            p = a * p
            b_scr[d, pl.ds(off, w), :] = h
            a_scr[d, pl.ds(off, w), :] = p
            return h, p

        lax.fori_loop(0, rows, col_step, (jnp.zeros((w, c), F32), jnp.ones((w, c), F32)))

        last = 0 if rev else (rows - 1) * w

        def carry_step(j, cin, d=d, rev=rev, last=last):
            col = (w - 1 - j) if rev else j
            cin_scr[pl.ds(col, 1), :] = cin
            return a_scr[d, pl.ds(last + col, 1), :] * cin + b_scr[d, pl.ds(last + col, 1), :]

        lax.fori_loop(0, w, carry_step, h0)
        cin = cin_scr[...]

        if not rev:
            def fix_step(r, _, d=d, cin=cin):
                off = pl.multiple_of(r * w, w)
                b_scr[d, pl.ds(off, w), :] = b_scr[d, pl.ds(off, w), :] + a_scr[d, pl.ds(off, w), :] * cin
                return 0

            lax.fori_loop(0, rows, fix_step, 0)
        else:
            def out_step(r, _, cin=cin):
                off = pl.multiple_of(r * w, w)
                hsum = (b_scr[0, pl.ds(off, w), :] + b_scr[1, pl.ds(off, w), :]
                        + a_scr[1, pl.ds(off, w), :] * cin)
                o_ref[r] = (hsum * _gelu(xg_ref[r].astype(F32))).astype(o_ref.dtype)
                return 0

            lax.fori_loop(0, rows, out_step, 0)


def _lru(u4, uc3, conv_w, conv_b, wg, bg, lam, c_blk=256):
    bsz, rows, w, n2 = u4.shape
    d_lru = n2 // 2
    cl = uc3.shape[1]
    ncb = d_lru // c_blk
    kern = functools.partial(_lru_kernel, rows=rows, cl=cl, slab_block=8)
    n = rows * w
    return pl.pallas_call(
        kern,
        grid=(bsz, ncb),
        in_specs=[pl.BlockSpec((None, rows, w, c_blk), lambda b, j: (b, 0, 0, j)),
                  pl.BlockSpec((None, rows, w, c_blk), lambda b, j: (b, 0, 0, ncb + j)),
                  pl.BlockSpec((None, cl, c_blk), lambda b, j: (b, 0, j)),
                  pl.BlockSpec((CONV_W, c_blk), lambda b, j: (0, j)),
                  pl.BlockSpec((1, c_blk), lambda b, j: (0, j)),
                  pl.BlockSpec((None, c_blk, 4 * c_blk), lambda b, j: (j, 0, 0)),
                  pl.BlockSpec((None, 1, 4 * c_blk), lambda b, j: (j, 0, 0)),
                  pl.BlockSpec((2, c_blk), lambda b, j: (0, j))],
        out_specs=pl.BlockSpec((None, rows, w, c_blk), lambda b, j: (b, 0, 0, j)),
        out_shape=jax.ShapeDtypeStruct((bsz, rows, w, d_lru), BF16),
        scratch_shapes=[pltpu.VMEM((2, n, c_blk), F32), pltpu.VMEM((2, n, c_blk), F32),
                        pltpu.VMEM((2, cl, c_blk), F32), pltpu.VMEM((2, cl, c_blk), F32),
                        pltpu.VMEM((w, c_blk), F32)],
        compiler_params=_cparams(("parallel", "parallel")),
        name="lru",
    )(u4, u4, uc3, conv_w, conv_b, wg, bg, lam)


def _mix_kernel(ys5_ref, ylru_ref, x_ref, lng_ref, lnb_ref, g1_ref, sc2_ref, sh2_ref,
                wglu_ref, bglu_ref, wo1_ref, wo2_ref, l1g_ref, l1b_ref, rwt_ref, rb_ref, tri_ref,
                h1_ref, v_ref, eidx_ref, rank_ref, wts_ref, tcnt_ref):
    y = _gelu(ys5_ref[...].astype(F32))
    s5o = y * jax.nn.sigmoid(_dot(y.astype(BF16), wglu_ref[...]) + bglu_ref[...])
    y1 = _dot(s5o.astype(BF16), wo1_ref[...]) + _dot(ylru_ref[...], wo2_ref[...])
    h = _ln(x_ref[...], lng_ref[...], lnb_ref[...])
    h1 = _ln(ALPHA * h + g1_ref[...] * y1, l1g_ref[...], l1b_ref[...])
    h1_ref[...] = h1
    v = h1 * (1.0 + sc2_ref[...]) + sh2_ref[...]
    v_ref[...] = v.astype(v_ref.dtype)

    tm = v.shape[0]
    ne, ng, gsz = N_EXPERTS, N_GROUPS, N_EXPERTS // N_GROUPS
    scores = jax.nn.sigmoid(_dot_nt(rwt_ref[...], v, HIGHEST))
    s3 = scores.reshape(ng, gsz, tm)
    sel3 = (scores + rb_ref[...]).reshape(ng, gsz, tm)
    ii = lax.broadcasted_iota(I32, (ng, gsz, tm), 1)
    gi = lax.broadcasted_iota(I32, (ng, gsz, tm), 0)
    neg = -jnp.inf

    m1 = jnp.max(sel3, axis=1, keepdims=True)
    f1 = jnp.min(jnp.where(sel3 == m1, ii, gsz), axis=1, keepdims=True)
    m2 = jnp.max(jnp.where(ii == f1, neg, sel3), axis=1, keepdims=True)
    cur = m1 + m2
    gidx = lax.broadcasted_iota(I32, (ng, 1, tm), 0)
    gmask = jnp.zeros((ng, 1, tm), jnp.bool_)
    for _ in range(TOPK_GROUPS):
        mx = jnp.max(cur, axis=0, keepdims=True)
        fg = jnp.min(jnp.where(cur == mx, gidx, ng), axis=0, keepdims=True)
        hit = gidx == fg
        gmask = gmask | hit
        cur = jnp.where(hit, neg, cur)

    selm = jnp.where(gmask, sel3, neg)
    eid = gi * gsz + ii
    picks, erows, wrows = [], [], []
    for _ in range(TOP_K):
        mx = jnp.max(jnp.max(selm, axis=1, keepdims=True), axis=0, keepdims=True)
        fe = jnp.min(jnp.min(jnp.where(selm == mx, eid, ne), axis=1, keepdims=True),
                     axis=0, keepdims=True)
        hit = eid == fe
        selm = jnp.where(hit, neg, selm)
        picks.append(hit)
        erows.append(fe.reshape(1, tm))
        wsel = jnp.where(hit, s3, 0.0)
        wrows.append(jnp.sum(jnp.sum(wsel, axis=1, keepdims=True), axis=0).reshape(1, tm))
    denom = wrows[0]
    for k in range(1, TOP_K):
        denom = denom + wrows[k]

    chosen = picks[0]
    for k in range(1, TOP_K):
        chosen = chosen | picks[k]
    chosen_f = jnp.where(chosen, 1.0, 0.0).reshape(ne, tm)
    cnt3 = _dot(chosen_f.astype(BF16), tri_ref[...]).reshape(ng, gsz, tm)
    rrows = []
    for k in range(TOP_K):
        rsel = jnp.where(picks[k], cnt3, 0.0)
        rrows.append(jnp.sum(jnp.sum(rsel, axis=1, keepdims=True), axis=0).reshape(1, tm))
    tcnt_ref[...] = jnp.concatenate(
        [jnp.sum(chosen_f[:, j * TOKEN_TILE:(j + 1) * TOKEN_TILE], axis=1, keepdims=True)
         for j in range(tm // TOKEN_TILE)], axis=1).astype(I32)

    eidx_ref[...] = jnp.concatenate(erows, axis=0)
    rank_ref[...] = jnp.concatenate(rrows, axis=0).astype(I32)
    wts_ref[...] = jnp.concatenate([wr / denom * ROUTED_SCALE for wr in wrows], axis=0)


def _mix(ys5, ylru, x2d, ln_g, ln_b, g1, sc2, sh2, wglu, bglu, wo1, wo2, l1g, l1b, rwt, rb, seq, tm=512):
    t, d = x2d.shape
    ds = ys5.shape[1]
    ne = rwt.shape[0]
    nsub = tm // TOKEN_TILE
    r_i = lax.broadcasted_iota(I32, (tm, tm), 0)
    c_i = lax.broadcasted_iota(I32, (tm, tm), 1)
    tri = ((r_i < c_i) & (r_i // TOKEN_TILE == c_i // TOKEN_TILE)).astype(BF16)
    row = lambda n: pl.BlockSpec((tm, n), lambda i: (i, 0))
    vec = lambda n: pl.BlockSpec((1, n), lambda i: (0, 0))
    mod = pl.BlockSpec((None, 1, d), lambda i: ((i * tm) // seq, 0, 0))
    full = lambda a, b: pl.BlockSpec((a, b), lambda i: (0, 0))
    tok = pl.BlockSpec((TOP_K, tm), lambda i: (0, i))
    return pl.pallas_call(
        _mix_kernel,
        grid=(t // tm,),
        in_specs=[row(ds), row(ds), row(d), vec(d), vec(d), mod, mod, mod,
                  full(ds, ds), vec(ds), full(ds, d), full(ds, d), vec(d), vec(d),
                  full(ne, d), full(ne, 1), full(tm, tm)],
        out_specs=[row(d), row(d), tok, tok, tok,
                   pl.BlockSpec((None, ne, nsub), lambda i: (i, 0, 0))],
        out_shape=[jax.ShapeDtypeStruct((t, d), F32), jax.ShapeDtypeStruct((t, d), BF16),
                   jax.ShapeDtypeStruct((TOP_K, t), I32), jax.ShapeDtypeStruct((TOP_K, t), I32),
                   jax.ShapeDtypeStruct((TOP_K, t), F32),
                   jax.ShapeDtypeStruct((t // tm, ne, nsub), I32)],
        compiler_params=_cparams(("parallel",)),
        name="mix",
    )(ys5, ylru, x2d, ln_g.reshape(1, d), ln_b.reshape(1, d), g1, sc2, sh2,
      wglu, bglu.reshape(1, ds), wo1, wo2, l1g.reshape(1, d), l1b.reshape(1, d), rwt, rb, tri)


def _plan_kernel(tc_ref, soff_ref, utot_ref, start_ref, total_ref, pieces_ref, npieces_ref):
    tc = tc_ref[...]
    ne, nt = tc.shape
    unit_shift = SEG_ALIGN.bit_length() - 1
    units = lax.shift_right_logical(tc + (SEG_ALIGN - 1), unit_shift)
    seg = (units * SEG_ALIGN).astype(F32)
    earlier_exp = jnp.where(lax.broadcasted_iota(I32, (ne, ne), 0) > lax.broadcasted_iota(I32, (ne, ne), 1),
                            1.0, 0.0)
    earlier_tile = jnp.where(lax.broadcasted_iota(I32, (nt, nt), 0) < lax.broadcasted_iota(I32, (nt, nt), 1),
                             1.0, 0.0)

    def exact(a, b):
        return jnp.dot(a, b, precision=HIGHEST, preferred_element_type=F32)

    soff = exact(earlier_exp, seg)
    total = jnp.sum(seg, axis=1, keepdims=True)
    start = jnp.sum(soff, axis=1, keepdims=True)
    dst = exact(seg, earlier_tile) + start
    soff_ref[...] = soff.astype(I32)
    utot_ref[...] = (jnp.sum(seg, axis=0, keepdims=True) * (1.0 / SEG_ALIGN)).astype(I32)
    start_ref[...] = start.astype(I32)
    total_ref[...] = total.astype(I32)

    counts = []
    for k in range(PIECE_SIZES):
        has = jnp.bitwise_and(lax.shift_right_logical(units, k), 1)
        below = (jnp.bitwise_and(units, (1 << k) - 1) * SEG_ALIGN).astype(F32)
        pos = exact(earlier_exp, has.astype(F32))
        counts.append(jnp.sum(has, axis=0, keepdims=True))
        rows = []
        for j in range(ne):
            sel = (has > 0) & (pos == float(j))
            srow = jnp.sum(jnp.where(sel, soff + below, 0.0), axis=0, keepdims=True).astype(I32)
            drow = jnp.sum(jnp.where(sel, dst + below, 0.0), axis=0, keepdims=True).astype(I32)
            rows.append(jnp.bitwise_or(lax.shift_left(srow, PIECE_PACK.bit_length() - 1), drow))
        pieces_ref[k * ne:(k + 1) * ne, :] = jnp.concatenate(rows, axis=0)
    counts += [jnp.zeros((1, nt), I32)] * (npieces_ref.shape[0] - PIECE_SIZES)
    npieces_ref[...] = jnp.concatenate(counts, axis=0)


def _plan(tc_t, n_rows):
    ne, nt = tc_t.shape
    assert n_rows <= PIECE_PACK
    full = lambda *s: pl.BlockSpec(s, lambda: (0,) * len(s))
    return pl.pallas_call(
        _plan_kernel,
        in_specs=[full(ne, nt)],
        out_specs=[full(ne, nt), full(1, nt), full(ne, 1), full(ne, 1), full(PIECE_SIZES * ne, nt),
                   full(8, nt)],
        out_shape=[jax.ShapeDtypeStruct((ne, nt), I32), jax.ShapeDtypeStruct((1, nt), I32),
                   jax.ShapeDtypeStruct((ne, 1), I32), jax.ShapeDtypeStruct((ne, 1), I32),
                   jax.ShapeDtypeStruct((PIECE_SIZES * ne, nt), I32), jax.ShapeDtypeStruct((8, nt), I32)],
        name="plan",
    )(tc_t)


def _visits_kernel(start_ref, total_ref, tail_ref, blk_ref, xblk_ref, exp_ref, lo_ref, hi_ref,
                   *, n_blocks, nv):
    ne = N_EXPERTS
    shift = ROW_BLOCK.bit_length() - 1

    def put(pos, blk, xblk, e, lo, hi):
        blk_ref[pos] = blk
        xblk_ref[pos] = xblk
        exp_ref[pos] = e
        lo_ref[pos] = lo
        hi_ref[pos] = hi

    def per_expert(e, carry):
        pos, done = carry
        off = start_ref[e]
        cnt = total_ref[e]
        first = lax.shift_right_logical(off, shift)
        last = lax.shift_right_logical(off + cnt - 1, shift)
        nvis = jnp.where(cnt > 0, last - first + 1, 0)

        def put_vis(k, c):
            put(pos + k, first + k, first + k, e, off, off + cnt)
            return c

        lax.fori_loop(0, nvis, put_vis, 0)
        return pos + nvis, jnp.where(cnt > 0, last + 1, done)

    pos, done = lax.fori_loop(0, ne, per_expert, (jnp.int32(0), jnp.int32(0)))
    tail_ref[0] = start_ref[ne - 1] + total_ref[ne - 1]

    def put_tail(k, c):
        put(pos + k, done + k, 0, ne - 1, 0, 0)
        return c

    lax.fori_loop(0, n_blocks - done, put_tail, 0)

    def put_rest(j, c):
        put(j, n_blocks - 1, 0, ne - 1, 0, 0)
        return c

    lax.fori_loop(pos + (n_blocks - done), nv, put_rest, 0)


def _sorted_rows(n_tokens):
    nt = n_tokens // TOKEN_TILE
    rows = n_tokens * TOP_K + nt * N_EXPERTS * (SEG_ALIGN - 1)
    return -(-rows // ROW_BLOCK) * ROW_BLOCK


def _visits(start, total, n_rows):
    ne = start.shape[0]
    n_blocks = n_rows // ROW_BLOCK
    nv = n_blocks + ne
    smem = pl.BlockSpec(memory_space=pltpu.SMEM)
    vec = jax.ShapeDtypeStruct((nv,), I32)
    return pl.pallas_call(
        functools.partial(_visits_kernel, n_blocks=n_blocks, nv=nv),
        in_specs=[smem, smem],
        out_specs=[smem] * 6,
        out_shape=[jax.ShapeDtypeStruct((1,), I32), vec, vec, vec, vec, vec],
        name="visits",
    )(start, total)


def _start_segment_copies(pieces_ref, npieces_ref, tile, hbm_ref, buf, to_hbm, sem):
    for k in range(PIECE_SIZES):
        size = SEG_ALIGN << k

        def body(j, c, k=k, size=size):
            packed = pieces_ref[k * N_EXPERTS + j, tile]
            srow = pl.multiple_of(lax.shift_right_logical(packed, PIECE_PACK.bit_length() - 1), SEG_ALIGN)
            drow = pl.multiple_of(jnp.bitwise_and(packed, PIECE_PACK - 1), SEG_ALIGN)
            slot_rows = buf.at[pl.ds(srow, size), :]
            sorted_rows = hbm_ref.at[pl.ds(drow, size), :]
            src, dst = (slot_rows, sorted_rows) if to_hbm else (sorted_rows, slot_rows)
            pltpu.make_async_copy(src, dst, sem).start()
            return c

        lax.fori_loop(0, npieces_ref[k, tile], body, 0)


def _wait_segment_copies(units, hbm_ref, buf, to_hbm, sem):
    for k in range(NSLOT.bit_length() - SEG_ALIGN.bit_length() + 1):
        @pl.when(jnp.bitwise_and(units, 1 << k) != 0)
        def _(k=k):
            size = SEG_ALIGN << k
            slot_rows = buf.at[pl.ds(0, size), :]
            sorted_rows = hbm_ref.at[pl.ds(0, size), :]
            src, dst = (slot_rows, sorted_rows) if to_hbm else (sorted_rows, slot_rows)
            pltpu.make_async_copy(src, dst, sem).wait()


def _dispatch_kernel(pieces_ref, npieces_ref, soff_ref, utot_ref, tail_ref, eidx_ref, lr_ref, v_ref,
                     xs_ref, srow_ref, cbuf, zbuf, sems, zsem):
    i = pl.program_id(0)
    nt = pl.num_programs(0)
    tm = v_ref.shape[0]
    s = i % 2

    @pl.when(i >= 2)
    def _():
        _wait_segment_copies(utot_ref[0, i - 2], xs_ref, cbuf.at[s], True, sems.at[s])

    e8 = eidx_ref[...]
    base = jnp.zeros_like(e8)
    for e in range(N_EXPERTS):
        base = jnp.where(e8 == e, soff_ref[e, i], base)
    tr = base + lr_ref[...]
    srow_ref[...] = tr

    vb = v_ref[...]
    for c0 in range(0, NSLOT, SLOT_CHUNK):
        rows = lax.broadcasted_iota(I32, (SLOT_CHUNK, tm), 0) + c0
        onehot = jnp.zeros((SLOT_CHUNK, tm), F32)
        for k in range(TOP_K):
            onehot = jnp.where(rows == tr[k:k + 1, :], 1.0, onehot)
        cbuf[s, c0:c0 + SLOT_CHUNK, :] = _dot(onehot.astype(BF16), vb).astype(BF16)
    _start_segment_copies(pieces_ref, npieces_ref, i, xs_ref, cbuf.at[s], True, sems.at[s])

    @pl.when(i == nt - 1)
    def _():
        @pl.when(i >= 1)
        def _():
            _wait_segment_copies(utot_ref[0, i - 1], xs_ref, cbuf.at[1 - s], True, sems.at[1 - s])

        _wait_segment_copies(utot_ref[0, i], xs_ref, cbuf.at[s], True, sems.at[s])
        zbuf[...] = jnp.zeros_like(zbuf)

        def zero_rows(start, size):
            rows = pl.ds(pl.multiple_of(start, SEG_ALIGN), size)
            return pltpu.make_async_copy(zbuf.at[pl.ds(0, size), :], xs_ref.at[rows, :], zsem)

        tail = tail_ref[0]
        n_small = lax.shift_right_logical(jnp.bitwise_and(-tail, ROW_BLOCK - 1),
                                          SEG_ALIGN.bit_length() - 1)
        tail_blk = tail + n_small * SEG_ALIGN
        n_big = lax.shift_right_logical(xs_ref.shape[0] - tail_blk, ROW_BLOCK.bit_length() - 1)

        def each(fn):
            lax.fori_loop(0, n_small, lambda q, c: fn(zero_rows(tail + q * SEG_ALIGN, SEG_ALIGN), c), 0)
            lax.fori_loop(0, n_big, lambda q, c: fn(zero_rows(tail_blk + q * ROW_BLOCK, ROW_BLOCK), c), 0)

        each(lambda cp, c: (cp.start(), c)[1])
        each(lambda cp, c: (cp.wait(), c)[1])


def _dispatch(pieces, npieces, soff, utot, tail, eidx, lrank, v, n_rows):
    t, d = v.shape
    tm = TOKEN_TILE
    tok = pl.BlockSpec((TOP_K, tm), lambda i, *_: (0, i))
    return pl.pallas_call(
        _dispatch_kernel,
        grid_spec=pltpu.PrefetchScalarGridSpec(
            num_scalar_prefetch=5, grid=(t // tm,),
            in_specs=[tok, tok, pl.BlockSpec((tm, d), lambda i, *_: (i, 0))],
            out_specs=[pl.BlockSpec(memory_space=pl.ANY), tok],
            scratch_shapes=[pltpu.VMEM((2, NSLOT, d), BF16), pltpu.VMEM((ROW_BLOCK, d), BF16),
                            pltpu.SemaphoreType.DMA((2,)), pltpu.SemaphoreType.DMA]),
        out_shape=[jax.ShapeDtypeStruct((n_rows, d), BF16), jax.ShapeDtypeStruct((TOP_K, t), I32)],
        compiler_params=_cparams(("arbitrary",)),
        name="dispatch",
    )(pieces, npieces, soff, utot, tail, eidx, lrank, v)


def _gmm_kernel(vb_ref, vx_ref, ve_ref, vlo_ref, vhi_ref, x_ref, wg_ref, wu_ref, wd_ref, y_ref,
                wg_b, wu_b, wd_b):
    j = pl.program_id(0)
    prev = jnp.maximum(j - 1, 0)
    first = jnp.logical_or(j == 0, vb_ref[prev] != vb_ref[j])
    live = vhi_ref[j] > vlo_ref[j]

    @pl.when(jnp.logical_or(j == 0, ve_ref[prev] != ve_ref[j]))
    def _():
        wg_b[...] = wg_ref[...].astype(BF16)
        wu_b[...] = wu_ref[...].astype(BF16)
        wd_b[...] = wd_ref[...].astype(BF16)

    @pl.when(live)
    def _():
        sub = ROW_BLOCK // GMM_SPLIT
        parts = []
        for r0 in range(0, ROW_BLOCK, sub):
            x = x_ref[r0:r0 + sub, :]
            hidden = _silu(_dot(x, wg_b[...])) * _dot(x, wu_b[...])
            y = _dot(hidden.astype(BF16), wd_b[...])
            rows = vb_ref[j] * ROW_BLOCK + r0 + lax.broadcasted_iota(I32, (sub, 1), 0)
            parts.append((r0, y, (rows >= vlo_ref[j]) & (rows < vhi_ref[j])))

        @pl.when(first)
        def _():
            for r0, y, mine in parts:
                y_ref[r0:r0 + sub, :] = jnp.where(mine, y, 0.0).astype(y_ref.dtype)

        @pl.when(jnp.logical_not(first))
        def _():
            for r0, y, mine in parts:
                keep = y_ref[r0:r0 + sub, :].astype(F32)
                y_ref[r0:r0 + sub, :] = jnp.where(mine, y, keep).astype(y_ref.dtype)

    @pl.when(jnp.logical_and(jnp.logical_not(live), first))
    def _():
        y_ref[...] = jnp.zeros_like(y_ref)


def _gmm(vblock, vxblock, vexp, vlo, vhi, xs, wg, wu, wd):
    n, d = xs.shape
    de = wg.shape[2]
    nv = vblock.shape[0]
    return pl.pallas_call(
        _gmm_kernel,
        grid_spec=pltpu.PrefetchScalarGridSpec(
            num_scalar_prefetch=5, grid=(nv,),
            in_specs=[pl.BlockSpec((ROW_BLOCK, d), lambda j, vb, vx, ve, lo, hi: (vx[j], 0)),
                      pl.BlockSpec((None, d, de), lambda j, vb, vx, ve, lo, hi: (ve[j], 0, 0)),
                      pl.BlockSpec((None, d, de), lambda j, vb, vx, ve, lo, hi: (ve[j], 0, 0)),
                      pl.BlockSpec((None, de, d), lambda j, vb, vx, ve, lo, hi: (ve[j], 0, 0))],
            out_specs=pl.BlockSpec((ROW_BLOCK, d), lambda j, vb, vx, ve, lo, hi: (vb[j], 0)),
            scratch_shapes=[pltpu.VMEM((d, de), BF16), pltpu.VMEM((d, de), BF16),
                            pltpu.VMEM((de, d), BF16)]),
        out_shape=jax.ShapeDtypeStruct((n, d), BF16),
        compiler_params=_cparams(("arbitrary",)),
        name="gmm",
    )(vblock, vxblock, vexp, vlo, vhi, xs, wg, wu, wd)


def _combine_kernel(pieces_ref, npieces_ref, utot_ref, srow_ref, w_ref, v_ref, h1_ref, g2_ref,
                    shg_ref, shu_ref, shd_ref, l2g_ref, l2b_ref, ys_ref, o_ref, ybuf, sems):
    i = pl.program_id(0)
    nt = pl.num_programs(0)
    tm, d = v_ref.shape
    cur = i % 2

    def fetch(tile, b):
        _start_segment_copies(pieces_ref, npieces_ref, tile, ys_ref, ybuf.at[b], False, sems.at[b])

    @pl.when(i == 0)
    def _():
        ybuf[...] = jnp.zeros_like(ybuf)
        fetch(0, 0)

    @pl.when(i + 1 < nt)
    def _():
        fetch(i + 1, 1 - cur)

    vb = v_ref[...]
    hidden = _silu(_dot(vb, shg_ref[...])) * _dot(vb, shu_ref[...])
    f = _dot(hidden.astype(BF16), shd_ref[...])

    tr = srow_ref[...]
    w8 = w_ref[...]
    _wait_segment_copies(utot_ref[0, i], ys_ref, ybuf.at[cur], False, sems.at[cur])
    for c0 in range(0, NSLOT, FOLD_CHUNK):
        lanes = lax.broadcasted_iota(I32, (tm, FOLD_CHUNK), 1) + c0
        pw = jnp.zeros((tm, FOLD_CHUNK), F32)
        for k in range(TOP_K):
            pw = jnp.where(lanes == tr[:, k:k + 1], w8[:, k:k + 1], pw)
        f = f + _dot(pw.astype(BF16), ybuf[cur, c0:c0 + FOLD_CHUNK, :])
    o_ref[...] = _ln(ALPHA * h1_ref[...] + g2_ref[...] * f, l2g_ref[...], l2b_ref[...])


def _combine(pieces, npieces, utot, srow_t, wts_t, v, h1, g2, shg, shu, shd, l2g, l2b, ys, seq):
    t, d = v.shape
    tm = TOKEN_TILE
    dsh = shg.shape[1]
    row = pl.BlockSpec((tm, d), lambda i, *_: (i, 0))
    vec = pl.BlockSpec((1, d), lambda i, *_: (0, 0))
    tok = pl.BlockSpec((tm, TOP_K), lambda i, *_: (i, 0))
    return pl.pallas_call(
        _combine_kernel,
        grid_spec=pltpu.PrefetchScalarGridSpec(
            num_scalar_prefetch=3, grid=(t // tm,),
            in_specs=[tok, tok, row, row,
                      pl.BlockSpec((None, 1, d), lambda i, *_: ((i * tm) // seq, 0, 0)),
                      pl.BlockSpec((d, dsh), lambda i, *_: (0, 0)),
                      pl.BlockSpec((d, dsh), lambda i, *_: (0, 0)),
                      pl.BlockSpec((dsh, d), lambda i, *_: (0, 0)), vec, vec,
                      pl.BlockSpec(memory_space=pl.ANY)],
            out_specs=row,
            scratch_shapes=[pltpu.VMEM((2, NSLOT, d), BF16), pltpu.SemaphoreType.DMA((2,))]),
        out_shape=jax.ShapeDtypeStruct((t, d), F32),
        compiler_params=_cparams(("arbitrary",)),
        name="combine",
    )(pieces, npieces, utot, srow_t, wts_t, v, h1, g2, shg, shu, shd,
      l2g.reshape(1, d), l2b.reshape(1, d), ys)


def _quad(a0, a1):
    return jnp.concatenate([a0, a1, a0, a1], axis=-1)


def kernel(x, c, ctx, c_ctx, ln_in_g, ln_in_b, ada_w, ada_b, w_in, s5_lam_re, s5_lam_im, s5_log_dt, s5_b_re, s5_b_im, s5_c_re, s5_c_im, s5_d, s5_w_glu, s5_b_glu, lru_conv_w, lru_conv_b, lru_w_a, lru_b_a, lru_w_x, lru_b_x, lru_lam, w_out, ln1_g, ln1_b, router_w, router_bias, exp_w_gate, exp_w_up, exp_w_down, sh_w_gate, sh_w_up, sh_w_down, ln2_g, ln2_b):
    bsz, seq, dm = x.shape
    cl = ctx.shape[1]
    assert ada_w.shape[0] == DEPTH
    d_s5 = s5_w_glu.shape[1]
    d_lru = lru_lam.shape[2]
    ngrp = d_s5 // S5_GROUP
    rows = seq // GRID_W
    t = bsz * seq
    lc = S5_CHUNK
    nkl, nkc = seq // lc, cl // lc

    pad = (-(bsz + 1)) % 8
    cc = jnp.concatenate([c, c_ctx[None, :], jnp.zeros((pad, dm), F32)], axis=0)
    mods = _ada(cc, ada_w[0], ada_b[0])
    sh1, sc1, g1, sh2, sc2, g2 = [mods[:bsz, k * dm:(k + 1) * dm].reshape(bsz, 1, dm) for k in range(6)]
    csh1, csc1 = [mods[bsz:bsz + 1, k * dm:(k + 1) * dm].reshape(1, 1, dm) for k in range(2)]

    x2d = x.reshape(t, dm)
    w_in_b = w_in[0].astype(BF16)
    us5, ulru = _in_proj(x2d, ln_in_g, ln_in_b, sc1, sh1, w_in_b, seq, d_s5)
    ucs5, uclru = _in_proj(ctx.reshape(bsz * cl, dm), ln_in_g, ln_in_b, csc1, csh1, w_in_b, bsz * cl, d_s5)

    def to_chunks(u, nk):
        u = u.reshape(bsz, nk, lc, ngrp, S5_GROUP).transpose(3, 1, 0, 2, 4)
        return u.reshape(ngrp, nk * bsz, lc * S5_GROUP)

    lam3 = jnp.stack([_quad(s5_lam_re[0, 0], s5_lam_re[0, 1]), _quad(s5_lam_im[0, 0], s5_lam_im[0, 1]),
                      _quad(*[jnp.broadcast_to(s5_log_dt[0, k][:, None], (ngrp, S5_STATE)) for k in range(2)])],
                     axis=1)
    bt_re = jnp.swapaxes(s5_b_re[0], -1, -2)
    bt_im = jnp.swapaxes(s5_b_im[0], -1, -2)
    tt, win, wout, a16 = _s5_prep(lam3, _quad(bt_re[0], bt_re[1]), _quad(bt_im[0], bt_im[1]),
                                  _quad(s5_c_re[0, 0], s5_c_re[0, 1]), _quad(s5_c_im[0, 0], s5_c_im[0, 1]),
                                  s5_d[0].reshape(ngrp, S5_GROUP, 1))
    ys = _s5(to_chunks(us5, nkl), to_chunks(ucs5, nkc), tt, win, wout, a16, bsz)
    ys5 = ys.reshape(ngrp, nkl, bsz, lc, S5_GROUP).transpose(2, 1, 3, 0, 4).reshape(t, d_s5)

    c_blk = 256
    hd = d_lru // LRU_HEADS
    hpb = c_blk // hd
    ncb = d_lru // c_blk

    def blockdiag(wh):
        wh = wh.reshape(ncb, hpb, hd, hd)
        eye = jnp.eye(hpb, dtype=wh.dtype)
        return jnp.einsum("nhij,hk->nhikj", wh, eye).reshape(ncb, c_blk, c_blk)

    wg = jnp.concatenate([blockdiag(lru_w_a[0, 0]), blockdiag(lru_w_x[0, 0]),
                          blockdiag(lru_w_a[0, 1]), blockdiag(lru_w_x[0, 1])], axis=-1).astype(BF16)
    bgate = jnp.concatenate([lru_b_a[0, 0].reshape(ncb, 1, c_blk), lru_b_x[0, 0].reshape(ncb, 1, c_blk),
                             lru_b_a[0, 1].reshape(ncb, 1, c_blk), lru_b_x[0, 1].reshape(ncb, 1, c_blk)], axis=-1)
    ylru = _lru(ulru.reshape(bsz, rows, GRID_W, 2 * d_lru), uclru.reshape(bsz, cl, 2 * d_lru),
                lru_conv_w[0], lru_conv_b[0].reshape(1, d_lru), wg, bgate, lru_lam[0], c_blk)
    ylru = ylru.reshape(t, d_lru)

    w_out_b = w_out[0].astype(BF16)
    h1, v, eidx, lrank, wts, tcnt = _mix(
        ys5, ylru, x2d, ln_in_g, ln_in_b, g1, sc2, sh2, s5_w_glu[0].astype(BF16), s5_b_glu[0],
        w_out_b[:d_s5], w_out_b[d_s5:], ln1_g[0], ln1_b[0], router_w[0].T,
        router_bias[0].reshape(N_EXPERTS, 1), seq)

    n_rows = _sorted_rows(t)
    tc_t = tcnt.transpose(1, 0, 2).reshape(N_EXPERTS, t // TOKEN_TILE)
    soff, utot, start, total, pieces, npieces = _plan(tc_t, n_rows)
    tail, vblock, vxblock, vexp, vlo, vhi = _visits(start.reshape(-1), total.reshape(-1), n_rows)
    xs, srow = _dispatch(pieces, npieces, soff, utot, tail, eidx, lrank, v, n_rows)
    ysort = _gmm(vblock, vxblock, vexp, vlo, vhi, xs, exp_w_gate[0], exp_w_up[0], exp_w_down[0])
    out = _combine(pieces, npieces, utot, srow.T, wts.T, v, h1, g2, sh_w_gate[0].astype(BF16),
                   sh_w_up[0].astype(BF16), sh_w_down[0].astype(BF16), ln2_g[0], ln2_b[0], ysort, seq)
    return out.reshape(bsz, seq, dm).astype(x.dtype)
```

```python
import functools
import math

import jax
import jax.numpy as jnp
from jax import lax
from jax.experimental import pallas as pl
from jax.experimental.pallas import tpu as pltpu

F32 = jnp.float32
BF16 = jnp.bfloat16
I32 = jnp.int32
HIGHEST = lax.Precision.HIGHEST

GRID_W = 64
S5_GROUP = 16
S5_STATE = 64
S5_CHUNK = 16
LRU_HEADS = 8
LRU_C = 8.0
CONV_W = 4
CONV_LEFT = 2
N_EXPERTS = 64
TOP_K = 8
N_GROUPS = 8
TOPK_GROUPS = 4
ROUTED_SCALE = 2.5
LN_EPS = 1e-5
DEPTH = 1
ALPHA = (2.0 * DEPTH) ** 0.25

ROW_BLOCK = 512
GMM_SPLIT = 1
TOKEN_TILE = 256
SEG_ALIGN = 16
NSLOT = TOKEN_TILE * TOP_K + N_EXPERTS * SEG_ALIGN
SLOT_CHUNK = NSLOT // 12
FOLD_CHUNK = NSLOT // 4
PIECE_SIZES = (TOKEN_TILE // SEG_ALIGN).bit_length()
PIECE_PACK = 1 << 19
VMEM_LIMIT = 52 * 1024 * 1024


def _cparams(sem):
    return pltpu.CompilerParams(dimension_semantics=sem, vmem_limit_bytes=VMEM_LIMIT)


def _ln(x, g, b):
    mu = jnp.mean(x, axis=-1, keepdims=True)
    xc = x - mu
    var = jnp.mean(xc * xc, axis=-1, keepdims=True)
    return xc * lax.rsqrt(var + LN_EPS) * g + b


def _gelu(x):
    return x * (0.5 * (1.0 + jnp.tanh(math.sqrt(2.0 / math.pi) * (x + 0.044715 * (x * x * x)))))


def _silu(x):
    return x * jax.nn.sigmoid(x)


def _softplus(x):
    return jnp.maximum(x, 0.0) + jnp.log1p(jnp.exp(-jnp.abs(x)))


def _dot(a, b):
    return jnp.dot(a, b, preferred_element_type=F32)


def _dot_nt(a, b, precision=None):
    return lax.dot_general(a, b, (((1,), (1,)), ((), ())), precision=precision,
                           preferred_element_type=F32)


def _ada_kernel(c_ref, w_ref, b_ref, o_ref):
    s = _silu(c_ref[...])
    o_ref[...] = jnp.dot(s, w_ref[...], precision=HIGHEST, preferred_element_type=F32) + b_ref[...]


def _ada(cc, ada_w, ada_b):
    r, d = cc.shape
    n = ada_w.shape[1]
    tn = 512
    return pl.pallas_call(
        _ada_kernel,
        grid=(n // tn,),
        in_specs=[pl.BlockSpec((r, d), lambda j: (0, 0)),
                  pl.BlockSpec((d, tn), lambda j: (0, j)),
                  pl.BlockSpec((1, tn), lambda j: (0, j))],
        out_specs=pl.BlockSpec((r, tn), lambda j: (0, j)),
        out_shape=jax.ShapeDtypeStruct((r, n), F32),
        compiler_params=_cparams(("parallel",)),
        name="ada",
    )(cc, ada_w, ada_b.reshape(1, n))


def _in_proj_kernel(x_ref, g_ref, b_ref, sc_ref, sh_ref, w_ref, us5_ref, ulru_ref):
    h = _ln(x_ref[...], g_ref[...], b_ref[...])
    m = h * (1.0 + sc_ref[...]) + sh_ref[...]
    u = _dot(m.astype(BF16), w_ref[...])
    ds5 = us5_ref.shape[-1]
    us5_ref[...] = u[:, :ds5].astype(BF16)
    ulru_ref[...] = u[:, ds5:].astype(BF16)


def _in_proj(x2d, ln_g, ln_b, sc, sh, w_bf16, rows_per_mod, d_s5, tm=512):
    t, d = x2d.shape
    n = w_bf16.shape[1]
    tm = min(tm, t)
    mod_spec = pl.BlockSpec((None, 1, d), lambda i: ((i * tm) // rows_per_mod, 0, 0))
    vec = pl.BlockSpec((1, d), lambda i: (0, 0))
    return pl.pallas_call(
        _in_proj_kernel,
        grid=(t // tm,),
        in_specs=[pl.BlockSpec((tm, d), lambda i: (i, 0)), vec, vec, mod_spec, mod_spec,
                  pl.BlockSpec((d, n), lambda i: (0, 0))],
        out_specs=[pl.BlockSpec((tm, d_s5), lambda i: (i, 0)),
                   pl.BlockSpec((tm, n - d_s5), lambda i: (i, 0))],
        out_shape=[jax.ShapeDtypeStruct((t, d_s5), BF16),
                   jax.ShapeDtypeStruct((t, n - d_s5), BF16)],
        compiler_params=_cparams(("parallel",)),
        name="in_proj",
    )(x2d, ln_g.reshape(1, d), ln_b.reshape(1, d), sc, sh, w_bf16)


def _s5_prep_kernel(lam_ref, bre_ref, bim_ref, cre_ref, cim_ref, d_ref,
                    tt_ref, win_ref, wout_ref, a16_ref, cp_scr):
    p, lc, gs = S5_STATE, S5_CHUNK, S5_GROUP
    nl = 4 * p
    lre = lam_ref[0:1, :]
    lim = lam_ref[1:2, :]
    dt = jnp.exp(lam_ref[2:3, :])
    lane = lax.broadcasted_iota(I32, (1, nl), 1)
    is_re = lane < 2 * p
    is_f = (lane % (2 * p)) < p

    def powers(ef, eb):
        e = jnp.where(is_f, float(ef), float(eb))
        mag = jnp.exp(lre * dt * e)
        ang = lim * dt * e
        return mag * jnp.cos(ang), mag * jnp.sin(ang)

    bre, bim = bre_ref[...], bim_ref[...]
    cre, cim = cre_ref[...], cim_ref[...]
    ar, ai = powers(1, 1)
    den = lre * lre + lim * lim
    nr = ar - 1.0
    fr = (nr * lre + ai * lim) / den
    fi = (ai * lre - nr * lim) / den
    bbre = fr * bre - fi * bim
    bbim = fr * bim + fi * bre

    def c_times(pc, ps):
        return cre * jnp.where(is_re, pc, -ps) + cim * jnp.where(is_re, -ps, -pc)

    for s in range(lc):
        pc, ps = powers(lc - 1 - s, s)
        qa = jnp.where(is_re, pc, ps)
        qb = jnp.where(is_re, -ps, pc)
        win_ref[gs * s:gs * (s + 1), :] = (bbre * qa + bbim * qb).astype(BF16)
    for t in range(lc):
        pc, ps = powers(t + 1, lc - t)
        wout_ref[gs * t:gs * (t + 1), :] = c_times(pc, ps).astype(BF16)
    for j in range(lc):
        pc, ps = powers(j, lc - 1 - j)
        cp_scr[gs * j:gs * (j + 1), :] = c_times(pc, ps)

    bbcat = jnp.where(is_re, bbre, bbim)
    cp = cp_scr[...]
    ktf = _dot_nt(jnp.where(is_f, bbcat, 0.0), cp, HIGHEST)
    ktb = _dot_nt(jnp.where(is_f, 0.0, bbcat), cp, HIGHEST)
    lane2 = lax.broadcasted_iota(I32, (gs, gs * lc), 1)
    row2 = lax.broadcasted_iota(I32, (gs, gs * lc), 0)
    dcol = d_ref[...]
    width = gs * lc
    for s in range(lc):
        sf = gs * s
        tf = ktf if sf == 0 else pltpu.roll(ktf, sf, 1)
        tf = jnp.where(lane2 >= sf, tf, 0.0)
        sb = (gs * (s + 1)) % width
        tb = ktb if sb == 0 else pltpu.roll(ktb, sb, 1)
        tb = jnp.where(lane2 < gs * (s + 1), tb, 0.0)
        skip = jnp.where(lane2 == sf + row2, dcol, 0.0)
        tt_ref[gs * s:gs * (s + 1), :] = (tf + tb + skip).astype(BF16)
    pc, ps = powers(lc, lc)
    a16_ref[0:1, :] = pc[:, :2 * p]
    a16_ref[1:2, :] = ps[:, :2 * p]


def _s5_prep(lam3, b4re, b4im, c4re, c4im, dcol):
    g = lam3.shape[0]
    nl = 4 * S5_STATE
    k = S5_GROUP * S5_CHUNK
    m3 = lambda i: (i, 0, 0)
    return pl.pallas_call(
        _s5_prep_kernel,
        grid=(g,),
        in_specs=[pl.BlockSpec((None, 3, nl), m3)]
        + [pl.BlockSpec((None, S5_GROUP, nl), m3)] * 4
        + [pl.BlockSpec((None, S5_GROUP, 1), m3)],
        out_specs=[pl.BlockSpec((None, k, k), m3), pl.BlockSpec((None, k, nl), m3),
                   pl.BlockSpec((None, k, nl), m3), pl.BlockSpec((None, 2, nl // 2), m3)],
        out_shape=[jax.ShapeDtypeStruct((g, k, k), BF16), jax.ShapeDtypeStruct((g, k, nl), BF16),
                   jax.ShapeDtypeStruct((g, k, nl), BF16), jax.ShapeDtypeStruct((g, 2, nl // 2), F32)],
        scratch_shapes=[pltpu.VMEM((k, nl), F32)],
        compiler_params=_cparams(("parallel",)),
        name="s5_prep",
    )(lam3, b4re, b4im, c4re, c4im, dcol)


def _s5_kernel(xl_ref, xc_ref, tt_ref, win_ref, wout_ref, a16_ref, y_ref,
               z_scr, zc_scr, sf_scr, sb_scr, *, nkl, nkc, bsz):
    h = 2 * S5_STATE
    xl = xl_ref[...]
    win = win_ref[...]
    z_scr[...] = _dot(xl, win)
    zc_scr[...] = _dot(xc_ref[...], win)
    is_f = lax.broadcasted_iota(I32, (bsz, h), 1) < S5_STATE
    ar = a16_ref[0:1, :]
    ai = a16_ref[1:2, :]

    def pick(z_ref, i, n):
        zf = z_ref[pl.ds(i * bsz, bsz), :]
        zb = z_ref[pl.ds((n - 1 - i) * bsz, bsz), :]
        return jnp.where(is_f, zf[:, :h], zb[:, :h]), jnp.where(is_f, zf[:, h:], zb[:, h:])

    def update(sr, si, zr, zi):
        return ar * sr - ai * si + zr, ar * si + ai * sr + zi

    def ctx_step(i, carry):
        zr, zi = pick(zc_scr, i, nkc)
        return update(*carry, zr, zi)

    def lat_step(i, carry):
        sr, si = carry
        st = jnp.concatenate([sr, si], axis=1)
        sf_scr[pl.ds(i * bsz, bsz), :] = st
        sb_scr[pl.ds((nkl - 1 - i) * bsz, bsz), :] = st
        zr, zi = pick(z_scr, i, nkl)
        return update(sr, si, zr, zi)

    zero = jnp.zeros((bsz, h), F32)
    carry = lax.fori_loop(0, nkc, ctx_step, (zero, zero))
    lax.fori_loop(0, nkl, lat_step, carry)

    is_f4 = (lax.broadcasted_iota(I32, (1, 2 * h), 1) % h) < S5_STATE
    xs = jnp.where(is_f4, sf_scr[...], sb_scr[...]).astype(BF16)
    y = _dot(xl, tt_ref[...]) + _dot_nt(xs, wout_ref[...])
    y_ref[...] = y.astype(BF16)


def _s5(xl, xc, tt, win, wout, a16, bsz):
    g, nl_rows, k = xl.shape
    nc_rows = xc.shape[1]
    nl = 4 * S5_STATE
    m3 = lambda i: (i, 0, 0)
    kern = functools.partial(_s5_kernel, nkl=nl_rows // bsz, nkc=nc_rows // bsz, bsz=bsz)
    return pl.pallas_call(
        kern,
        grid=(g,),
        in_specs=[pl.BlockSpec((None, nl_rows, k), m3), pl.BlockSpec((None, nc_rows, k), m3),
                  pl.BlockSpec((None, k, k), m3), pl.BlockSpec((None, k, nl), m3),
                  pl.BlockSpec((None, k, nl), m3), pl.BlockSpec((None, 2, nl // 2), m3)],
        out_specs=pl.BlockSpec((None, nl_rows, k), m3),
        out_shape=jax.ShapeDtypeStruct((g, nl_rows, k), BF16),
        scratch_shapes=[pltpu.VMEM((nl_rows, nl), F32), pltpu.VMEM((nc_rows, nl), F32),
                        pltpu.VMEM((nl_rows, nl), F32), pltpu.VMEM((nl_rows, nl), F32)],
        compiler_params=_cparams(("parallel",)),
        name="s5",
    )(xl, xc, tt, win, wout, a16)


def _lru_kernel(xv_ref, xg_ref, xc_ref, cw_ref, cb_ref, wg_ref, bg_ref, lam_ref, o_ref,
                a_scr, b_scr, ac_scr, bc_scr, cin_scr, *, rows, cl, slab_block):
    w = GRID_W
    c = xv_ref.shape[-1]
    cw = cw_ref[...]
    cb = cb_ref[...]
    sp = _softplus(-lam_ref[...])
    bg = bg_ref[...]
    wg = wg_ref[...]

    def coeffs(xs, store):
        g = _dot(xs.astype(BF16), wg) + bg
        for d in range(2):
            r = jax.nn.sigmoid(g[:, (2 * d) * c:(2 * d + 1) * c])
            i = jax.nn.sigmoid(g[:, (2 * d + 1) * c:(2 * d + 2) * c])
            log_a = -LRU_C * r * sp[d:d + 1, :]
            a = jnp.exp(log_a)
            b = jnp.sqrt(-jnp.tanh(log_a) * (a * a + 1.0)) * (i * xs)
            store(d, a, b)

    def tap(r0, r1, off):
        lo, hi = r0 + off, r1 + off
        clo, chi = max(lo, 0), min(hi, rows)
        parts = []
        if clo > lo:
            parts.append(jnp.zeros((clo - lo, w, c), F32))
        if chi > clo:
            parts.append(xv_ref[clo:chi].astype(F32))
        if hi > chi:
            parts.append(jnp.zeros((hi - chi, w, c), F32))
        return parts[0] if len(parts) == 1 else jnp.concatenate(parts, axis=0)

    for r0 in range(0, rows, slab_block):
        r1 = r0 + slab_block
        xs3 = cb
        for k in range(CONV_W):
            xs3 = xs3 + cw[k:k + 1, :] * tap(r0, r1, k - CONV_LEFT)
        xs = xs3.reshape(slab_block * w, c)

        def store_lat(d, a, b, r0=r0, r1=r1):
            a_scr[d, r0 * w:r1 * w, :] = a
            b_scr[d, r0 * w:r1 * w, :] = b

        coeffs(xs, store_lat)

    xc = xc_ref[...].astype(F32)
    trow = lax.broadcasted_iota(I32, (cl, c), 0)
    xcs = cb
    for k in range(CONV_W):
        off = k - CONV_LEFT
        sh = xc if off == 0 else pltpu.roll(xc, (-off) % cl, 0)
        ok = (trow + off >= 0) & (trow + off < cl)
        xcs = xcs + cw[k:k + 1, :] * jnp.where(ok, sh, 0.0)

    def store_ctx(d, a, b):
        ac_scr[d] = a
        bc_scr[d] = b

    coeffs(xcs, store_ctx)

    for d in range(2):
        rev = d == 1

        def ctx_step(j, h, d=d, rev=rev):
            idx = (cl - 1 - j) if rev else j
            return ac_scr[d, pl.ds(idx, 1), :] * h + bc_scr[d, pl.ds(idx, 1), :]

        h0 = lax.fori_loop(0, cl, ctx_step, jnp.zeros((1, c), F32))

        def col_step(j, carry, d=d, rev=rev):
            h, p = carry
            r = (rows - 1 - j) if rev else j
            off = pl.multiple_of(r * w, w)
            a = a_scr[d, pl.ds(off, w), :]
            h = a * h + b_scr[d, pl.ds(off, w), :]
            p = a * p
            b_scr[d, pl.ds(off, w), :] = h
            a_scr[d, pl.ds(off, w), :] = p
            return h, p

        lax.fori_loop(0, rows, col_step, (jnp.zeros((w, c), F32), jnp.ones((w, c), F32)))

        last = 0 if rev else (rows - 1) * w

        def carry_step(j, cin, d=d, rev=rev, last=last):
            col = (w - 1 - j) if rev else j
            cin_scr[pl.ds(col, 1), :] = cin
            return a_scr[d, pl.ds(last + col, 1), :] * cin + b_scr[d, pl.ds(last + col, 1), :]

        lax.fori_loop(0, w, carry_step, h0)
        cin = cin_scr[...]

        if not rev:
            def fix_step(r, _, d=d, cin=cin):
                off = pl.multiple_of(r * w, w)
                b_scr[d, pl.ds(off, w), :] = b_scr[d, pl.ds(off, w), :] + a_scr[d, pl.ds(off, w), :] * cin
                return 0

            lax.fori_loop(0, rows, fix_step, 0)
        else:
            def out_step(r, _, cin=cin):
                off = pl.multiple_of(r * w, w)
                hsum = (b_scr[0, pl.ds(off, w), :] + b_scr[1, pl.ds(off, w), :]
                        + a_scr[1, pl.ds(off, w), :] * cin)
                o_ref[r] = (hsum * _gelu(xg_ref[r].astype(F32))).astype(o_ref.dtype)
                return 0

            lax.fori_loop(0, rows, out_step, 0)


def _lru(u4, uc3, conv_w, conv_b, wg, bg, lam, c_blk=256):
    bsz, rows, w, n2 = u4.shape
    d_lru = n2 // 2
    cl = uc3.shape[1]
    ncb = d_lru // c_blk
    kern = functools.partial(_lru_kernel, rows=rows, cl=cl, slab_block=8)
    n = rows * w
    return pl.pallas_call(
        kern,
        grid=(bsz, ncb),
        in_specs=[pl.BlockSpec((None, rows, w, c_blk), lambda b, j: (b, 0, 0, j)),
                  pl.BlockSpec((None, rows, w, c_blk), lambda b, j: (b, 0, 0, ncb + j)),
                  pl.BlockSpec((None, cl, c_blk), lambda b, j: (b, 0, j)),
                  pl.BlockSpec((CONV_W, c_blk), lambda b, j: (0, j)),
                  pl.BlockSpec((1, c_blk), lambda b, j: (0, j)),
                  pl.BlockSpec((None, c_blk, 4 * c_blk), lambda b, j: (j, 0, 0)),
                  pl.BlockSpec((None, 1, 4 * c_blk), lambda b, j: (j, 0, 0)),
                  pl.BlockSpec((2, c_blk), lambda b, j: (0, j))],
        out_specs=pl.BlockSpec((None, rows, w, c_blk), lambda b, j: (b, 0, 0, j)),
        out_shape=jax.ShapeDtypeStruct((bsz, rows, w, d_lru), BF16),
        scratch_shapes=[pltpu.VMEM((2, n, c_blk), F32), pltpu.VMEM((2, n, c_blk), F32),
                        pltpu.VMEM((2, cl, c_blk), F32), pltpu.VMEM((2, cl, c_blk), F32),
                        pltpu.VMEM((w, c_blk), F32)],
        compiler_params=_cparams(("parallel", "parallel")),
        name="lru",
    )(u4, u4, uc3, conv_w, conv_b, wg, bg, lam)


def _mix_kernel(ys5_ref, ylru_ref, x_ref, lng_ref, lnb_ref, g1_ref, sc2_ref, sh2_ref,
                wglu_ref, bglu_ref, wo1_ref, wo2_ref, l1g_ref, l1b_ref, rwt_ref, rb_ref, tri_ref,
                h1_ref, v_ref, eidx_ref, rank_ref, wts_ref, tcnt_ref):
    y = _gelu(ys5_ref[...].astype(F32))
    s5o = y * jax.nn.sigmoid(_dot(y.astype(BF16), wglu_ref[...]) + bglu_ref[...])
    y1 = _dot(s5o.astype(BF16), wo1_ref[...]) + _dot(ylru_ref[...], wo2_ref[...])
    h = _ln(x_ref[...], lng_ref[...], lnb_ref[...])
    h1 = _ln(ALPHA * h + g1_ref[...] * y1, l1g_ref[...], l1b_ref[...])
    h1_ref[...] = h1
    v = h1 * (1.0 + sc2_ref[...]) + sh2_ref[...]
    v_ref[...] = v.astype(v_ref.dtype)

    tm = v.shape[0]
    ne, ng, gsz = N_EXPERTS, N_GROUPS, N_EXPERTS // N_GROUPS
    scores = jax.nn.sigmoid(_dot_nt(rwt_ref[...], v, HIGHEST))
    s3 = scores.reshape(ng, gsz, tm)
    sel3 = (scores + rb_ref[...]).reshape(ng, gsz, tm)
    ii = lax.broadcasted_iota(I32, (ng, gsz, tm), 1)
    gi = lax.broadcasted_iota(I32, (ng, gsz, tm), 0)
    neg = -jnp.inf

    m1 = jnp.max(sel3, axis=1, keepdims=True)
    f1 = jnp.min(jnp.where(sel3 == m1, ii, gsz), axis=1, keepdims=True)
    m2 = jnp.max(jnp.where(ii == f1, neg, sel3), axis=1, keepdims=True)
    cur = m1 + m2
    gidx = lax.broadcasted_iota(I32, (ng, 1, tm), 0)
    gmask = jnp.zeros((ng, 1, tm), jnp.bool_)
    for _ in range(TOPK_GROUPS):
        mx = jnp.max(cur, axis=0, keepdims=True)
        fg = jnp.min(jnp.where(cur == mx, gidx, ng), axis=0, keepdims=True)
        hit = gidx == fg
        gmask = gmask | hit
        cur = jnp.where(hit, neg, cur)

    selm = jnp.where(gmask, sel3, neg)
    eid = gi * gsz + ii
    picks, erows, wrows = [], [], []
    for _ in range(TOP_K):
        mx = jnp.max(jnp.max(selm, axis=1, keepdims=True), axis=0, keepdims=True)
        fe = jnp.min(jnp.min(jnp.where(selm == mx, eid, ne), axis=1, keepdims=True),
                     axis=0, keepdims=True)
        hit = eid == fe
        selm = jnp.where(hit, neg, selm)
        picks.append(hit)
        erows.append(fe.reshape(1, tm))
        wsel = jnp.where(hit, s3, 0.0)
        wrows.append(jnp.sum(jnp.sum(wsel, axis=1, keepdims=True), axis=0).reshape(1, tm))
    denom = wrows[0]
    for k in range(1, TOP_K):
        denom = denom + wrows[k]

    chosen = picks[0]
    for k in range(1, TOP_K):
        chosen = chosen | picks[k]
    chosen_f = jnp.where(chosen, 1.0, 0.0).reshape(ne, tm)
    cnt3 = _dot(chosen_f.astype(BF16), tri_ref[...]).reshape(ng, gsz, tm)
    rrows = []
    for k in range(TOP_K):
        rsel = jnp.where(picks[k], cnt3, 0.0)
        rrows.append(jnp.sum(jnp.sum(rsel, axis=1, keepdims=True), axis=0).reshape(1, tm))
    tcnt_ref[...] = jnp.concatenate(
        [jnp.sum(chosen_f[:, j * TOKEN_TILE:(j + 1) * TOKEN_TILE], axis=1, keepdims=True)
         for j in range(tm // TOKEN_TILE)], axis=1).astype(I32)

    eidx_ref[...] = jnp.concatenate(erows, axis=0)
    rank_ref[...] = jnp.concatenate(rrows, axis=0).astype(I32)
    wts_ref[...] = jnp.concatenate([wr / denom * ROUTED_SCALE for wr in wrows], axis=0)


def _mix(ys5, ylru, x2d, ln_g, ln_b, g1, sc2, sh2, wglu, bglu, wo1, wo2, l1g, l1b, rwt, rb, seq, tm=512):
    t, d = x2d.shape
    ds = ys5.shape[1]
    ne = rwt.shape[0]
    nsub = tm // TOKEN_TILE
    r_i = lax.broadcasted_iota(I32, (tm, tm), 0)
    c_i = lax.broadcasted_iota(I32, (tm, tm), 1)
    tri = ((r_i < c_i) & (r_i // TOKEN_TILE == c_i // TOKEN_TILE)).astype(BF16)
    row = lambda n: pl.BlockSpec((tm, n), lambda i: (i, 0))
    vec = lambda n: pl.BlockSpec((1, n), lambda i: (0, 0))
    mod = pl.BlockSpec((None, 1, d), lambda i: ((i * tm) // seq, 0, 0))
    full = lambda a, b: pl.BlockSpec((a, b), lambda i: (0, 0))
    tok = pl.BlockSpec((TOP_K, tm), lambda i: (0, i))
    return pl.pallas_call(
        _mix_kernel,
        grid=(t // tm,),
        in_specs=[row(ds), row(ds), row(d), vec(d), vec(d), mod, mod, mod,
                  full(ds, ds), vec(ds), full(ds, d), full(ds, d), vec(d), vec(d),
                  full(ne, d), full(ne, 1), full(tm, tm)],
        out_specs=[row(d), row(d), tok, tok, tok,
                   pl.BlockSpec((None, ne, nsub), lambda i: (i, 0, 0))],
        out_shape=[jax.ShapeDtypeStruct((t, d), F32), jax.ShapeDtypeStruct((t, d), BF16),
                   jax.ShapeDtypeStruct((TOP_K, t), I32), jax.ShapeDtypeStruct((TOP_K, t), I32),
                   jax.ShapeDtypeStruct((TOP_K, t), F32),
                   jax.ShapeDtypeStruct((t // tm, ne, nsub), I32)],
        compiler_params=_cparams(("parallel",)),
        name="mix",
    )(ys5, ylru, x2d, ln_g.reshape(1, d), ln_b.reshape(1, d), g1, sc2, sh2,
      wglu, bglu.reshape(1, ds), wo1, wo2, l1g.reshape(1, d), l1b.reshape(1, d), rwt, rb, tri)


def _plan_kernel(tc_ref, soff_ref, utot_ref, start_ref, total_ref, pieces_ref, npieces_ref):
    tc = tc_ref[...]
    ne, nt = tc.shape
    unit_shift = SEG_ALIGN.bit_length() - 1
    units = lax.shift_right_logical(tc + (SEG_ALIGN - 1), unit_shift)
    seg = (units * SEG_ALIGN).astype(F32)
    earlier_exp = jnp.where(lax.broadcasted_iota(I32, (ne, ne), 0) > lax.broadcasted_iota(I32, (ne, ne), 1),
                            1.0, 0.0)
    earlier_tile = jnp.where(lax.broadcasted_iota(I32, (nt, nt), 0) < lax.broadcasted_iota(I32, (nt, nt), 1),
                             1.0, 0.0)

    def exact(a, b):
        return jnp.dot(a, b, precision=HIGHEST, preferred_element_type=F32)

    soff = exact(earlier_exp, seg)
    total = jnp.sum(seg, axis=1, keepdims=True)
    start = jnp.sum(soff, axis=1, keepdims=True)
    dst = exact(seg, earlier_tile) + start
    soff_ref[...] = soff.astype(I32)
    utot_ref[...] = (jnp.sum(seg, axis=0, keepdims=True) * (1.0 / SEG_ALIGN)).astype(I32)
    start_ref[...] = start.astype(I32)
    total_ref[...] = total.astype(I32)

    counts = []
    for k in range(PIECE_SIZES):
        has = jnp.bitwise_and(lax.shift_right_logical(units, k), 1)
        below = (jnp.bitwise_and(units, (1 << k) - 1) * SEG_ALIGN).astype(F32)
        pos = exact(earlier_exp, has.astype(F32))
        counts.append(jnp.sum(has, axis=0, keepdims=True))
        rows = []
        for j in range(ne):
            sel = (has > 0) & (pos == float(j))
            srow = jnp.sum(jnp.where(sel, soff + below, 0.0), axis=0, keepdims=True).astype(I32)
            drow = jnp.sum(jnp.where(sel, dst + below, 0.0), axis=0, keepdims=True).astype(I32)
            rows.append(jnp.bitwise_or(lax.shift_left(srow, PIECE_PACK.bit_length() - 1), drow))
        pieces_ref[k * ne:(k + 1) * ne, :] = jnp.concatenate(rows, axis=0)
    counts += [jnp.zeros((1, nt), I32)] * (npieces_ref.shape[0] - PIECE_SIZES)
    npieces_ref[...] = jnp.concatenate(counts, axis=0)


def _plan(tc_t, n_rows):
    ne, nt = tc_t.shape
    assert n_rows <= PIECE_PACK
    full = lambda *s: pl.BlockSpec(s, lambda: (0,) * len(s))
    return pl.pallas_call(
        _plan_kernel,
        in_specs=[full(ne, nt)],
        out_specs=[full(ne, nt), full(1, nt), full(ne, 1), full(ne, 1), full(PIECE_SIZES * ne, nt),
                   full(8, nt)],
        out_shape=[jax.ShapeDtypeStruct((ne, nt), I32), jax.ShapeDtypeStruct((1, nt), I32),
                   jax.ShapeDtypeStruct((ne, 1), I32), jax.ShapeDtypeStruct((ne, 1), I32),
                   jax.ShapeDtypeStruct((PIECE_SIZES * ne, nt), I32), jax.ShapeDtypeStruct((8, nt), I32)],
        name="plan",
    )(tc_t)


def _visits_kernel(start_ref, total_ref, tail_ref, blk_ref, xblk_ref, exp_ref, lo_ref, hi_ref,
                   *, n_blocks, nv):
    ne = N_EXPERTS
    shift = ROW_BLOCK.bit_length() - 1

    def put(pos, blk, xblk, e, lo, hi):
        blk_ref[pos] = blk
        xblk_ref[pos] = xblk
        exp_ref[pos] = e
        lo_ref[pos] = lo
        hi_ref[pos] = hi

    def per_expert(e, carry):
        pos, done = carry
        off = start_ref[e]
        cnt = total_ref[e]
        first = lax.shift_right_logical(off, shift)
        last = lax.shift_right_logical(off + cnt - 1, shift)
        nvis = jnp.where(cnt > 0, last - first + 1, 0)

        def put_vis(k, c):
            put(pos + k, first + k, first + k, e, off, off + cnt)
            return c

        lax.fori_loop(0, nvis, put_vis, 0)
        return pos + nvis, jnp.where(cnt > 0, last + 1, done)

    pos, done = lax.fori_loop(0, ne, per_expert, (jnp.int32(0), jnp.int32(0)))
    tail_ref[0] = start_ref[ne - 1] + total_ref[ne - 1]

    def put_tail(k, c):
        put(pos + k, done + k, 0, ne - 1, 0, 0)
        return c

    lax.fori_loop(0, n_blocks - done, put_tail, 0)

    def put_rest(j, c):
        put(j, n_blocks - 1, 0, ne - 1, 0, 0)
        return c

    lax.fori_loop(pos + (n_blocks - done), nv, put_rest, 0)


def _sorted_rows(n_tokens):
    nt = n_tokens // TOKEN_TILE
    rows = n_tokens * TOP_K + nt * N_EXPERTS * (SEG_ALIGN - 1)
    return -(-rows // ROW_BLOCK) * ROW_BLOCK


def _visits(start, total, n_rows):
    ne = start.shape[0]
    n_blocks = n_rows // ROW_BLOCK
    nv = n_blocks + ne
    smem = pl.BlockSpec(memory_space=pltpu.SMEM)
    vec = jax.ShapeDtypeStruct((nv,), I32)
    return pl.pallas_call(
        functools.partial(_visits_kernel, n_blocks=n_blocks, nv=nv),
        in_specs=[smem, smem],
        out_specs=[smem] * 6,
        out_shape=[jax.ShapeDtypeStruct((1,), I32), vec, vec, vec, vec, vec],
        name="visits",
    )(start, total)


def _start_segment_copies(pieces_ref, npieces_ref, tile, hbm_ref, buf, to_hbm, sem):
    for k in range(PIECE_SIZES):
        size = SEG_ALIGN << k

        def body(j, c, k=k, size=size):
            packed = pieces_ref[k * N_EXPERTS + j, tile]
            srow = pl.multiple_of(lax.shift_right_logical(packed, PIECE_PACK.bit_length() - 1), SEG_ALIGN)
            drow = pl.multiple_of(jnp.bitwise_and(packed, PIECE_PACK - 1), SEG_ALIGN)
            slot_rows = buf.at[pl.ds(srow, size), :]
            sorted_rows = hbm_ref.at[pl.ds(drow, size), :]
            src, dst = (slot_rows, sorted_rows) if to_hbm else (sorted_rows, slot_rows)
            pltpu.make_async_copy(src, dst, sem).start()
            return c

        lax.fori_loop(0, npieces_ref[k, tile], body, 0)


def _wait_segment_copies(units, hbm_ref, buf, to_hbm, sem):
    for k in range(NSLOT.bit_length() - SEG_ALIGN.bit_length() + 1):
        @pl.when(jnp.bitwise_and(units, 1 << k) != 0)
        def _(k=k):
            size = SEG_ALIGN << k
            slot_rows = buf.at[pl.ds(0, size), :]
            sorted_rows = hbm_ref.at[pl.ds(0, size), :]
            src, dst = (slot_rows, sorted_rows) if to_hbm else (sorted_rows, slot_rows)
            pltpu.make_async_copy(src, dst, sem).wait()


def _dispatch_kernel(pieces_ref, npieces_ref, soff_ref, utot_ref, tail_ref, eidx_ref, lr_ref, v_ref,
                     xs_ref, srow_ref, cbuf, zbuf, sems, zsem):
    i = pl.program_id(0)
    nt = pl.num_programs(0)
    tm = v_ref.shape[0]
    s = i % 2

    @pl.when(i >= 2)
    def _():
        _wait_segment_copies(utot_ref[0, i - 2], xs_ref, cbuf.at[s], True, sems.at[s])

    e8 = eidx_ref[...]
    base = jnp.zeros_like(e8)
    for e in range(N_EXPERTS):
        base = jnp.where(e8 == e, soff_ref[e, i], base)
    tr = base + lr_ref[...]
    srow_ref[...] = tr

    vb = v_ref[...]
    for c0 in range(0, NSLOT, SLOT_CHUNK):
        rows = lax.broadcasted_iota(I32, (SLOT_CHUNK, tm), 0) + c0
        onehot = jnp.zeros((SLOT_CHUNK, tm), F32)
        for k in range(TOP_K):
            onehot = jnp.where(rows == tr[k:k + 1, :], 1.0, onehot)
        cbuf[s, c0:c0 + SLOT_CHUNK, :] = _dot(onehot.astype(BF16), vb).astype(BF16)
    _start_segment_copies(pieces_ref, npieces_ref, i, xs_ref, cbuf.at[s], True, sems.at[s])

    @pl.when(i == nt - 1)
    def _():
        @pl.when(i >= 1)
        def _():
            _wait_segment_copies(utot_ref[0, i - 1], xs_ref, cbuf.at[1 - s], True, sems.at[1 - s])

        _wait_segment_copies(utot_ref[0, i], xs_ref, cbuf.at[s], True, sems.at[s])
        zbuf[...] = jnp.zeros_like(zbuf)

        def zero_rows(start, size):
            rows = pl.ds(pl.multiple_of(start, SEG_ALIGN), size)
            return pltpu.make_async_copy(zbuf.at[pl.ds(0, size), :], xs_ref.at[rows, :], zsem)

        tail = tail_ref[0]
        n_small = lax.shift_right_logical(jnp.bitwise_and(-tail, ROW_BLOCK - 1),
                                          SEG_ALIGN.bit_length() - 1)
        tail_blk = tail + n_small * SEG_ALIGN
        n_big = lax.shift_right_logical(xs_ref.shape[0] - tail_blk, ROW_BLOCK.bit_length() - 1)

        def each(fn):
            lax.fori_loop(0, n_small, lambda q, c: fn(zero_rows(tail + q * SEG_ALIGN, SEG_ALIGN), c), 0)
            lax.fori_loop(0, n_big, lambda q, c: fn(zero_rows(tail_blk + q * ROW_BLOCK, ROW_BLOCK), c), 0)

        each(lambda cp, c: (cp.start(), c)[1])
        each(lambda cp, c: (cp.wait(), c)[1])


def _dispatch(pieces, npieces, soff, utot, tail, eidx, lrank, v, n_rows):
    t, d = v.shape
    tm = TOKEN_TILE
    tok = pl.BlockSpec((TOP_K, tm), lambda i, *_: (0, i))
    return pl.pallas_call(
        _dispatch_kernel,
        grid_spec=pltpu.PrefetchScalarGridSpec(
            num_scalar_prefetch=5, grid=(t // tm,),
            in_specs=[tok, tok, pl.BlockSpec((tm, d), lambda i, *_: (i, 0))],
            out_specs=[pl.BlockSpec(memory_space=pl.ANY), tok],
            scratch_shapes=[pltpu.VMEM((2, NSLOT, d), BF16), pltpu.VMEM((ROW_BLOCK, d), BF16),
                            pltpu.SemaphoreType.DMA((2,)), pltpu.SemaphoreType.DMA]),
        out_shape=[jax.ShapeDtypeStruct((n_rows, d), BF16), jax.ShapeDtypeStruct((TOP_K, t), I32)],
        compiler_params=_cparams(("arbitrary",)),
        name="dispatch",
    )(pieces, npieces, soff, utot, tail, eidx, lrank, v)


def _gmm_kernel(vb_ref, vx_ref, ve_ref, vlo_ref, vhi_ref, x_ref, wg_ref, wu_ref, wd_ref, y_ref,
                wg_b, wu_b, wd_b):
    j = pl.program_id(0)
    prev = jnp.maximum(j - 1, 0)
    first = jnp.logical_or(j == 0, vb_ref[prev] != vb_ref[j])
    live = vhi_ref[j] > vlo_ref[j]

    @pl.when(jnp.logical_or(j == 0, ve_ref[prev] != ve_ref[j]))
    def _():
        wg_b[...] = wg_ref[...].astype(BF16)
        wu_b[...] = wu_ref[...].astype(BF16)
        wd_b[...] = wd_ref[...].astype(BF16)

    @pl.when(live)
    def _():
        sub = ROW_BLOCK // GMM_SPLIT
        parts = []
        for r0 in range(0, ROW_BLOCK, sub):
            x = x_ref[r0:r0 + sub, :]
            hidden = _silu(_dot(x, wg_b[...])) * _dot(x, wu_b[...])
            y = _dot(hidden.astype(BF16), wd_b[...])
            rows = vb_ref[j] * ROW_BLOCK + r0 + lax.broadcasted_iota(I32, (sub, 1), 0)
            parts.append((r0, y, (rows >= vlo_ref[j]) & (rows < vhi_ref[j])))

        @pl.when(first)
        def _():
            for r0, y, mine in parts:
                y_ref[r0:r0 + sub, :] = jnp.where(mine, y, 0.0).astype(y_ref.dtype)

        @pl.when(jnp.logical_not(first))
        def _():
            for r0, y, mine in parts:
                keep = y_ref[r0:r0 + sub, :].astype(F32)
                y_ref[r0:r0 + sub, :] = jnp.where(mine, y, keep).astype(y_ref.dtype)

    @pl.when(jnp.logical_and(jnp.logical_not(live), first))
    def _():
        y_ref[...] = jnp.zeros_like(y_ref)


def _gmm(vblock, vxblock, vexp, vlo, vhi, xs, wg, wu, wd):
    n, d = xs.shape
    de = wg.shape[2]
    nv = vblock.shape[0]
    return pl.pallas_call(
        _gmm_kernel,
        grid_spec=pltpu.PrefetchScalarGridSpec(
            num_scalar_prefetch=5, grid=(nv,),
            in_specs=[pl.BlockSpec((ROW_BLOCK, d), lambda j, vb, vx, ve, lo, hi: (vx[j], 0)),
                      pl.BlockSpec((None, d, de), lambda j, vb, vx, ve, lo, hi: (ve[j], 0, 0)),
                      pl.BlockSpec((None, d, de), lambda j, vb, vx, ve, lo, hi: (ve[j], 0, 0)),
                      pl.BlockSpec((None, de, d), lambda j, vb, vx, ve, lo, hi: (ve[j], 0, 0))],
            out_specs=pl.BlockSpec((ROW_BLOCK, d), lambda j, vb, vx, ve, lo, hi: (vb[j], 0)),
            scratch_shapes=[pltpu.VMEM((d, de), BF16), pltpu.VMEM((d, de), BF16),
                            pltpu.VMEM((de, d), BF16)]),
        out_shape=jax.ShapeDtypeStruct((n, d), BF16),
        compiler_params=_cparams(("arbitrary",)),
        name="gmm",
    )(vblock, vxblock, vexp, vlo, vhi, xs, wg, wu, wd)


def _combine_kernel(pieces_ref, npieces_ref, utot_ref, srow_ref, w_ref, v_ref, h1_ref, g2_ref,
                    shg_ref, shu_ref, shd_ref, l2g_ref, l2b_ref, ys_ref, o_ref, ybuf, sems):
    i = pl.program_id(0)
    nt = pl.num_programs(0)
    tm, d = v_ref.shape
    cur = i % 2

    def fetch(tile, b):
        _start_segment_copies(pieces_ref, npieces_ref, tile, ys_ref, ybuf.at[b], False, sems.at[b])

    @pl.when(i == 0)
    def _():
        ybuf[...] = jnp.zeros_like(ybuf)
        fetch(0, 0)

    @pl.when(i + 1 < nt)
    def _():
        fetch(i + 1, 1 - cur)

    vb = v_ref[...]
    hidden = _silu(_dot(vb, shg_ref[...])) * _dot(vb, shu_ref[...])
    f = _dot(hidden.astype(BF16), shd_ref[...])

    tr = srow_ref[...]
    w8 = w_ref[...]
    _wait_segment_copies(utot_ref[0, i], ys_ref, ybuf.at[cur], False, sems.at[cur])
    for c0 in range(0, NSLOT, FOLD_CHUNK):
        lanes = lax.broadcasted_iota(I32, (tm, FOLD_CHUNK), 1) + c0
        pw = jnp.zeros((tm, FOLD_CHUNK), F32)
        for k in range(TOP_K):
            pw = jnp.where(lanes == tr[:, k:k + 1], w8[:, k:k + 1], pw)
        f = f + _dot(pw.astype(BF16), ybuf[cur, c0:c0 + FOLD_CHUNK, :])
    o_ref[...] = _ln(ALPHA * h1_ref[...] + g2_ref[...] * f, l2g_ref[...], l2b_ref[...])


def _combine(pieces, npieces, utot, srow_t, wts_t, v, h1, g2, shg, shu, shd, l2g, l2b, ys, seq):
    t, d = v.shape
    tm = TOKEN_TILE
    dsh = shg.shape[1]
    row = pl.BlockSpec((tm, d), lambda i, *_: (i, 0))
    vec = pl.BlockSpec((1, d), lambda i, *_: (0, 0))
    tok = pl.BlockSpec((tm, TOP_K), lambda i, *_: (i, 0))
    return pl.pallas_call(
        _combine_kernel,
        grid_spec=pltpu.PrefetchScalarGridSpec(
            num_scalar_prefetch=3, grid=(t // tm,),
            in_specs=[tok, tok, row, row,
                      pl.BlockSpec((None, 1, d), lambda i, *_: ((i * tm) // seq, 0, 0)),
                      pl.BlockSpec((d, dsh), lambda i, *_: (0, 0)),
                      pl.BlockSpec((d, dsh), lambda i, *_: (0, 0)),
                      pl.BlockSpec((dsh, d), lambda i, *_: (0, 0)), vec, vec,
                      pl.BlockSpec(memory_space=pl.ANY)],
            out_specs=row,
            scratch_shapes=[pltpu.VMEM((2, NSLOT, d), BF16), pltpu.SemaphoreType.DMA((2,))]),
        out_shape=jax.ShapeDtypeStruct((t, d), F32),
        compiler_params=_cparams(("arbitrary",)),
        name="combine",
    )(pieces, npieces, utot, srow_t, wts_t, v, h1, g2, shg, shu, shd,
      l2g.reshape(1, d), l2b.reshape(1, d), ys)


def _quad(a0, a1):
    return jnp.concatenate([a0, a1, a0, a1], axis=-1)


def kernel(x, c, ctx, c_ctx, ln_in_g, ln_in_b, ada_w, ada_b, w_in, s5_lam_re, s5_lam_im, s5_log_dt, s5_b_re, s5_b_im, s5_c_re, s5_c_im, s5_d, s5_w_glu, s5_b_glu, lru_conv_w, lru_conv_b, lru_w_a, lru_b_a, lru_w_x, lru_b_x, lru_lam, w_out, ln1_g, ln1_b, router_w, router_bias, exp_w_gate, exp_w_up, exp_w_down, sh_w_gate, sh_w_up, sh_w_down, ln2_g, ln2_b):
    bsz, seq, dm = x.shape
    cl = ctx.shape[1]
    assert ada_w.shape[0] == DEPTH
    d_s5 = s5_w_glu.shape[1]
    d_lru = lru_lam.shape[2]
    ngrp = d_s5 // S5_GROUP
    rows = seq // GRID_W
    t = bsz * seq
    lc = S5_CHUNK
    nkl, nkc = seq // lc, cl // lc

    pad = (-(bsz + 1)) % 8
    cc = jnp.concatenate([c, c_ctx[None, :], jnp.zeros((pad, dm), F32)], axis=0)
    mods = _ada(cc, ada_w[0], ada_b[0])
    sh1, sc1, g1, sh2, sc2, g2 = [mods[:bsz, k * dm:(k + 1) * dm].reshape(bsz, 1, dm) for k in range(6)]
    csh1, csc1 = [mods[bsz:bsz + 1, k * dm:(k + 1) * dm].reshape(1, 1, dm) for k in range(2)]

    x2d = x.reshape(t, dm)
    w_in_b = w_in[0].astype(BF16)
    us5, ulru = _in_proj(x2d, ln_in_g, ln_in_b, sc1, sh1, w_in_b, seq, d_s5)
    ucs5, uclru = _in_proj(ctx.reshape(bsz * cl, dm), ln_in_g, ln_in_b, csc1, csh1, w_in_b, bsz * cl, d_s5)

    def to_chunks(u, nk):
        u = u.reshape(bsz, nk, lc, ngrp, S5_GROUP).transpose(3, 1, 0, 2, 4)
        return u.reshape(ngrp, nk * bsz, lc * S5_GROUP)

    lam3 = jnp.stack([_quad(s5_lam_re[0, 0], s5_lam_re[0, 1]), _quad(s5_lam_im[0, 0], s5_lam_im[0, 1]),
                      _quad(*[jnp.broadcast_to(s5_log_dt[0, k][:, None], (ngrp, S5_STATE)) for k in range(2)])],
                     axis=1)
    bt_re = jnp.swapaxes(s5_b_re[0], -1, -2)
    bt_im = jnp.swapaxes(s5_b_im[0], -1, -2)
    tt, win, wout, a16 = _s5_prep(lam3, _quad(bt_re[0], bt_re[1]), _quad(bt_im[0], bt_im[1]),
                                  _quad(s5_c_re[0, 0], s5_c_re[0, 1]), _quad(s5_c_im[0, 0], s5_c_im[0, 1]),
                                  s5_d[0].reshape(ngrp, S5_GROUP, 1))
    ys = _s5(to_chunks(us5, nkl), to_chunks(ucs5, nkc), tt, win, wout, a16, bsz)
    ys5 = ys.reshape(ngrp, nkl, bsz, lc, S5_GROUP).transpose(2, 1, 3, 0, 4).reshape(t, d_s5)

    c_blk = 256
    hd = d_lru // LRU_HEADS
    hpb = c_blk // hd
    ncb = d_lru // c_blk

    def blockdiag(wh):
        wh = wh.reshape(ncb, hpb, hd, hd)
        eye = jnp.eye(hpb, dtype=wh.dtype)
        return jnp.einsum("nhij,hk->nhikj", wh, eye).reshape(ncb, c_blk, c_blk)

    wg = jnp.concatenate([blockdiag(lru_w_a[0, 0]), blockdiag(lru_w_x[0, 0]),
                          blockdiag(lru_w_a[0, 1]), blockdiag(lru_w_x[0, 1])], axis=-1).astype(BF16)
    bgate = jnp.concatenate([lru_b_a[0, 0].reshape(ncb, 1, c_blk), lru_b_x[0, 0].reshape(ncb, 1, c_blk),
                             lru_b_a[0, 1].reshape(ncb, 1, c_blk), lru_b_x[0, 1].reshape(ncb, 1, c_blk)], axis=-1)
    ylru = _lru(ulru.reshape(bsz, rows, GRID_W, 2 * d_lru), uclru.reshape(bsz, cl, 2 * d_lru),
                lru_conv_w[0], lru_conv_b[0].reshape(1, d_lru), wg, bgate, lru_lam[0], c_blk)
    ylru = ylru.reshape(t, d_lru)

    w_out_b = w_out[0].astype(BF16)
    h1, v, eidx, lrank, wts, tcnt = _mix(
        ys5, ylru, x2d, ln_in_g, ln_in_b, g1, sc2, sh2, s5_w_glu[0].astype(BF16), s5_b_glu[0],
        w_out_b[:d_s5], w_out_b[d_s5:], ln1_g[0], ln1_b[0], router_w[0].T,
        router_bias[0].reshape(N_EXPERTS, 1), seq)

    n_rows = _sorted_rows(t)
    tc_t = tcnt.transpose(1, 0, 2).reshape(N_EXPERTS, t // TOKEN_TILE)
    soff, utot, start, total, pieces, npieces = _plan(tc_t, n_rows)
    tail, vblock, vxblock, vexp, vlo, vhi = _visits(start.reshape(-1), total.reshape(-1), n_rows)
    xs, srow = _dispatch(pieces, npieces, soff, utot, tail, eidx, lrank, v, n_rows)
    ysort = _gmm(vblock, vxblock, vexp, vlo, vhi, xs, exp_w_gate[0], exp_w_up[0], exp_w_down[0])
    out = _combine(pieces, npieces, utot, srow.T, wts.T, v, h1, g2, sh_w_gate[0].astype(BF16),
                   sh_w_up[0].astype(BF16), sh_w_down[0].astype(BF16), ln2_g[0], ln2_b[0], ysort, seq)
    return out.reshape(bsz, seq, dm).astype(x.dtype)
```

```python
import functools
import math

import jax
import jax.numpy as jnp
from jax import lax
from jax.experimental import pallas as pl
from jax.experimental.pallas import tpu as pltpu

F32 = jnp.float32
BF16 = jnp.bfloat16
I32 = jnp.int32
HIGHEST = lax.Precision.HIGHEST

GRID_W = 64
S5_GROUP = 16
S5_STATE = 64
S5_CHUNK = 16
LRU_HEADS = 8
LRU_C = 8.0
CONV_W = 4
CONV_LEFT = 2
N_EXPERTS = 64
TOP_K = 8
N_GROUPS = 8
TOPK_GROUPS = 4
ROUTED_SCALE = 2.5
LN_EPS = 1e-5
DEPTH = 1
ALPHA = (2.0 * DEPTH) ** 0.25

ROW_BLOCK = 512
GMM_SPLIT = 1
TOKEN_TILE = 256
SEG_ALIGN = 16
NSLOT = TOKEN_TILE * TOP_K + N_EXPERTS * SEG_ALIGN
SLOT_CHUNK = NSLOT // 12
FOLD_CHUNK = NSLOT // 4
PIECE_SIZES = (TOKEN_TILE // SEG_ALIGN).bit_length()
PIECE_PACK = 1 << 19
VMEM_LIMIT = 52 * 1024 * 1024


def _cparams(sem):
    return pltpu.CompilerParams(dimension_semantics=sem, vmem_limit_bytes=VMEM_LIMIT)


def _ln(x, g, b):
    mu = jnp.mean(x, axis=-1, keepdims=True)
    xc = x - mu
    var = jnp.mean(xc * xc, axis=-1, keepdims=True)
    return xc * lax.rsqrt(var + LN_EPS) * g + b


def _gelu(x):
    return x * (0.5 * (1.0 + jnp.tanh(math.sqrt(2.0 / math.pi) * (x + 0.044715 * (x * x * x)))))


def _silu(x):
    return x * jax.nn.sigmoid(x)


def _softplus(x):
    return jnp.maximum(x, 0.0) + jnp.log1p(jnp.exp(-jnp.abs(x)))


def _dot(a, b):
    return jnp.dot(a, b, preferred_element_type=F32)


def _dot_nt(a, b, precision=None):
    return lax.dot_general(a, b, (((1,), (1,)), ((), ())), precision=precision,
                           preferred_element_type=F32)


def _ada_kernel(c_ref, w_ref, b_ref, o_ref):
    s = _silu(c_ref[...])
    o_ref[...] = jnp.dot(s, w_ref[...], precision=HIGHEST, preferred_element_type=F32) + b_ref[...]


def _ada(cc, ada_w, ada_b):
    r, d = cc.shape
    n = ada_w.shape[1]
    tn = 512
    return pl.pallas_call(
        _ada_kernel,
        grid=(n // tn,),
        in_specs=[pl.BlockSpec((r, d), lambda j: (0, 0)),
                  pl.BlockSpec((d, tn), lambda j: (0, j)),
                  pl.BlockSpec((1, tn), lambda j: (0, j))],
        out_specs=pl.BlockSpec((r, tn), lambda j: (0, j)),
        out_shape=jax.ShapeDtypeStruct((r, n), F32),
        compiler_params=_cparams(("parallel",)),
        name="ada",
    )(cc, ada_w, ada_b.reshape(1, n))


def _in_proj_kernel(x_ref, g_ref, b_ref, sc_ref, sh_ref, w_ref, us5_ref, ulru_ref):
    h = _ln(x_ref[...], g_ref[...], b_ref[...])
    m = h * (1.0 + sc_ref[...]) + sh_ref[...]
    u = _dot(m.astype(BF16), w_ref[...])
    ds5 = us5_ref.shape[-1]
    us5_ref[...] = u[:, :ds5].astype(BF16)
    ulru_ref[...] = u[:, ds5:].astype(BF16)


def _in_proj(x2d, ln_g, ln_b, sc, sh, w_bf16, rows_per_mod, d_s5, tm=512):
    t, d = x2d.shape
    n = w_bf16.shape[1]
    tm = min(tm, t)
    mod_spec = pl.BlockSpec((None, 1, d), lambda i: ((i * tm) // rows_per_mod, 0, 0))
    vec = pl.BlockSpec((1, d), lambda i: (0, 0))
    return pl.pallas_call(
        _in_proj_kernel,
        grid=(t // tm,),
        in_specs=[pl.BlockSpec((tm, d), lambda i: (i, 0)), vec, vec, mod_spec, mod_spec,
                  pl.BlockSpec((d, n), lambda i: (0, 0))],
        out_specs=[pl.BlockSpec((tm, d_s5), lambda i: (i, 0)),
                   pl.BlockSpec((tm, n - d_s5), lambda i: (i, 0))],
        out_shape=[jax.ShapeDtypeStruct((t, d_s5), BF16),
                   jax.ShapeDtypeStruct((t, n - d_s5), BF16)],
        compiler_params=_cparams(("parallel",)),
        name="in_proj",
    )(x2d, ln_g.reshape(1, d), ln_b.reshape(1, d), sc, sh, w_bf16)


def _s5_prep_kernel(lam_ref, bre_ref, bim_ref, cre_ref, cim_ref, d_ref,
                    tt_ref, win_ref, wout_ref, a16_ref, cp_scr):
    p, lc, gs = S5_STATE, S5_CHUNK, S5_GROUP
    nl = 4 * p
    lre = lam_ref[0:1, :]
    lim = lam_ref[1:2, :]
    dt = jnp.exp(lam_ref[2:3, :])
    lane = lax.broadcasted_iota(I32, (1, nl), 1)
    is_re = lane < 2 * p
    is_f = (lane % (2 * p)) < p

    def powers(ef, eb):
        e = jnp.where(is_f, float(ef), float(eb))
        mag = jnp.exp(lre * dt * e)
        ang = lim * dt * e
        return mag * jnp.cos(ang), mag * jnp.sin(ang)

    bre, bim = bre_ref[...], bim_ref[...]
    cre, cim = cre_ref[...], cim_ref[...]
    ar, ai = powers(1, 1)
    den = lre * lre + lim * lim
    nr = ar - 1.0
    fr = (nr * lre + ai * lim) / den
    fi = (ai * lre - nr * lim) / den
    bbre = fr * bre - fi * bim
    bbim = fr * bim + fi * bre

    def c_times(pc, ps):
        return cre * jnp.where(is_re, pc, -ps) + cim * jnp.where(is_re, -ps, -pc)

    for s in range(lc):
        pc, ps = powers(lc - 1 - s, s)
        qa = jnp.where(is_re, pc, ps)
        qb = jnp.where(is_re, -ps, pc)
        win_ref[gs * s:gs * (s + 1), :] = (bbre * qa + bbim * qb).astype(BF16)
    for t in range(lc):
        pc, ps = powers(t + 1, lc - t)
        wout_ref[gs * t:gs * (t + 1), :] = c_times(pc, ps).astype(BF16)
    for j in range(lc):
        pc, ps = powers(j, lc - 1 - j)
        cp_scr[gs * j:gs * (j + 1), :] = c_times(pc, ps)

    bbcat = jnp.where(is_re, bbre, bbim)
    cp = cp_scr[...]
    ktf = _dot_nt(jnp.where(is_f, bbcat, 0.0), cp, HIGHEST)
    ktb = _dot_nt(jnp.where(is_f, 0.0, bbcat), cp, HIGHEST)
    lane2 = lax.broadcasted_iota(I32, (gs, gs * lc), 1)
    row2 = lax.broadcasted_iota(I32, (gs, gs * lc), 0)
    dcol = d_ref[...]
    width = gs * lc
    for s in range(lc):
        sf = gs * s
        tf = ktf if sf == 0 else pltpu.roll(ktf, sf, 1)
        tf = jnp.where(lane2 >= sf, tf, 0.0)
        sb = (gs * (s + 1)) % width
        tb = ktb if sb == 0 else pltpu.roll(ktb, sb, 1)
        tb = jnp.where(lane2 < gs * (s + 1), tb, 0.0)
        skip = jnp.where(lane2 == sf + row2, dcol, 0.0)
        tt_ref[gs * s:gs * (s + 1), :] = (tf + tb + skip).astype(BF16)
    pc, ps = powers(lc, lc)
    a16_ref[0:1, :] = pc[:, :2 * p]
    a16_ref[1:2, :] = ps[:, :2 * p]


def _s5_prep(lam3, b4re, b4im, c4re, c4im, dcol):
    g = lam3.shape[0]
    nl = 4 * S5_STATE
    k = S5_GROUP * S5_CHUNK
    m3 = lambda i: (i, 0, 0)
    return pl.pallas_call(
        _s5_prep_kernel,
        grid=(g,),
        in_specs=[pl.BlockSpec((None, 3, nl), m3)]
        + [pl.BlockSpec((None, S5_GROUP, nl), m3)] * 4
        + [pl.BlockSpec((None, S5_GROUP, 1), m3)],
        out_specs=[pl.BlockSpec((None, k, k), m3), pl.BlockSpec((None, k, nl), m3),
                   pl.BlockSpec((None, k, nl), m3), pl.BlockSpec((None, 2, nl // 2), m3)],
        out_shape=[jax.ShapeDtypeStruct((g, k, k), BF16), jax.ShapeDtypeStruct((g, k, nl), BF16),
                   jax.ShapeDtypeStruct((g, k, nl), BF16), jax.ShapeDtypeStruct((g, 2, nl // 2), F32)],
        scratch_shapes=[pltpu.VMEM((k, nl), F32)],
        compiler_params=_cparams(("parallel",)),
        name="s5_prep",
    )(lam3, b4re, b4im, c4re, c4im, dcol)


def _s5_kernel(xl_ref, xc_ref, tt_ref, win_ref, wout_ref, a16_ref, y_ref,
               z_scr, zc_scr, sf_scr, sb_scr, *, nkl, nkc, bsz):
    h = 2 * S5_STATE
    xl = xl_ref[...]
    win = win_ref[...]
    z_scr[...] = _dot(xl, win)
    zc_scr[...] = _dot(xc_ref[...], win)
    is_f = lax.broadcasted_iota(I32, (bsz, h), 1) < S5_STATE
    ar = a16_ref[0:1, :]
    ai = a16_ref[1:2, :]

    def pick(z_ref, i, n):
        zf = z_ref[pl.ds(i * bsz, bsz), :]
        zb = z_ref[pl.ds((n - 1 - i) * bsz, bsz), :]
        return jnp.where(is_f, zf[:, :h], zb[:, :h]), jnp.where(is_f, zf[:, h:], zb[:, h:])

    def update(sr, si, zr, zi):
        return ar * sr - ai * si + zr, ar * si + ai * sr + zi

    def ctx_step(i, carry):
        zr, zi = pick(zc_scr, i, nkc)
        return update(*carry, zr, zi)

    def lat_step(i, carry):
        sr, si = carry
        st = jnp.concatenate([sr, si], axis=1)
        sf_scr[pl.ds(i * bsz, bsz), :] = st
        sb_scr[pl.ds((nkl - 1 - i) * bsz, bsz), :] = st
        zr, zi = pick(z_scr, i, nkl)
        return update(sr, si, zr, zi)

    zero = jnp.zeros((bsz, h), F32)
    carry = lax.fori_loop(0, nkc, ctx_step, (zero, zero))
    lax.fori_loop(0, nkl, lat_step, carry)

    is_f4 = (lax.broadcasted_iota(I32, (1, 2 * h), 1) % h) < S5_STATE
    xs = jnp.where(is_f4, sf_scr[...], sb_scr[...]).astype(BF16)
    y = _dot(xl, tt_ref[...]) + _dot_nt(xs, wout_ref[...])
    y_ref[...] = y.astype(BF16)


def _s5(xl, xc, tt, win, wout, a16, bsz):
    g, nl_rows, k = xl.shape
    nc_rows = xc.shape[1]
    nl = 4 * S5_STATE
    m3 = lambda i: (i, 0, 0)
    kern = functools.partial(_s5_kernel, nkl=nl_rows // bsz, nkc=nc_rows // bsz, bsz=bsz)
    return pl.pallas_call(
        kern,
        grid=(g,),
        in_specs=[pl.BlockSpec((None, nl_rows, k), m3), pl.BlockSpec((None, nc_rows, k), m3),
                  pl.BlockSpec((None, k, k), m3), pl.BlockSpec((None, k, nl), m3),
                  pl.BlockSpec((None, k, nl), m3), pl.BlockSpec((None, 2, nl // 2), m3)],
        out_specs=pl.BlockSpec((None, nl_rows, k), m3),
        out_shape=jax.ShapeDtypeStruct((g, nl_rows, k), BF16),
        scratch_shapes=[pltpu.VMEM((nl_rows, nl), F32), pltpu.VMEM((nc_rows, nl), F32),
                        pltpu.VMEM((nl_rows, nl), F32), pltpu.VMEM((nl_rows, nl), F32)],
        compiler_params=_cparams(("parallel",)),
        name="s5",
    )(xl, xc, tt, win, wout, a16)


def _lru_kernel(xv_ref, xg_ref, xc_ref, cw_ref, cb_ref, wg_ref, bg_ref, lam_ref, o_ref,
                a_scr, b_scr, ac_scr, bc_scr, cin_scr, *, rows, cl, slab_block):
    w = GRID_W
    c = xv_ref.shape[-1]
    cw = cw_ref[...]
    cb = cb_ref[...]
    sp = _softplus(-lam_ref[...])
    bg = bg_ref[...]
    wg = wg_ref[...]

    def coeffs(xs, store):
        g = _dot(xs.astype(BF16), wg) + bg
        for d in range(2):
            r = jax.nn.sigmoid(g[:, (2 * d) * c:(2 * d + 1) * c])
            i = jax.nn.sigmoid(g[:, (2 * d + 1) * c:(2 * d + 2) * c])
            log_a = -LRU_C * r * sp[d:d + 1, :]
            a = jnp.exp(log_a)
            b = jnp.sqrt(-jnp.tanh(log_a) * (a * a + 1.0)) * (i * xs)
            store(d, a, b)

    def tap(r0, r1, off):
        lo, hi = r0 + off, r1 + off
        clo, chi = max(lo, 0), min(hi, rows)
        parts = []
        if clo > lo:
            parts.append(jnp.zeros((clo - lo, w, c), F32))
        if chi > clo:
            parts.append(xv_ref[clo:chi].astype(F32))
        if hi > chi:
            parts.append(jnp.zeros((hi - chi, w, c), F32))
        return parts[0] if len(parts) == 1 else jnp.concatenate(parts, axis=0)

    for r0 in range(0, rows, slab_block):
        r1 = r0 + slab_block
        xs3 = cb
        for k in range(CONV_W):
            xs3 = xs3 + cw[k:k + 1, :] * tap(r0, r1, k - CONV_LEFT)
        xs = xs3.reshape(slab_block * w, c)

        def store_lat(d, a, b, r0=r0, r1=r1):
            a_scr[d, r0 * w:r1 * w, :] = a
            b_scr[d, r0 * w:r1 * w, :] = b

        coeffs(xs, store_lat)

    xc = xc_ref[...].astype(F32)
    trow = lax.broadcasted_iota(I32, (cl, c), 0)
    xcs = cb
    for k in range(CONV_W):
        off = k - CONV_LEFT
        sh = xc if off == 0 else pltpu.roll(xc, (-off) % cl, 0)
        ok = (trow + off >= 0) & (trow + off < cl)
        xcs = xcs + cw[k:k + 1, :] * jnp.where(ok, sh, 0.0)

    def store_ctx(d, a, b):
        ac_scr[d] = a
        bc_scr[d] = b

    coeffs(xcs, store_ctx)

    for d in range(2):
        rev = d == 1

        def ctx_step(j, h, d=d, rev=rev):
            idx = (cl - 1 - j) if rev else j
            return ac_scr[d, pl.ds(idx, 1), :] * h + bc_scr[d, pl.ds(idx, 1), :]

        h0 = lax.fori_loop(0, cl, ctx_step, jnp.zeros((1, c), F32))

        def col_step(j, carry, d=d, rev=rev):
            h, p = carry
            r = (rows - 1 - j) if rev else j
            off = pl.multiple_of(r * w, w)
            a = a_scr[d, pl.ds(off, w), :]
            h = a * h + b_scr[d, pl.ds(off, w), :]
            p = a * p
            b_scr[d, pl.ds(off, w), :] = h
            a_scr[d, pl.ds(off, w), :] = p
            return h, p

        lax.fori_loop(0, rows, col_step, (jnp.zeros((w, c), F32), jnp.ones((w, c), F32)))

        last = 0 if rev else (rows - 1) * w

        def carry_step(j, cin, d=d, rev=rev, last=last):
            col = (w - 1 - j) if rev else j
            cin_scr[pl.ds(col, 1), :] = cin
            return a_scr[d, pl.ds(last + col, 1), :] * cin + b_scr[d, pl.ds(last + col, 1), :]

        lax.fori_loop(0, w, carry_step, h0)
        cin = cin_scr[...]

        if not rev:
            def fix_step(r, _, d=d, cin=cin):
                off = pl.multiple_of(r * w, w)
                b_scr[d, pl.ds(off, w), :] = b_scr[d, pl.ds(off, w), :] + a_scr[d, pl.ds(off, w), :] * cin
                return 0

            lax.fori_loop(0, rows, fix_step, 0)
        else:
            def out_step(r, _, cin=cin):
                off = pl.multiple_of(r * w, w)
                hsum = (b_scr[0, pl.ds(off, w), :] + b_scr[1, pl.ds(off, w), :]
                        + a_scr[1, pl.ds(off, w), :] * cin)
                o_ref[r] = (hsum * _gelu(xg_ref[r].astype(F32))).astype(o_ref.dtype)
                return 0

            lax.fori_loop(0, rows, out_step, 0)


def _lru(u4, uc3, conv_w, conv_b, wg, bg, lam, c_blk=256):
    bsz, rows, w, n2 = u4.shape
    d_lru = n2 // 2
    cl = uc3.shape[1]
    ncb = d_lru // c_blk
    kern = functools.partial(_lru_kernel, rows=rows, cl=cl, slab_block=8)
    n = rows * w
    return pl.pallas_call(
        kern,
        grid=(bsz, ncb),
        in_specs=[pl.BlockSpec((None, rows, w, c_blk), lambda b, j: (b, 0, 0, j)),
                  pl.BlockSpec((None, rows, w, c_blk), lambda b, j: (b, 0, 0, ncb + j)),
                  pl.BlockSpec((None, cl, c_blk), lambda b, j: (b, 0, j)),
                  pl.BlockSpec((CONV_W, c_blk), lambda b, j: (0, j)),
                  pl.BlockSpec((1, c_blk), lambda b, j: (0, j)),
                  pl.BlockSpec((None, c_blk, 4 * c_blk), lambda b, j: (j, 0, 0)),
                  pl.BlockSpec((None, 1, 4 * c_blk), lambda b, j: (j, 0, 0)),
                  pl.BlockSpec((2, c_blk), lambda b, j: (0, j))],
        out_specs=pl.BlockSpec((None, rows, w, c_blk), lambda b, j: (b, 0, 0, j)),
        out_shape=jax.ShapeDtypeStruct((bsz, rows, w, d_lru), BF16),
        scratch_shapes=[pltpu.VMEM((2, n, c_blk), F32), pltpu.VMEM((2, n, c_blk), F32),
                        pltpu.VMEM((2, cl, c_blk), F32), pltpu.VMEM((2, cl, c_blk), F32),
                        pltpu.VMEM((w, c_blk), F32)],
        compiler_params=_cparams(("parallel", "parallel")),
        name="lru",
    )(u4, u4, uc3, conv_w, conv_b, wg, bg, lam)


def _mix_kernel(ys5_ref, ylru_ref, x_ref, lng_ref, lnb_ref, g1_ref, sc2_ref, sh2_ref,
                wglu_ref, bglu_ref, wo1_ref, wo2_ref, l1g_ref, l1b_ref, rwt_ref, rb_ref, tri_ref,
                h1_ref, v_ref, eidx_ref, rank_ref, wts_ref, tcnt_ref):
    y = _gelu(ys5_ref[...].astype(F32))
    s5o = y * jax.nn.sigmoid(_dot(y.astype(BF16), wglu_ref[...]) + bglu_ref[...])
    y1 = _dot(s5o.astype(BF16), wo1_ref[...]) + _dot(ylru_ref[...], wo2_ref[...])
    h = _ln(x_ref[...], lng_ref[...], lnb_ref[...])
    h1 = _ln(ALPHA * h + g1_ref[...] * y1, l1g_ref[...], l1b_ref[...])
    h1_ref[...] = h1
    v = h1 * (1.0 + sc2_ref[...]) + sh2_ref[...]
    v_ref[...] = v.astype(v_ref.dtype)

    tm = v.shape[0]
    ne, ng, gsz = N_EXPERTS, N_GROUPS, N_EXPERTS // N_GROUPS
    scores = jax.nn.sigmoid(_dot_nt(rwt_ref[...], v, HIGHEST))
    s3 = scores.reshape(ng, gsz, tm)
    sel3 = (scores + rb_ref[...]).reshape(ng, gsz, tm)
    ii = lax.broadcasted_iota(I32, (ng, gsz, tm), 1)
    gi = lax.broadcasted_iota(I32, (ng, gsz, tm), 0)
    neg = -jnp.inf

    m1 = jnp.max(sel3, axis=1, keepdims=True)
    f1 = jnp.min(jnp.where(sel3 == m1, ii, gsz), axis=1, keepdims=True)
    m2 = jnp.max(jnp.where(ii == f1, neg, sel3), axis=1, keepdims=True)
    cur = m1 + m2
    gidx = lax.broadcasted_iota(I32, (ng, 1, tm), 0)
    gmask = jnp.zeros((ng, 1, tm), jnp.bool_)
    for _ in range(TOPK_GROUPS):
        mx = jnp.max(cur, axis=0, keepdims=True)
        fg = jnp.min(jnp.where(cur == mx, gidx, ng), axis=0, keepdims=True)
        hit = gidx == fg
        gmask = gmask | hit
        cur = jnp.where(hit, neg, cur)

    selm = jnp.where(gmask, sel3, neg)
    eid = gi * gsz + ii
    picks, erows, wrows = [], [], []
    for _ in range(TOP_K):
        mx = jnp.max(jnp.max(selm, axis=1, keepdims=True), axis=0, keepdims=True)
        fe = jnp.min(jnp.min(jnp.where(selm == mx, eid, ne), axis=1, keepdims=True),
                     axis=0, keepdims=True)
        hit = eid == fe
        selm = jnp.where(hit, neg, selm)
        picks.append(hit)
        erows.append(fe.reshape(1, tm))
        wsel = jnp.where(hit, s3, 0.0)
        wrows.append(jnp.sum(jnp.sum(wsel, axis=1, keepdims=True), axis=0).reshape(1, tm))
    denom = wrows[0]
    for k in range(1, TOP_K):
        denom = denom + wrows[k]

    chosen = picks[0]
    for k in range(1, TOP_K):
        chosen = chosen | picks[k]
    chosen_f = jnp.where(chosen, 1.0, 0.0).reshape(ne, tm)
    cnt3 = _dot(chosen_f.astype(BF16), tri_ref[...]).reshape(ng, gsz, tm)
    rrows = []
    for k in range(TOP_K):
        rsel = jnp.where(picks[k], cnt3, 0.0)
        rrows.append(jnp.sum(jnp.sum(rsel, axis=1, keepdims=True), axis=0).reshape(1, tm))
    tcnt_ref[...] = jnp.concatenate(
        [jnp.sum(chosen_f[:, j * TOKEN_TILE:(j + 1) * TOKEN_TILE], axis=1, keepdims=True)
         for j in range(tm // TOKEN_TILE)], axis=1).astype(I32)

    eidx_ref[...] = jnp.concatenate(erows, axis=0)
    rank_ref[...] = jnp.concatenate(rrows, axis=0).astype(I32)
    wts_ref[...] = jnp.concatenate([wr / denom * ROUTED_SCALE for wr in wrows], axis=0)


def _mix(ys5, ylru, x2d, ln_g, ln_b, g1, sc2, sh2, wglu, bglu, wo1, wo2, l1g, l1b, rwt, rb, seq, tm=512):
    t, d = x2d.shape
    ds = ys5.shape[1]
    ne = rwt.shape[0]
    nsub = tm // TOKEN_TILE
    r_i = lax.broadcasted_iota(I32, (tm, tm), 0)
    c_i = lax.broadcasted_iota(I32, (tm, tm), 1)
    tri = ((r_i < c_i) & (r_i // TOKEN_TILE == c_i // TOKEN_TILE)).astype(BF16)
    row = lambda n: pl.BlockSpec((tm, n), lambda i: (i, 0))
    vec = lambda n: pl.BlockSpec((1, n), lambda i: (0, 0))
    mod = pl.BlockSpec((None, 1, d), lambda i: ((i * tm) // seq, 0, 0))
    full = lambda a, b: pl.BlockSpec((a, b), lambda i: (0, 0))
    tok = pl.BlockSpec((TOP_K, tm), lambda i: (0, i))
    return pl.pallas_call(
        _mix_kernel,
        grid=(t // tm,),
        in_specs=[row(ds), row(ds), row(d), vec(d), vec(d), mod, mod, mod,
                  full(ds, ds), vec(ds), full(ds, d), full(ds, d), vec(d), vec(d),
                  full(ne, d), full(ne, 1), full(tm, tm)],
        out_specs=[row(d), row(d), tok, tok, tok,
                   pl.BlockSpec((None, ne, nsub), lambda i: (i, 0, 0))],
        out_shape=[jax.ShapeDtypeStruct((t, d), F32), jax.ShapeDtypeStruct((t, d), BF16),
                   jax.ShapeDtypeStruct((TOP_K, t), I32), jax.ShapeDtypeStruct((TOP_K, t), I32),
                   jax.ShapeDtypeStruct((TOP_K, t), F32),
                   jax.ShapeDtypeStruct((t // tm, ne, nsub), I32)],
        compiler_params=_cparams(("parallel",)),
        name="mix",
    )(ys5, ylru, x2d, ln_g.reshape(1, d), ln_b.reshape(1, d), g1, sc2, sh2,
      wglu, bglu.reshape(1, ds), wo1, wo2, l1g.reshape(1, d), l1b.reshape(1, d), rwt, rb, tri)


def _plan_kernel(tc_ref, soff_ref, utot_ref, start_ref, total_ref, pieces_ref, npieces_ref):
    tc = tc_ref[...]
    ne, nt = tc.shape
    unit_shift = SEG_ALIGN.bit_length() - 1
    units = lax.shift_right_logical(tc + (SEG_ALIGN - 1), unit_shift)
    seg = (units * SEG_ALIGN).astype(F32)
    earlier_exp = jnp.where(lax.broadcasted_iota(I32, (ne, ne), 0) > lax.broadcasted_iota(I32, (ne, ne), 1),
                            1.0, 0.0)
    earlier_tile = jnp.where(lax.broadcasted_iota(I32, (nt, nt), 0) < lax.broadcasted_iota(I32, (nt, nt), 1),
                             1.0, 0.0)

    def exact(a, b):
        return jnp.dot(a, b, precision=HIGHEST, preferred_element_type=F32)

    soff = exact(earlier_exp, seg)
    total = jnp.sum(seg, axis=1, keepdims=True)
    start = jnp.sum(soff, axis=1, keepdims=True)
    dst = exact(seg, earlier_tile) + start
    soff_ref[...] = soff.astype(I32)
    utot_ref[...] = (jnp.sum(seg, axis=0, keepdims=True) * (1.0 / SEG_ALIGN)).astype(I32)
    start_ref[...] = start.astype(I32)
    total_ref[...] = total.astype(I32)

    counts = []
    for k in range(PIECE_SIZES):
        has = jnp.bitwise_and(lax.shift_right_logical(units, k), 1)
        below = (jnp.bitwise_and(units, (1 << k) - 1) * SEG_ALIGN).astype(F32)
        pos = exact(earlier_exp, has.astype(F32))
        counts.append(jnp.sum(has, axis=0, keepdims=True))
        rows = []
        for j in range(ne):
            sel = (has > 0) & (pos == float(j))
            srow = jnp.sum(jnp.where(sel, soff + below, 0.0), axis=0, keepdims=True).astype(I32)
            drow = jnp.sum(jnp.where(sel, dst + below, 0.0), axis=0, keepdims=True).astype(I32)
            rows.append(jnp.bitwise_or(lax.shift_left(srow, PIECE_PACK.bit_length() - 1), drow))
        pieces_ref[k * ne:(k + 1) * ne, :] = jnp.concatenate(rows, axis=0)
    counts += [jnp.zeros((1, nt), I32)] * (npieces_ref.shape[0] - PIECE_SIZES)
    npieces_ref[...] = jnp.concatenate(counts, axis=0)


def _plan(tc_t, n_rows):
    ne, nt = tc_t.shape
    assert n_rows <= PIECE_PACK
    full = lambda *s: pl.BlockSpec(s, lambda: (0,) * len(s))
    return pl.pallas_call(
        _plan_kernel,
        in_specs=[full(ne, nt)],
        out_specs=[full(ne, nt), full(1, nt), full(ne, 1), full(ne, 1), full(PIECE_SIZES * ne, nt),
                   full(8, nt)],
        out_shape=[jax.ShapeDtypeStruct((ne, nt), I32), jax.ShapeDtypeStruct((1, nt), I32),
                   jax.ShapeDtypeStruct((ne, 1), I32), jax.ShapeDtypeStruct((ne, 1), I32),
                   jax.ShapeDtypeStruct((PIECE_SIZES * ne, nt), I32), jax.ShapeDtypeStruct((8, nt), I32)],
        name="plan",
    )(tc_t)


def _visits_kernel(start_ref, total_ref, tail_ref, blk_ref, xblk_ref, exp_ref, lo_ref, hi_ref,
                   *, n_blocks, nv):
    ne = N_EXPERTS
    shift = ROW_BLOCK.bit_length() - 1

    def put(pos, blk, xblk, e, lo, hi):
        blk_ref[pos] = blk
        xblk_ref[pos] = xblk
        exp_ref[pos] = e
        lo_ref[pos] = lo
        hi_ref[pos] = hi

    def per_expert(e, carry):
        pos, done = carry
        off = start_ref[e]
        cnt = total_ref[e]
        first = lax.shift_right_logical(off, shift)
        last = lax.shift_right_logical(off + cnt - 1, shift)
        nvis = jnp.where(cnt > 0, last - first + 1, 0)

        def put_vis(k, c):
            put(pos + k, first + k, first + k, e, off, off + cnt)
            return c

        lax.fori_loop(0, nvis, put_vis, 0)
        return pos + nvis, jnp.where(cnt > 0, last + 1, done)

    pos, done = lax.fori_loop(0, ne, per_expert, (jnp.int32(0), jnp.int32(0)))
    tail_ref[0] = start_ref[ne - 1] + total_ref[ne - 1]

    def put_tail(k, c):
        put(pos + k, done + k, 0, ne - 1, 0, 0)
        return c

    lax.fori_loop(0, n_blocks - done, put_tail, 0)

    def put_rest(j, c):
        put(j, n_blocks - 1, 0, ne - 1, 0, 0)
        return c

    lax.fori_loop(pos + (n_blocks - done), nv, put_rest, 0)


def _sorted_rows(n_tokens):
    nt = n_tokens // TOKEN_TILE
    rows = n_tokens * TOP_K + nt * N_EXPERTS * (SEG_ALIGN - 1)
    return -(-rows // ROW_BLOCK) * ROW_BLOCK


def _visits(start, total, n_rows):
    ne = start.shape[0]
    n_blocks = n_rows // ROW_BLOCK
    nv = n_blocks + ne
    smem = pl.BlockSpec(memory_space=pltpu.SMEM)
    vec = jax.ShapeDtypeStruct((nv,), I32)
    return pl.pallas_call(
        functools.partial(_visits_kernel, n_blocks=n_blocks, nv=nv),
        in_specs=[smem, smem],
        out_specs=[smem] * 6,
        out_shape=[jax.ShapeDtypeStruct((1,), I32), vec, vec, vec, vec, vec],
        name="visits",
    )(start, total)


def _start_segment_copies(pieces_ref, npieces_ref, tile, hbm_ref, buf, to_hbm, sem):
    for k in range(PIECE_SIZES):
        size = SEG_ALIGN << k

        def body(j, c, k=k, size=size):
            packed = pieces_ref[k * N_EXPERTS + j, tile]
            srow = pl.multiple_of(lax.shift_right_logical(packed, PIECE_PACK.bit_length() - 1), SEG_ALIGN)
            drow = pl.multiple_of(jnp.bitwise_and(packed, PIECE_PACK - 1), SEG_ALIGN)
            slot_rows = buf.at[pl.ds(srow, size), :]
            sorted_rows = hbm_ref.at[pl.ds(drow, size), :]
            src, dst = (slot_rows, sorted_rows) if to_hbm else (sorted_rows, slot_rows)
            pltpu.make_async_copy(src, dst, sem).start(priority=k % 2)
            return c

        lax.fori_loop(0, npieces_ref[k, tile], body, 0)


def _wait_segment_copies(units, hbm_ref, buf, to_hbm, sem):
    for k in range(NSLOT.bit_length() - SEG_ALIGN.bit_length() + 1):
        @pl.when(jnp.bitwise_and(units, 1 << k) != 0)
        def _(k=k):
            size = SEG_ALIGN << k
            slot_rows = buf.at[pl.ds(0, size), :]
            sorted_rows = hbm_ref.at[pl.ds(0, size), :]
            src, dst = (slot_rows, sorted_rows) if to_hbm else (sorted_rows, slot_rows)
            pltpu.make_async_copy(src, dst, sem).wait()


def _dispatch_kernel(pieces_ref, npieces_ref, soff_ref, utot_ref, tail_ref, eidx_ref, lr_ref, v_ref,
                     xs_ref, srow_ref, cbuf, zbuf, sems, zsem):
    i = pl.program_id(0)
    nt = pl.num_programs(0)
    tm = v_ref.shape[0]
    s = i % 2

    @pl.when(i >= 2)
    def _():
        _wait_segment_copies(utot_ref[0, i - 2], xs_ref, cbuf.at[s], True, sems.at[s])

    e8 = eidx_ref[...]
    base = jnp.zeros_like(e8)
    for e in range(N_EXPERTS):
        base = jnp.where(e8 == e, soff_ref[e, i], base)
    tr = base + lr_ref[...]
    srow_ref[...] = tr

    vb = v_ref[...]
    for c0 in range(0, NSLOT, SLOT_CHUNK):
        rows = lax.broadcasted_iota(I32, (SLOT_CHUNK, tm), 0) + c0
        onehot = jnp.zeros((SLOT_CHUNK, tm), F32)
        for k in range(TOP_K):
            onehot = jnp.where(rows == tr[k:k + 1, :], 1.0, onehot)
        cbuf[s, c0:c0 + SLOT_CHUNK, :] = _dot(onehot.astype(BF16), vb).astype(BF16)
    _start_segment_copies(pieces_ref, npieces_ref, i, xs_ref, cbuf.at[s], True, sems.at[s])

    @pl.when(i == nt - 1)
    def _():
        @pl.when(i >= 1)
        def _():
            _wait_segment_copies(utot_ref[0, i - 1], xs_ref, cbuf.at[1 - s], True, sems.at[1 - s])

        _wait_segment_copies(utot_ref[0, i], xs_ref, cbuf.at[s], True, sems.at[s])
        zbuf[...] = jnp.zeros_like(zbuf)

        def zero_rows(start, size):
            rows = pl.ds(pl.multiple_of(start, SEG_ALIGN), size)
            return pltpu.make_async_copy(zbuf.at[pl.ds(0, size), :], xs_ref.at[rows, :], zsem)

        tail = tail_ref[0]
        n_small = lax.shift_right_logical(jnp.bitwise_and(-tail, ROW_BLOCK - 1),
                                          SEG_ALIGN.bit_length() - 1)
        tail_blk = tail + n_small * SEG_ALIGN
        n_big = lax.shift_right_logical(xs_ref.shape[0] - tail_blk, ROW_BLOCK.bit_length() - 1)

        def each(fn):
            lax.fori_loop(0, n_small, lambda q, c: fn(zero_rows(tail + q * SEG_ALIGN, SEG_ALIGN), c), 0)
            lax.fori_loop(0, n_big, lambda q, c: fn(zero_rows(tail_blk + q * ROW_BLOCK, ROW_BLOCK), c), 0)

        each(lambda cp, c: (cp.start(), c)[1])
        each(lambda cp, c: (cp.wait(), c)[1])


def _dispatch(pieces, npieces, soff, utot, tail, eidx, lrank, v, n_rows):
    t, d = v.shape
    tm = TOKEN_TILE
    tok = pl.BlockSpec((TOP_K, tm), lambda i, *_: (0, i))
    return pl.pallas_call(
        _dispatch_kernel,
        grid_spec=pltpu.PrefetchScalarGridSpec(
            num_scalar_prefetch=5, grid=(t // tm,),
            in_specs=[tok, tok, pl.BlockSpec((tm, d), lambda i, *_: (i, 0))],
            out_specs=[pl.BlockSpec(memory_space=pl.ANY), tok],
            scratch_shapes=[pltpu.VMEM((2, NSLOT, d), BF16), pltpu.VMEM((ROW_BLOCK, d), BF16),
                            pltpu.SemaphoreType.DMA((2,)), pltpu.SemaphoreType.DMA]),
        out_shape=[jax.ShapeDtypeStruct((n_rows, d), BF16), jax.ShapeDtypeStruct((TOP_K, t), I32)],
        compiler_params=_cparams(("arbitrary",)),
        name="dispatch",
    )(pieces, npieces, soff, utot, tail, eidx, lrank, v)


def _gmm_kernel(vb_ref, vx_ref, ve_ref, vlo_ref, vhi_ref, x_ref, wg_ref, wu_ref, wd_ref, y_ref,
                wg_b, wu_b, wd_b):
    j = pl.program_id(0)
    prev = jnp.maximum(j - 1, 0)
    first = jnp.logical_or(j == 0, vb_ref[prev] != vb_ref[j])
    live = vhi_ref[j] > vlo_ref[j]

    @pl.when(jnp.logical_or(j == 0, ve_ref[prev] != ve_ref[j]))
    def _():
        wg_b[...] = wg_ref[...].astype(BF16)
        wu_b[...] = wu_ref[...].astype(BF16)
        wd_b[...] = wd_ref[...].astype(BF16)

    @pl.when(live)
    def _():
        sub = ROW_BLOCK // GMM_SPLIT
        parts = []
        for r0 in range(0, ROW_BLOCK, sub):
            x = x_ref[r0:r0 + sub, :]
            hidden = _silu(_dot(x, wg_b[...])) * _dot(x, wu_b[...])
            y = _dot(hidden.astype(BF16), wd_b[...])
            rows = vb_ref[j] * ROW_BLOCK + r0 + lax.broadcasted_iota(I32, (sub, 1), 0)
            parts.append((r0, y, (rows >= vlo_ref[j]) & (rows < vhi_ref[j])))

        @pl.when(first)
        def _():
            for r0, y, mine in parts:
                y_ref[r0:r0 + sub, :] = jnp.where(mine, y, 0.0).astype(y_ref.dtype)

        @pl.when(jnp.logical_not(first))
        def _():
            for r0, y, mine in parts:
                keep = y_ref[r0:r0 + sub, :].astype(F32)
                y_ref[r0:r0 + sub, :] = jnp.where(mine, y, keep).astype(y_ref.dtype)

    @pl.when(jnp.logical_and(jnp.logical_not(live), first))
    def _():
        y_ref[...] = jnp.zeros_like(y_ref)


def _gmm(vblock, vxblock, vexp, vlo, vhi, xs, wg, wu, wd):
    n, d = xs.shape
    de = wg.shape[2]
    nv = vblock.shape[0]
    return pl.pallas_call(
        _gmm_kernel,
        grid_spec=pltpu.PrefetchScalarGridSpec(
            num_scalar_prefetch=5, grid=(nv,),
            in_specs=[pl.BlockSpec((ROW_BLOCK, d), lambda j, vb, vx, ve, lo, hi: (vx[j], 0)),
                      pl.BlockSpec((None, d, de), lambda j, vb, vx, ve, lo, hi: (ve[j], 0, 0)),
                      pl.BlockSpec((None, d, de), lambda j, vb, vx, ve, lo, hi: (ve[j], 0, 0)),
                      pl.BlockSpec((None, de, d), lambda j, vb, vx, ve, lo, hi: (ve[j], 0, 0))],
            out_specs=pl.BlockSpec((ROW_BLOCK, d), lambda j, vb, vx, ve, lo, hi: (vb[j], 0)),
            scratch_shapes=[pltpu.VMEM((d, de), BF16), pltpu.VMEM((d, de), BF16),
                            pltpu.VMEM((de, d), BF16)]),
        out_shape=jax.ShapeDtypeStruct((n, d), BF16),
        compiler_params=_cparams(("arbitrary",)),
        name="gmm",
    )(vblock, vxblock, vexp, vlo, vhi, xs, wg, wu, wd)


def _combine_kernel(pieces_ref, npieces_ref, utot_ref, srow_ref, w_ref, v_ref, h1_ref, g2_ref,
                    shg_ref, shu_ref, shd_ref, l2g_ref, l2b_ref, ys_ref, o_ref, ybuf, sems):
    i = pl.program_id(0)
    nt = pl.num_programs(0)
    tm, d = v_ref.shape
    cur = i % 2

    def fetch(tile, b):
        _start_segment_copies(pieces_ref, npieces_ref, tile, ys_ref, ybuf.at[b], False, sems.at[b])

    @pl.when(i == 0)
    def _():
        ybuf[...] = jnp.zeros_like(ybuf)
        fetch(0, 0)

    @pl.when(i + 1 < nt)
    def _():
        fetch(i + 1, 1 - cur)

    vb = v_ref[...]
    hidden = _silu(_dot(vb, shg_ref[...])) * _dot(vb, shu_ref[...])
    f = _dot(hidden.astype(BF16), shd_ref[...])

    tr = srow_ref[...]
    w8 = w_ref[...]
    _wait_segment_copies(utot_ref[0, i], ys_ref, ybuf.at[cur], False, sems.at[cur])
    for c0 in range(0, NSLOT, FOLD_CHUNK):
        lanes = lax.broadcasted_iota(I32, (tm, FOLD_CHUNK), 1) + c0
        pw = jnp.zeros((tm, FOLD_CHUNK), F32)
        for k in range(TOP_K):
            pw = jnp.where(lanes == tr[:, k:k + 1], w8[:, k:k + 1], pw)
        f = f + _dot(pw.astype(BF16), ybuf[cur, c0:c0 + FOLD_CHUNK, :])
    o_ref[...] = _ln(ALPHA * h1_ref[...] + g2_ref[...] * f, l2g_ref[...], l2b_ref[...])


def _combine(pieces, npieces, utot, srow_t, wts_t, v, h1, g2, shg, shu, shd, l2g, l2b, ys, seq):
    t, d = v.shape
    tm = TOKEN_TILE
    dsh = shg.shape[1]
    row = pl.BlockSpec((tm, d), lambda i, *_: (i, 0))
    vec = pl.BlockSpec((1, d), lambda i, *_: (0, 0))
    tok = pl.BlockSpec((tm, TOP_K), lambda i, *_: (i, 0))
    return pl.pallas_call(
        _combine_kernel,
        grid_spec=pltpu.PrefetchScalarGridSpec(
            num_scalar_prefetch=3, grid=(t // tm,),
            in_specs=[tok, tok, row, row,
                      pl.BlockSpec((None, 1, d), lambda i, *_: ((i * tm) // seq, 0, 0)),
                      pl.BlockSpec((d, dsh), lambda i, *_: (0, 0)),
                      pl.BlockSpec((d, dsh), lambda i, *_: (0, 0)),
                      pl.BlockSpec((dsh, d), lambda i, *_: (0, 0)), vec, vec,
                      pl.BlockSpec(memory_space=pl.ANY)],
            out_specs=row,
            scratch_shapes=[pltpu.VMEM((2, NSLOT, d), BF16), pltpu.SemaphoreType.DMA((2,))]),
        out_shape=jax.ShapeDtypeStruct((t, d), F32),
        compiler_params=_cparams(("arbitrary",)),
        name="combine",
    )(pieces, npieces, utot, srow_t, wts_t, v, h1, g2, shg, shu, shd,
      l2g.reshape(1, d), l2b.reshape(1, d), ys)


def _quad(a0, a1):
    return jnp.concatenate([a0, a1, a0, a1], axis=-1)


def kernel(x, c, ctx, c_ctx, ln_in_g, ln_in_b, ada_w, ada_b, w_in, s5_lam_re, s5_lam_im, s5_log_dt, s5_b_re, s5_b_im, s5_c_re, s5_c_im, s5_d, s5_w_glu, s5_b_glu, lru_conv_w, lru_conv_b, lru_w_a, lru_b_a, lru_w_x, lru_b_x, lru_lam, w_out, ln1_g, ln1_b, router_w, router_bias, exp_w_gate, exp_w_up, exp_w_down, sh_w_gate, sh_w_up, sh_w_down, ln2_g, ln2_b):
    bsz, seq, dm = x.shape
    cl = ctx.shape[1]
    assert ada_w.shape[0] == DEPTH
    d_s5 = s5_w_glu.shape[1]
    d_lru = lru_lam.shape[2]
    ngrp = d_s5 // S5_GROUP
    rows = seq // GRID_W
    t = bsz * seq
    lc = S5_CHUNK
    nkl, nkc = seq // lc, cl // lc

    pad = (-(bsz + 1)) % 8
    cc = jnp.concatenate([c, c_ctx[None, :], jnp.zeros((pad, dm), F32)], axis=0)
    mods = _ada(cc, ada_w[0], ada_b[0])
    sh1, sc1, g1, sh2, sc2, g2 = [mods[:bsz, k * dm:(k + 1) * dm].reshape(bsz, 1, dm) for k in range(6)]
    csh1, csc1 = [mods[bsz:bsz + 1, k * dm:(k + 1) * dm].reshape(1, 1, dm) for k in range(2)]

    x2d = x.reshape(t, dm)
    w_in_b = w_in[0].astype(BF16)
    us5, ulru = _in_proj(x2d, ln_in_g, ln_in_b, sc1, sh1, w_in_b, seq, d_s5)
    ucs5, uclru = _in_proj(ctx.reshape(bsz * cl, dm), ln_in_g, ln_in_b, csc1, csh1, w_in_b, bsz * cl, d_s5)

    def to_chunks(u, nk):
        u = u.reshape(bsz, nk, lc, ngrp, S5_GROUP).transpose(3, 1, 0, 2, 4)
        return u.reshape(ngrp, nk * bsz, lc * S5_GROUP)

    lam3 = jnp.stack([_quad(s5_lam_re[0, 0], s5_lam_re[0, 1]), _quad(s5_lam_im[0, 0], s5_lam_im[0, 1]),
                      _quad(*[jnp.broadcast_to(s5_log_dt[0, k][:, None], (ngrp, S5_STATE)) for k in range(2)])],
                     axis=1)
    bt_re = jnp.swapaxes(s5_b_re[0], -1, -2)
    bt_im = jnp.swapaxes(s5_b_im[0], -1, -2)
    tt, win, wout, a16 = _s5_prep(lam3, _quad(bt_re[0], bt_re[1]), _quad(bt_im[0], bt_im[1]),
                                  _quad(s5_c_re[0, 0], s5_c_re[0, 1]), _quad(s5_c_im[0, 0], s5_c_im[0, 1]),
                                  s5_d[0].reshape(ngrp, S5_GROUP, 1))
    ys = _s5(to_chunks(us5, nkl), to_chunks(ucs5, nkc), tt, win, wout, a16, bsz)
    ys5 = ys.reshape(ngrp, nkl, bsz, lc, S5_GROUP).transpose(2, 1, 3, 0, 4).reshape(t, d_s5)

    c_blk = 256
    hd = d_lru // LRU_HEADS
    hpb = c_blk // hd
    ncb = d_lru // c_blk

    def blockdiag(wh):
        wh = wh.reshape(ncb, hpb, hd, hd)
        eye = jnp.eye(hpb, dtype=wh.dtype)
        return jnp.einsum("nhij,hk->nhikj", wh, eye).reshape(ncb, c_blk, c_blk)

    wg = jnp.concatenate([blockdiag(lru_w_a[0, 0]), blockdiag(lru_w_x[0, 0]),
                          blockdiag(lru_w_a[0, 1]), blockdiag(lru_w_x[0, 1])], axis=-1).astype(BF16)
    bgate = jnp.concatenate([lru_b_a[0, 0].reshape(ncb, 1, c_blk), lru_b_x[0, 0].reshape(ncb, 1, c_blk),
                             lru_b_a[0, 1].reshape(ncb, 1, c_blk), lru_b_x[0, 1].reshape(ncb, 1, c_blk)], axis=-1)
    ylru = _lru(ulru.reshape(bsz, rows, GRID_W, 2 * d_lru), uclru.reshape(bsz, cl, 2 * d_lru),
                lru_conv_w[0], lru_conv_b[0].reshape(1, d_lru), wg, bgate, lru_lam[0], c_blk)
    ylru = ylru.reshape(t, d_lru)

    w_out_b = w_out[0].astype(BF16)
    h1, v, eidx, lrank, wts, tcnt = _mix(
        ys5, ylru, x2d, ln_in_g, ln_in_b, g1, sc2, sh2, s5_w_glu[0].astype(BF16), s5_b_glu[0],
        w_out_b[:d_s5], w_out_b[d_s5:], ln1_g[0], ln1_b[0], router_w[0].T,
        router_bias[0].reshape(N_EXPERTS, 1), seq)

    n_rows = _sorted_rows(t)
    tc_t = tcnt.transpose(1, 0, 2).reshape(N_EXPERTS, t // TOKEN_TILE)
    soff, utot, start, total, pieces, npieces = _plan(tc_t, n_rows)
    tail, vblock, vxblock, vexp, vlo, vhi = _visits(start.reshape(-1), total.reshape(-1), n_rows)
    xs, srow = _dispatch(pieces, npieces, soff, utot, tail, eidx, lrank, v, n_rows)
    ysort = _gmm(vblock, vxblock, vexp, vlo, vhi, xs, exp_w_gate[0], exp_w_up[0], exp_w_down[0])
    out = _combine(pieces, npieces, utot, srow.T, wts.T, v, h1, g2, sh_w_gate[0].astype(BF16),
                   sh_w_up[0].astype(BF16), sh_w_down[0].astype(BF16), ln2_g[0], ln2_b[0], ysort, seq)
    return out.reshape(bsz, seq, dm).astype(x.dtype)
```
